```python
import math
import jax
import jax.numpy as jnp
from jax import lax
import numpy as np

D_MODEL = 2048
BATCH = 4
SEQ = 2048
DEPTH = 2
DEC_BATCH = 128
DEC_SEQ = 8
PAST_LEN = 16384
PAGE_SIZE = 128

N_META = 16
N_EVEN = (DEPTH + 1) // 2
N_ODD = DEPTH // 2

A_WIDTH = D_MODEL // 2
A_HEADS = 4
A_DV = A_WIDTH // A_HEADS
A_DK = A_DV // 2
A_GATE_RANK = 16
A_GATE_TAU = 16.0
A_CHUNK = 64
A_COLS = 2 * A_HEADS * A_DK + 2 * A_WIDTH + A_GATE_RANK

B_WIDTH = D_MODEL // 2
B_HEAD = 64
B_HEADS = B_WIDTH // B_HEAD
B_DECAY_RANK = 64
B_AAA_RANK = 64
B_GATE_RANK = 128
B_COLS = 3 * B_WIDTH + B_DECAY_RANK + B_AAA_RANK + B_GATE_RANK

C_WIDTH = D_MODEL
C_GROUP = 16
C_GROUPS = C_WIDTH // C_GROUP
C_STATE = 64

N_EXPERTS = 16
N_EXPERT_GROUPS = 4
EXPERTS_PER_GROUP = N_EXPERTS // N_EXPERT_GROUPS
TOP_K = 2
D_EXPERT = D_MODEL // 2

ALPHA = (2.0 * DEPTH) ** 0.25
BETA = (8.0 * DEPTH) ** -0.25
LN_EPS = 1e-5
HEAD_NORM_EPS = 1e-5
RWKV_GN_EPS = 64e-5

kernel_name = 'hybrid_gla_rwkv7_s5_moe_step'


def layer_norm(x, g, b):
    xf = x.astype(jnp.float32)
    mu = jnp.mean(xf, -1, keepdims=True)
    xc = xf - mu
    var = jnp.mean(xc * xc, -1, keepdims=True)
    y = xc * lax.rsqrt(var + LN_EPS) * g.astype(jnp.float32) + b.astype(jnp.float32)
    return y.astype(x.dtype)


def gla_segment(q, k, v, lg, s0, chunk):
    bn, L, H, DK = q.shape
    DV = v.shape[-1]
    n = L // chunk

    def to_chunks(t):
        return t.reshape(bn, n, chunk, H, t.shape[-1]).transpose(1, 0, 2, 3, 4)

    qc, kc, vc, gc = to_chunks(q), to_chunks(k), to_chunks(v), to_chunks(lg)
    mask = jnp.tril(jnp.ones((chunk, chunk), dtype=bool))[None, :, :, None, None]

    def step(S, inp):
        qi, ki, vi, gi = inp
        b = jnp.cumsum(gi, axis=1)
        diff = b[:, :, None] - b[:, None, :]
        dec = jnp.exp(jnp.where(mask, diff, -jnp.inf))
        att = jnp.einsum('bihd,bjhd,bijhd->bhij', qi, ki, dec)
        o = (jnp.einsum('bhij,bjhv->bihv', att, vi)
             + jnp.einsum('bihd,bhdv->bihv', qi * jnp.exp(b), S))
        bl = b[:, -1]
        S = (jnp.exp(bl)[..., None] * S
             + jnp.einsum('bjhd,bjhv->bhdv', ki * jnp.exp(bl[:, None] - b), vi))
        return S, o

    S, o = lax.scan(step, s0, (qc, kc, vc, gc))
    o = o.transpose(1, 0, 2, 3, 4).reshape(bn, L, H, DV)
    return o, S


def gla_mixer(pa, gate_up, gate_b, norm_g, s0, segments):
    bn, L, _ = pa.shape
    qk = A_HEADS * A_DK
    q, k, v, r, gd = jnp.split(pa, [qk, 2 * qk, 2 * qk + A_WIDTH, 2 * qk + 2 * A_WIDTH], axis=-1)
    q = q.reshape(bn, L, A_HEADS, A_DK) * (A_DK ** -0.5)
    k = k.reshape(bn, L, A_HEADS, A_DK)
    v = v.reshape(bn, L, A_HEADS, A_DV)
    lg = (jax.nn.log_sigmoid(gd @ gate_up + gate_b) / A_GATE_TAU).reshape(bn, L, A_HEADS, A_DK)
    s = s0.astype(jnp.float32)
    outs = []
    start = 0
    for seg in segments:
        sl = slice(start, start + seg)
        o, s = gla_segment(q[:, sl], k[:, sl], v[:, sl], lg[:, sl], s, math.gcd(seg, A_CHUNK))
        outs.append(o)
        start += seg
    o = jnp.concatenate(outs, axis=1) if len(outs) > 1 else outs[0]
    o = o * lax.rsqrt(jnp.mean(o * o, -1, keepdims=True) + HEAD_NORM_EPS)
    o = o.reshape(bn, L, A_WIDTH) * norm_g * jax.nn.silu(r)
    return o, s


def rwkv7_scan(r, w_log, k, v, kk, a, s0):
    def step(S, inp):
        rt, wt, kt, vt, kkt, at = inp
        sa = jnp.einsum('bhvk,bhk->bhv', S, kkt)
        S = (S * jnp.exp(wt)[:, :, None, :]
             - sa[..., None] * (kkt * at)[:, :, None, :]
             + vt[..., None] * kt[:, :, None, :])
        o = jnp.einsum('bhvk,bhk->bhv', S, rt)
        return S, o

    xs = tuple(t.transpose(1, 0, 2, 3) for t in (r, w_log, k, v, kk, a))
    S, o = lax.scan(step, s0, xs)
    return o.transpose(1, 0, 2, 3), S


def rwkv7_mixer(pb, mu, w0, w_up, a0, a_up, g_up, k_k, k_a, r_k, ln_g, ln_b, s0, shift0):
    bn, L, _ = pb.shape
    W = B_WIDTH
    prev = jnp.concatenate([shift0.astype(jnp.float32)[:, None], pb[:, :-1]], axis=1)
    xm = pb + (prev - pb) * mu
    r, k, v, wd, ad, gd = jnp.split(
        xm, [W, 2 * W, 3 * W, 3 * W + B_DECAY_RANK, 3 * W + B_DECAY_RANK + B_AAA_RANK], axis=-1)
    w = -jax.nn.softplus(-(w0 + jnp.tanh(wd) @ w_up)) - 0.5
    w_log = -jnp.exp(w)
    a = jax.nn.sigmoid(a0 + ad @ a_up)
    g = jax.nn.sigmoid(gd) @ g_up

    def heads(t):
        return t.reshape(bn, L, B_HEADS, B_HEAD)

    kk = heads(k * k_k)
    kk = kk / jnp.maximum(jnp.linalg.norm(kk, axis=-1, keepdims=True), 1e-12)
    k = k * (1.0 + (a - 1.0) * k_a)
    r_h, k_h, v_h, a_h, w_h = heads(r), heads(k), heads(v), heads(a), heads(w_log)
    o, s_new = rwkv7_scan(r_h, w_h, k_h, v_h, kk, a_h, s0.astype(jnp.float32))
    mu_o = jnp.mean(o, -1, keepdims=True)
    oc = o - mu_o
    o = oc * lax.rsqrt(jnp.mean(oc * oc, -1, keepdims=True) + RWKV_GN_EPS)
    o = o.reshape(bn, L, W) * ln_g + ln_b
    bonus = jnp.sum(r_h * k_h * r_k.reshape(B_HEADS, B_HEAD), -1, keepdims=True) * v_h
    o = (o + bonus.reshape(bn, L, W)) * g
    return o, s_new, pb[:, -1]


def even_mixer(x, w_in, w_out, a_gate_up, a_gate_b, a_norm_g, b_mu, b_w0, b_w_up, b_a0,
               b_a_up, b_g_up, b_k_k, b_k_a, b_r_k, b_ln_g, b_ln_b, s_gla, s_rwkv, s_shift,
               segments):
    p = jnp.einsum('bld,dc->blc', x, w_in).astype(jnp.float32)
    o_a, s_gla_new = gla_mixer(p[..., :A_COLS], a_gate_up, a_gate_b, a_norm_g, s_gla, segments)
    o_b, s_rwkv_new, s_shift_new = rwkv7_mixer(
        p[..., A_COLS:], b_mu, b_w0, b_w_up, b_a0, b_a_up, b_g_up, b_k_k, b_k_a, b_r_k,
        b_ln_g, b_ln_b, s_rwkv, s_shift)
    o = jnp.concatenate([o_a, o_b], axis=-1).astype(x.dtype)
    return jnp.einsum('blc,cd->bld', o, w_out), s_gla_new, s_rwkv_new, s_shift_new


def _s5_combine(e1, e2):
    a1, b1 = e1
    a2, b2 = e2
    return a1 * a2, a2 * b1 + b2


def s5_mixer(x, w_in, a_re, a_im, log_dt, b_re, b_im, c_re, c_im, d, w_glu, b_glu, w_out,
             s_re, s_im):
    f32 = jnp.float32
    bn, L, _ = x.shape
    u = jnp.einsum('bld,dc->blc', x, w_in).astype(f32)
    lam = lax.complex(a_re.astype(f32), a_im.astype(f32))
    dt = jnp.exp(log_dt.astype(f32))[:, None]
    a_bar = jnp.exp(lam * dt)
    b_bar = ((a_bar - 1.0) / lam)[..., None] * lax.complex(b_re.astype(f32), b_im.astype(f32))
    ug = u.reshape(bn, L, C_GROUPS, C_GROUP).transpose(1, 0, 2, 3)
    bu = jnp.einsum('lbgc,gpc->lbgp', ug.astype(jnp.complex64), b_bar)
    x0 = lax.complex(s_re.astype(f32), s_im.astype(f32))
    bu = bu.at[0].add(a_bar * x0)
    a_seq = jnp.broadcast_to(a_bar, (L, 1) + a_bar.shape)
    _, states = lax.associative_scan(_s5_combine, (a_seq, bu), axis=0)
    c = lax.complex(c_re.astype(f32), c_im.astype(f32))
    y = jnp.real(jnp.einsum('lbgp,gcp->lbgc', states, c)) + d.astype(f32).reshape(C_GROUPS, C_GROUP) * ug
    y = y.transpose(1, 0, 2, 3).reshape(bn, L, C_WIDTH)
    z = jax.nn.gelu(y)
    z = z * jax.nn.sigmoid(z @ w_glu + b_glu)
    out = jnp.einsum('blc,cd->bld', z.astype(x.dtype), w_out)
    last = states[-1]
    return out, jnp.real(last), jnp.imag(last)


def moe(x, w_router, w_up, w_down):
    bn, L, D = x.shape
    t = x.reshape(bn * L, D)
    probs = jax.nn.softmax(jnp.einsum('td,de->te', t, w_router).astype(jnp.float32), axis=-1)
    pg = probs.reshape(-1, N_EXPERT_GROUPS, EXPERTS_PER_GROUP)
    group_score = jnp.sum(lax.top_k(pg, TOP_K)[0], axis=-1)
    g_sel = jnp.argmax(group_score, axis=-1)
    in_group = jnp.einsum('tg,tge->te', jax.nn.one_hot(g_sel, N_EXPERT_GROUPS, dtype=jnp.float32), pg)
    vals, idx = lax.top_k(in_group, TOP_K)
    vals = vals / jnp.sum(vals, -1, keepdims=True)
    eidx = g_sel[:, None] * EXPERTS_PER_GROUP + idx
    gate = jnp.sum(jax.nn.one_hot(eidx, N_EXPERTS, dtype=jnp.float32) * vals[..., None], axis=1)
    h = jnp.einsum('td,edf->tef', t, w_up)
    h = jax.nn.silu(h[..., :D_EXPERT]) * h[..., D_EXPERT:] * gate[..., None].astype(h.dtype)
    y = jnp.einsum('tef,efd->td', h, w_down)
    return y.reshape(bn, L, D)


def trunk(x, s_gla, s_rwkv, s_shift, s_re, s_im, segments, p):
    gla_new, rwkv_new, shift_new, re_new, im_new = [], [], [], [], []
    for layer in range(DEPTH):
        i = layer // 2
        if layer % 2 == 0:
            h, sg, sr, ss = even_mixer(
                x, p['ev_w_in'][i], p['ev_w_out'][i], p['a_gate_up'][i], p['a_gate_b'][i],
                p['a_norm_g'][i], p['b_mu'][i], p['b_w0'][i], p['b_w_up'][i], p['b_a0'][i],
                p['b_a_up'][i], p['b_g_up'][i], p['b_k_k'][i], p['b_k_a'][i], p['b_r_k'][i],
                p['b_ln_g'][i], p['b_ln_b'][i], s_gla[i], s_rwkv[i], s_shift[i], segments)
            gla_new.append(sg)
            rwkv_new.append(sr)
            shift_new.append(ss)
        else:
            h, sre, sim = s5_mixer(
                x, p['od_w_in'][i], p['c_a_re'][i], p['c_a_im'][i], p['c_log_dt'][i],
                p['c_b_re'][i], p['c_b_im'][i], p['c_c_re'][i], p['c_c_im'][i], p['c_d'][i],
                p['c_w_glu'][i], p['c_b_glu'][i], p['od_w_out'][i], s_re[i], s_im[i])
            re_new.append(sre)
            im_new.append(sim)
        x = layer_norm(ALPHA * x + h, p['ln_mix_g'][layer], p['ln_mix_b'][layer])
        f = moe(x, p['w_router'], p['moe_w_up'][layer], p['moe_w_down'][layer])
        x = layer_norm(ALPHA * x + f, p['ln_ffn_g'][layer], p['ln_ffn_b'][layer])
    return (x, jnp.stack(gla_new), jnp.stack(rwkv_new), jnp.stack(shift_new),
            jnp.stack(re_new), jnp.stack(im_new))


def setup_inputs(seed: int = 0) -> dict:
    key = jax.random.key(seed)
    ks = iter(jax.random.split(key, 64))

    def nrm(shape, std):
        return std * jax.random.normal(next(ks), shape, jnp.float32)

    def unif(shape, lo, hi):
        return jax.random.uniform(next(ks), shape, jnp.float32, lo, hi)

    ev_cols = A_COLS + B_COLS
    n_idx = jnp.arange(C_STATE, dtype=jnp.float32)
    w0_base = -6.5 + 5.0 * jnp.linspace(0.0, 1.0, B_WIDTH, dtype=jnp.float32) ** 1.5
    return {
        'x_prompt': nrm((BATCH, SEQ, D_MODEL), 1.0),
        'x_sample': nrm((DEC_BATCH, DEC_SEQ, D_MODEL), 1.0),
        'state_gla': nrm((N_EVEN, DEC_BATCH, A_HEADS, A_DK, A_DV), 0.3),
        'state_rwkv': nrm((N_EVEN, DEC_BATCH, B_HEADS, B_HEAD, B_HEAD), 0.3),
        'state_shift': nrm((N_EVEN, DEC_BATCH, B_COLS), 1.0),
        'state_s5_re': nrm((N_ODD, DEC_BATCH, C_GROUPS, C_STATE), 0.3),
        'state_s5_im': nrm((N_ODD, DEC_BATCH, C_GROUPS, C_STATE), 0.3),
        'meta': nrm((N_META, D_MODEL), 1.0),
        'ev_w_in': nrm((N_EVEN, D_MODEL, ev_cols), D_MODEL ** -0.5),
        'ev_w_out': nrm((N_EVEN, A_WIDTH + B_WIDTH, D_MODEL), BETA * (A_WIDTH + B_WIDTH) ** -0.5),
        'a_gate_up': nrm((N_EVEN, A_GATE_RANK, A_HEADS * A_DK), A_GATE_RANK ** -0.5),
        'a_gate_b': nrm((N_EVEN, A_HEADS * A_DK), 0.1),
        'a_norm_g': 1.0 + nrm((N_EVEN, A_WIDTH), 0.02),
        'b_mu': unif((N_EVEN, B_COLS), 0.0, 1.0),
        'b_w0': w0_base + nrm((N_EVEN, B_WIDTH), 0.1),
        'b_w_up': nrm((N_EVEN, B_DECAY_RANK, B_WIDTH), 0.1 * B_DECAY_RANK ** -0.5),
        'b_a0': nrm((N_EVEN, B_WIDTH), 0.1),
        'b_a_up': nrm((N_EVEN, B_AAA_RANK, B_WIDTH), 0.1 * B_AAA_RANK ** -0.5),
        'b_g_up': nrm((N_EVEN, B_GATE_RANK, B_WIDTH), B_GATE_RANK ** -0.5),
        'b_k_k': 0.85 + nrm((N_EVEN, B_WIDTH), 0.02),
        'b_k_a': 1.0 + nrm((N_EVEN, B_WIDTH), 0.02),
        'b_r_k': nrm((N_EVEN, B_WIDTH), 0.1),
        'b_ln_g': 1.0 + nrm((N_EVEN, B_WIDTH), 0.02),
        'b_ln_b': nrm((N_EVEN, B_WIDTH), 0.01),
        'od_w_in': nrm((N_ODD, D_MODEL, C_WIDTH), D_MODEL ** -0.5),
        'c_a_re': -0.5 + nrm((N_ODD, C_GROUPS, C_STATE), 0.01),
        'c_a_im': math.pi * n_idx + nrm((N_ODD, C_GROUPS, C_STATE), 0.01),
        'c_log_dt': unif((N_ODD, C_GROUPS), math.log(1e-3), math.log(1e-1)),
        'c_b_re': nrm((N_ODD, C_GROUPS, C_STATE, C_GROUP), (2.0 * C_GROUP) ** -0.5),
        'c_b_im': nrm((N_ODD, C_GROUPS, C_STATE, C_GROUP), (2.0 * C_GROUP) ** -0.5),
        'c_c_re': nrm((N_ODD, C_GROUPS, C_GROUP, C_STATE), 0.5),
        'c_c_im': nrm((N_ODD, C_GROUPS, C_GROUP, C_STATE), 0.5),
        'c_d': nrm((N_ODD, C_WIDTH), 0.5),
        'c_w_glu': nrm((N_ODD, C_WIDTH, C_WIDTH), C_WIDTH ** -0.5),
        'c_b_glu': nrm((N_ODD, C_WIDTH), 0.01),
        'od_w_out': nrm((N_ODD, C_WIDTH, D_MODEL), BETA * C_WIDTH ** -0.5),
        'w_router': nrm((D_MODEL, N_EXPERTS), D_MODEL ** -0.5),
        'moe_w_up': nrm((DEPTH, N_EXPERTS, D_MODEL, 2 * D_EXPERT), D_MODEL ** -0.5),
        'moe_w_down': nrm((DEPTH, N_EXPERTS, D_EXPERT, D_MODEL), BETA * D_EXPERT ** -0.5),
        'ln_mix_g': 1.0 + nrm((DEPTH, D_MODEL), 0.02),
        'ln_mix_b': nrm((DEPTH, D_MODEL), 0.01),
        'ln_ffn_g': 1.0 + nrm((DEPTH, D_MODEL), 0.02),
        'ln_ffn_b': nrm((DEPTH, D_MODEL), 0.01),
    }


def reference(x_prompt, x_sample, state_gla, state_rwkv, state_shift, state_s5_re, state_s5_im,
              meta, ev_w_in, ev_w_out, a_gate_up, a_gate_b, a_norm_g, b_mu, b_w0, b_w_up, b_a0,
              b_a_up, b_g_up, b_k_k, b_k_a, b_r_k, b_ln_g, b_ln_b, od_w_in, c_a_re, c_a_im,
              c_log_dt, c_b_re, c_b_im, c_c_re, c_c_im, c_d, c_w_glu, c_b_glu, od_w_out,
              w_router, moe_w_up, moe_w_down, ln_mix_g, ln_mix_b, ln_ffn_g, ln_ffn_b):
    p = dict(ev_w_in=ev_w_in, ev_w_out=ev_w_out, a_gate_up=a_gate_up, a_gate_b=a_gate_b,
             a_norm_g=a_norm_g, b_mu=b_mu, b_w0=b_w0, b_w_up=b_w_up, b_a0=b_a0, b_a_up=b_a_up,
             b_g_up=b_g_up, b_k_k=b_k_k, b_k_a=b_k_a, b_r_k=b_r_k, b_ln_g=b_ln_g, b_ln_b=b_ln_b,
             od_w_in=od_w_in, c_a_re=c_a_re, c_a_im=c_a_im, c_log_dt=c_log_dt, c_b_re=c_b_re,
             c_b_im=c_b_im, c_c_re=c_c_re, c_c_im=c_c_im, c_d=c_d, c_w_glu=c_w_glu,
             c_b_glu=c_b_glu, od_w_out=od_w_out, w_router=w_router, moe_w_up=moe_w_up,
             moe_w_down=moe_w_down, ln_mix_g=ln_mix_g, ln_mix_b=ln_mix_b, ln_ffn_g=ln_ffn_g,
             ln_ffn_b=ln_ffn_b)
    f32 = jnp.float32
    bp, sp = x_prompt.shape[0], x_prompt.shape[1]
    xp = jnp.concatenate(
        [jnp.broadcast_to(meta.astype(x_prompt.dtype)[None], (bp, N_META, D_MODEL)), x_prompt], axis=1)
    yp, gla_p, rwkv_p, shift_p, re_p, im_p = trunk(
        xp,
        jnp.zeros((N_EVEN, bp, A_HEADS, A_DK, A_DV), f32),
        jnp.zeros((N_EVEN, bp, B_HEADS, B_HEAD, B_HEAD), f32),
        jnp.zeros((N_EVEN, bp, B_COLS), f32),
        jnp.zeros((N_ODD, bp, C_GROUPS, C_STATE), f32),
        jnp.zeros((N_ODD, bp, C_GROUPS, C_STATE), f32),
        (N_META, sp), p)
    ys, gla_s, rwkv_s, shift_s, re_s, im_s = trunk(
        x_sample, state_gla, state_rwkv, state_shift, state_s5_re, state_s5_im,
        (x_sample.shape[1],), p)
    return (yp[:, N_META:], ys, gla_p, gla_s, rwkv_p, rwkv_s, shift_p, shift_s,
            re_p, re_s, im_p, im_s)
```

```python
import functools
import math

import jax
import jax.numpy as jnp
from jax import lax
from jax.experimental import pallas as pl
from jax.experimental.pallas import tpu as pltpu

F32 = jnp.float32
BF16 = jnp.bfloat16

D_MODEL = 2048
DEPTH = 2
N_META = 16

A_WIDTH = 1024
A_HEADS = 4
A_DV = 256
A_DK = 128
A_GATE_RANK = 16
A_GATE_TAU = 16.0
A_QK = A_HEADS * A_DK
A_COLS = 2 * A_QK + 2 * A_WIDTH + A_GATE_RANK

B_WIDTH = 1024
B_HEAD = 64
B_HEADS = 16
B_DECAY_RANK = 64
B_AAA_RANK = 64
B_GATE_RANK = 128
B_COLS = 3 * B_WIDTH + B_DECAY_RANK + B_AAA_RANK + B_GATE_RANK

C_GROUP = 16
C_GROUPS = 128
C_STATE = 64

N_EXPERTS = 16
N_EXPERT_GROUPS = 4
EXPERTS_PER_GROUP = 4
TOP_K = 2
D_EXPERT = 1024

ALPHA = (2.0 * DEPTH) ** 0.25
LN_EPS = 1e-5
HEAD_NORM_EPS = 1e-5
RWKV_GN_EPS = 64e-5

P_B_OFF = B_COLS
P_COLS = 2 * B_COLS

VMEM_LIMIT = 56 * 1024 * 1024
MXU_TILE = 256
SUBLANES = 8


def _cparams(sem):
    return pltpu.CompilerParams(dimension_semantics=sem, vmem_limit_bytes=VMEM_LIMIT)


def _cdiv(a, b):
    return (a + b - 1) // b


def _softplus(x):
    return jnp.maximum(x, 0.0) + jnp.log(1.0 + jnp.exp(-jnp.abs(x)))


def _sigmoid(x):
    return 1.0 / (1.0 + jnp.exp(-x))


def _mm_kernel(x_ref, w_ref, o_ref):
    o_ref[...] = jnp.dot(x_ref[...].astype(BF16), w_ref[...].astype(BF16),
                         preferred_element_type=F32)


def matmul(x, w, tm, tn):
    m, k = x.shape
    n = w.shape[1]
    assert n % tn == 0
    return pl.pallas_call(
        _mm_kernel,
        out_shape=jax.ShapeDtypeStruct((m, n), F32),
        grid=(n // tn, _cdiv(m, tm)),
        in_specs=[pl.BlockSpec((tm, k), lambda j, i: (i, 0)),
                  pl.BlockSpec((k, tn), lambda j, i: (0, j))],
        out_specs=pl.BlockSpec((tm, tn), lambda j, i: (i, j)),
        compiler_params=_cparams(("arbitrary", "arbitrary")),
        name="matmul",
    )(x, w)


def _layer_norm_rows(y, g, b):
    mu = jnp.mean(y, axis=-1, keepdims=True)
    yc = y - mu
    var = jnp.mean(yc * yc, axis=-1, keepdims=True)
    return yc * lax.rsqrt(var + LN_EPS) * g + b


def _proj_ln_kernel(n_lhs, *refs):
    lhs = refs[:n_lhs]
    w_ref, x_ref, g_ref, b_ref, o_ref, ob_ref = refs[n_lhs:]
    acc = None
    off = 0
    for r in lhs:
        kk = r.shape[1]
        part = jnp.dot(r[...], w_ref[off:off + kk, :], preferred_element_type=F32)
        acc = part if acc is None else acc + part
        off += kk
    y = _layer_norm_rows(ALPHA * x_ref[...] + acc, g_ref[...], b_ref[...])
    o_ref[...] = y
    ob_ref[...] = y.astype(BF16)


def proj_ln(lhs_list, w_bf16, x, g, b, tm):
    m, d = x.shape
    kin = w_bf16.shape[0]
    in_specs = [pl.BlockSpec((tm, l.shape[1]), lambda i: (i, 0)) for l in lhs_list]
    in_specs += [pl.BlockSpec((kin, d), lambda i: (0, 0)),
                 pl.BlockSpec((tm, d), lambda i: (i, 0)),
                 pl.BlockSpec((1, d), lambda i: (0, 0)),
                 pl.BlockSpec((1, d), lambda i: (0, 0))]
    return pl.pallas_call(
        functools.partial(_proj_ln_kernel, len(lhs_list)),
        out_shape=(jax.ShapeDtypeStruct((m, d), F32), jax.ShapeDtypeStruct((m, d), BF16)),
        grid=(_cdiv(m, tm),),
        in_specs=in_specs,
        out_specs=(pl.BlockSpec((tm, d), lambda i: (i, 0)),
                   pl.BlockSpec((tm, d), lambda i: (i, 0))),
        compiler_params=_cparams(("arbitrary",)),
        name="proj_ln",
    )(*lhs_list, w_bf16, x, g.reshape(1, d), b.reshape(1, d))


def _add_ln_kernel(x_ref, f0_ref, f1_ref, g_ref, b_ref, o_ref, ob_ref):
    y = _layer_norm_rows(ALPHA * x_ref[...] + (f0_ref[...] + f1_ref[...]), g_ref[...], b_ref[...])
    o_ref[...] = y
    ob_ref[...] = y.astype(BF16)


def add_ln(x, f0, f1, g, b, tm):
    m, d = x.shape
    row = pl.BlockSpec((tm, d), lambda i: (i, 0))
    vec = pl.BlockSpec((1, d), lambda i: (0, 0))
    return pl.pallas_call(
        _add_ln_kernel,
        out_shape=(jax.ShapeDtypeStruct((m, d), F32), jax.ShapeDtypeStruct((m, d), BF16)),
        grid=(_cdiv(m, tm),),
        in_specs=[row, row, row, vec, vec],
        out_specs=(row, row),
        compiler_params=_cparams(("arbitrary",)),
        name="add_ln",
    )(x, f0, f1, g.reshape(1, d), b.reshape(1, d))


def _gla_kernel(nb, lb, chunk, *refs):
    p_refs = refs[:nb]
    (s0_ref, gup_ref, gb_ref, ng_ref) = refs[nb:nb + 4]
    o_ref = refs[nb + 4]
    sout_ref = refs[nb + 5]
    s_ref = refs[nb + 6]
    blk = pl.program_id(1)

    @pl.when(blk == 0)
    def _():
        s_ref[...] = s0_ref[...]

    rows = lax.broadcasted_iota(jnp.int32, (chunk, chunk), 0)
    cols = lax.broadcasted_iota(jnp.int32, (chunk, chunk), 1)
    tril = rows >= cols
    tril_f = tril.astype(F32)
    for j in range(nb):
        p_ref = p_refs[j]
        gd = p_ref[:, 2 * A_QK + 2 * A_WIDTH:2 * A_QK + 2 * A_WIDTH + A_GATE_RANK]
        z = jnp.dot(gd, gup_ref[...], preferred_element_type=F32) + gb_ref[...]
        lg_all = -_softplus(-z) * (1.0 / A_GATE_TAU)
        for c in range(lb // chunk):
            r0 = c * chunk
            lg = lg_all[r0:r0 + chunk, :]
            bc = jnp.dot(tril_f, lg, preferred_element_type=F32, precision=lax.Precision.HIGHEST)
            for h in range(A_HEADS):
                q = p_ref[r0:r0 + chunk, h * A_DK:(h + 1) * A_DK] * (A_DK ** -0.5)
                k = p_ref[r0:r0 + chunk, A_QK + h * A_DK:A_QK + (h + 1) * A_DK]
                v = p_ref[r0:r0 + chunk, 2 * A_QK + h * A_DV:2 * A_QK + (h + 1) * A_DV]
                rg = p_ref[r0:r0 + chunk,
                           2 * A_QK + A_WIDTH + h * A_DV:2 * A_QK + A_WIDTH + (h + 1) * A_DV]
                b = bc[:, h * A_DK:(h + 1) * A_DK]
                bl = b[chunk - 1:chunk, :]
                qd = (q * jnp.exp(b)).astype(BF16)
                kd = (k * jnp.exp(-b)).astype(BF16)
                vb = v.astype(BF16)
                att = lax.dot_general(qd, kd, (((1,), (1,)), ((), ())), preferred_element_type=F32)
                att = jnp.where(tril, att, 0.0).astype(BF16)
                s = s_ref[j, h]
                o = (jnp.dot(att, vb, preferred_element_type=F32)
                     + jnp.dot(qd, s.astype(BF16), preferred_element_type=F32))
                kl = (k * jnp.exp(bl - b)).astype(BF16)
                kv = lax.dot_general(kl, vb, (((0,), (0,)), ((), ())), preferred_element_type=F32)
                dec = jnp.broadcast_to(jnp.exp(bl), (A_DK, A_DK)).T
                s_ref[j, h] = s * jnp.concatenate([dec] * (A_DV // A_DK), axis=1) + kv
                o = o * lax.rsqrt(jnp.mean(o * o, axis=-1, keepdims=True) + HEAD_NORM_EPS)
                o = o * ng_ref[:, h * A_DV:(h + 1) * A_DV] * (rg * _sigmoid(rg))
                o_ref[j, r0:r0 + chunk, h * A_DV:(h + 1) * A_DV] = o.astype(BF16)

    @pl.when(blk == pl.num_programs(1) - 1)
    def _():
        sout_ref[...] = s_ref[...]


def gla_group(p, s0, gate_up, gate_b, norm_g, *, row_off, nbatch, seq, nb, lb, chunk):
    t = p.shape[0]
    nblk = seq // lb
    assert seq % lb == 0 and lb % chunk == 0 and nbatch % nb == 0 and row_off % lb == 0
    base = row_off // lb

    def p_map(j):
        return lambda bi, blk: (base + (bi * nb + j) * nblk + blk, 0)

    in_specs = [pl.BlockSpec((lb, P_B_OFF), p_map(j)) for j in range(nb)]
    in_specs += [pl.BlockSpec((nb, A_HEADS, A_DK, A_DV), lambda bi, blk: (bi, 0, 0, 0)),
                 pl.BlockSpec((A_GATE_RANK, A_QK), lambda bi, blk: (0, 0)),
                 pl.BlockSpec((1, A_QK), lambda bi, blk: (0, 0)),
                 pl.BlockSpec((1, A_WIDTH), lambda bi, blk: (0, 0))]

    out_specs = [pl.BlockSpec((nb, lb, A_WIDTH), lambda bi, blk: (bi, blk, 0)),
                 pl.BlockSpec((nb, A_HEADS, A_DK, A_DV), lambda bi, blk: (bi, 0, 0, 0))]
    out_shape = [jax.ShapeDtypeStruct((nbatch, seq, A_WIDTH), BF16),
                 jax.ShapeDtypeStruct((nbatch, A_HEADS, A_DK, A_DV), F32)]
    o, s_new = pl.pallas_call(
        functools.partial(_gla_kernel, nb, lb, chunk),
        out_shape=out_shape,
        grid=(nbatch // nb, nblk),
        in_specs=in_specs,
        out_specs=out_specs,
        scratch_shapes=[pltpu.VMEM((nb, A_HEADS, A_DK, A_DV), F32)],
        compiler_params=_cparams(("arbitrary", "arbitrary")),
        name="gla",
    )(*([p] * nb), s0, gate_up, gate_b.reshape(1, A_QK), norm_g.reshape(1, A_WIDTH))
    return o.reshape(nbatch * seq, A_WIDTH), s_new


def _head_ones(dtype):
    r = lax.broadcasted_iota(jnp.int32, (MXU_TILE, MXU_TILE), 0) // B_HEAD
    c = lax.broadcasted_iota(jnp.int32, (MXU_TILE, MXU_TILE), 1) // B_HEAD
    return (r == c).astype(dtype)


def _head_sum(x, ones, precision=None):
    parts = [jnp.dot(x[:, c * MXU_TILE:(c + 1) * MXU_TILE], ones,
                     preferred_element_type=F32, precision=precision)
             for c in range(B_WIDTH // MXU_TILE)]
    return jnp.concatenate(parts, axis=-1)


def _rwkv_kernel(nb, lb, *refs):
    p_refs = refs[:nb]
    (shift0_ref, s0_ref, mu_ref, w0_ref, wup_ref, a0_ref, aup_ref, gup_ref,
     kk_ref, ka_ref, rk_ref, lng_ref, lnb_ref) = refs[nb:nb + 13]
    o_ref, sout_ref = refs[nb + 13:nb + 15]
    (s_ref, prev_ref, dm_ref, r_s, w_s, k_s, kk_s, kka_s, vh_s, vl_s, o_s, g_s, bon_s
     ) = refs[nb + 15:]
    blk = pl.program_id(1)
    W = B_WIDTH

    @pl.when(blk == 0)
    def _():
        s_ref[...] = s0_ref[...]
        prev_ref[...] = shift0_ref[...]

    ones_b = _head_ones(BF16)
    ones_f = _head_ones(F32)
    hi = lax.Precision.HIGHEST
    dm_ref[...] = (lax.broadcasted_iota(jnp.int32, (B_HEAD, W), 1) % B_HEAD
                   == lax.broadcasted_iota(jnp.int32, (B_HEAD, W), 0)).astype(F32)

    for j in range(nb):
        pb = p_refs[j][...]
        first = lax.broadcasted_iota(jnp.int32, (lb, 1), 0) == 0
        prev = jnp.where(first, prev_ref[j], pltpu.roll(pb, 1, axis=0))
        prev_ref[j] = pb[lb - 1:lb, :]
        xm = pb + (prev - pb) * mu_ref[...]
        r = xm[:, :W]
        k = xm[:, W:2 * W]
        v = xm[:, 2 * W:3 * W]
        wd = xm[:, 3 * W:3 * W + B_DECAY_RANK]
        ad = xm[:, 3 * W + B_DECAY_RANK:3 * W + B_DECAY_RANK + B_AAA_RANK]
        gd = xm[:, 3 * W + B_DECAY_RANK + B_AAA_RANK:]
        w = -_softplus(-(w0_ref[...] + jnp.dot(jnp.tanh(wd).astype(BF16), wup_ref[...].astype(BF16),
                                               preferred_element_type=F32))) - 0.5
        a = _sigmoid(a0_ref[...] + jnp.dot(ad.astype(BF16), aup_ref[...].astype(BF16),
                                           preferred_element_type=F32))
        g = jnp.dot(_sigmoid(gd).astype(BF16), gup_ref[...].astype(BF16), preferred_element_type=F32)
        kk = k * kk_ref[...]
        nrm = jnp.sqrt(_head_sum(kk * kk, ones_f, hi))
        kk = kk / jnp.maximum(nrm, 1e-12)
        k2 = k * (1.0 + (a - 1.0) * ka_ref[...])
        vh = v.astype(BF16).astype(F32)
        r_s[j] = r
        w_s[j] = jnp.exp(-jnp.exp(w))
        k_s[j] = k2
        kk_s[j] = kk
        kka_s[j] = kk * a
        vh_s[j] = vh
        vl_s[j] = v - vh
        g_s[j] = g
        bon_s[j] = _head_sum(r * k2 * rk_ref[...], ones_f, hi) * v

    def step(t, carry):
        dm = dm_ref[...]
        for j in range(nb):
            s = s_ref[j]
            row = lambda ref: ref[j, pl.ds(t, 1), :]
            sa = _head_sum((s * row(kk_s)).astype(BF16), ones_b)
            vcol = (_head_sum((dm * row(vh_s)).astype(BF16), ones_b)
                    + _head_sum((dm * row(vl_s)).astype(BF16), ones_b))
            s = s * row(w_s) - sa * row(kka_s) + vcol * row(k_s)
            s_ref[j] = s
            ob = _head_sum((s * row(r_s)).astype(BF16), ones_b)
            o_s[j, pl.ds(t, 1), :] = jnp.sum(ob * dm, axis=0, keepdims=True)
        return carry

    lax.fori_loop(0, lb, step, 0)

    for j in range(nb):
        o = o_s[j]
        oc = o - _head_sum(o, ones_f, hi) * (1.0 / B_HEAD)
        var = _head_sum(oc * oc, ones_f, hi) * (1.0 / B_HEAD)
        o = oc * lax.rsqrt(var + RWKV_GN_EPS) * lng_ref[...] + lnb_ref[...]
        o_ref[j] = ((o + bon_s[j]) * g_s[j]).astype(BF16)

    @pl.when(blk == pl.num_programs(1) - 1)
    def _():
        sout_ref[...] = s_ref[...]


def rwkv_group(p, s0, shift0, mu, w0, w_up, a0, a_up, g_up, k_k, k_a, r_k, ln_g, ln_b,
               *, row_off, nbatch, seq, nb, lb):
    nblk = seq // lb
    assert seq % lb == 0 and nbatch % nb == 0 and row_off % lb == 0
    base = row_off // lb
    W = B_WIDTH
    s0 = s0.transpose(0, 2, 1, 3).reshape(nbatch, B_HEAD, W)

    def p_map(j):
        return lambda bi, blk: (base + (bi * nb + j) * nblk + blk, 1)

    const = lambda shape: pl.BlockSpec(shape, lambda bi, blk: (0,) * len(shape))
    in_specs = [pl.BlockSpec((lb, B_COLS), p_map(j)) for j in range(nb)]
    in_specs += [pl.BlockSpec((nb, 1, B_COLS), lambda bi, blk: (bi, 0, 0)),
                 pl.BlockSpec((nb, B_HEAD, W), lambda bi, blk: (bi, 0, 0)),
                 const((1, B_COLS)), const((1, W)), const((B_DECAY_RANK, W)), const((1, W)),
                 const((B_AAA_RANK, W)), const((B_GATE_RANK, W)),
                 const((1, W)), const((1, W)), const((1, W)), const((1, W)), const((1, W))]
    out_specs = [pl.BlockSpec((nb, lb, W), lambda bi, blk: (bi, blk, 0)),
                 pl.BlockSpec((nb, B_HEAD, W), lambda bi, blk: (bi, 0, 0))]
    out_shape = [jax.ShapeDtypeStruct((nbatch, seq, W), BF16),
                 jax.ShapeDtypeStruct((nbatch, B_HEAD, W), F32)]
    tok = pltpu.VMEM((nb, lb, W), F32)
    o, s_new = pl.pallas_call(
        functools.partial(_rwkv_kernel, nb, lb),
        out_shape=out_shape,
        grid=(nbatch // nb, nblk),
        in_specs=in_specs,
        out_specs=out_specs,
        scratch_shapes=[pltpu.VMEM((nb, B_HEAD, W), F32), pltpu.VMEM((nb, 1, B_COLS), F32),
                        pltpu.VMEM((B_HEAD, W), F32)] + [tok] * 10,
        compiler_params=_cparams(("arbitrary", "arbitrary")),
        name="rwkv",
    )(*([p] * nb), shift0.reshape(nbatch, 1, B_COLS), s0, mu.reshape(1, B_COLS),
      w0.reshape(1, W), w_up, a0.reshape(1, W), a_up, g_up, k_k.reshape(1, W),
      k_a.reshape(1, W), r_k.reshape(1, W), ln_g.reshape(1, W), ln_b.reshape(1, W))
    s_new = s_new.reshape(nbatch, B_HEAD, B_HEADS, B_HEAD).transpose(0, 2, 1, 3)
    return o.reshape(nbatch * seq, W), s_new


GP = 2
SP = C_GROUPS * C_STATE


def s5_tables(a_re, a_im, log_dt, b_re, b_im, c_re, c_im, d, chunk):
    G, P, c = C_GROUPS, C_STATE, C_GROUP
    hi = lax.Precision.HIGHEST
    dt = jnp.exp(log_dt)[:, None]
    m = jnp.arange(chunk + 1, dtype=F32)[:, None, None]
    mag = jnp.exp(m * (dt * a_re))
    ang = m * (dt * a_im)
    pw_re, pw_im = mag * jnp.cos(ang), mag * jnp.sin(ang)
    num_re, num_im = pw_re[1] - 1.0, pw_im[1]
    den = a_re * a_re + a_im * a_im
    q_re = (num_re * a_re + num_im * a_im) / den
    q_im = (num_im * a_re - num_re * a_im) / den
    bb_re = q_re[..., None] * b_re - q_im[..., None] * b_im
    bb_im = q_re[..., None] * b_im + q_im[..., None] * b_re
    ca_re = c_re[None] * pw_re[:, :, None, :] - c_im[None] * pw_im[:, :, None, :]
    ca_im = c_re[None] * pw_im[:, :, None, :] + c_im[None] * pw_re[:, :, None, :]
    kern = (jnp.einsum('mgcp,gpd->mgdc', ca_re[:chunk], bb_re, precision=hi)
            - jnp.einsum('mgcp,gpd->mgdc', ca_im[:chunk], bb_im, precision=hi))
    tj = jnp.arange(chunk)
    lag = tj[None, :] - tj[:, None]
    toep = jnp.where((lag >= 0)[:, :, None, None, None], kern[jnp.clip(lag, 0, chunk - 1)], 0.0)
    toep = toep.transpose(2, 0, 3, 1, 4).reshape(G, chunk * c, chunk * c)
    dvec = jnp.tile(d.reshape(G, 1, c), (1, chunk, 1)).reshape(G, 1, chunk * c)
    rev_re, rev_im = pw_re[:chunk][::-1], pw_im[:chunk][::-1]
    bs_re = rev_re[..., None] * bb_re[None] - rev_im[..., None] * bb_im[None]
    bs_im = rev_re[..., None] * bb_im[None] + rev_im[..., None] * bb_re[None]
    to_in = lambda z: z.transpose(1, 0, 3, 2).reshape(G, chunk * c, P)
    to_out = lambda z: z.transpose(1, 3, 0, 2).reshape(G, P, chunk * c)
    cs_re, cs_im = to_out(ca_re[1:]), to_out(-ca_im[1:])
    even = (jnp.arange(G) % GP == 0)[:, None, None]

    def pad_lanes(z):
        zero = jnp.zeros_like(z)
        return jnp.where(even, jnp.concatenate([z, zero], -1), jnp.concatenate([zero, z], -1))

    def pad_rows(z):
        zero = jnp.zeros_like(z)
        return jnp.where(even, jnp.concatenate([z, zero], 1), jnp.concatenate([zero, z], 1))

    return dict(toep=toep.astype(BF16), dvec=dvec,
                bs_re=pad_lanes(to_in(bs_re)).astype(BF16), bs_im=pad_lanes(to_in(bs_im)).astype(BF16),
                cs_re=pad_rows(cs_re).astype(BF16), cs_im=pad_rows(cs_im).astype(BF16),
                ac_re=pw_re[chunk].reshape(1, SP), ac_im=pw_im[chunk].reshape(1, SP))


def _s5_in_kernel(uf_ref, bre_ref, bim_ref, vre_ref, vim_ref):
    re = None
    im = None
    for q in range(GP):
        u = uf_ref[q].astype(BF16)
        pr = jnp.dot(u, bre_ref[q], preferred_element_type=F32)
        pi = jnp.dot(u, bim_ref[q], preferred_element_type=F32)
        re = pr if re is None else re + pr
        im = pi if im is None else im + pi
    vre_ref[...] = re
    vim_ref[...] = im


def _s5_scan_kernel(nbatch, nchunks, cpi, vre_ref, vim_ref, s0re_ref, s0im_ref, acre_ref, acim_ref,
                    sre_ref, sim_ref, fre_ref, fim_ref):
    ar = acre_ref[...]
    ai = acim_ref[...]
    n_iter = _cdiv(nchunks, cpi)
    nrow = cpi * nbatch

    def iteration(i, carry):
        sr, si = carry
        start = i * nrow
        rows = pl.ds(start if isinstance(start, int) else pl.multiple_of(start, nrow), nrow)
        vre = vre_ref[rows, :]
        vim = vim_ref[rows, :]
        starts_r, starts_i, after = [], [], []
        for q in range(cpi):
            starts_r.append(sr)
            starts_i.append(si)
            vr = vre[q * nbatch:(q + 1) * nbatch]
            vi = vim[q * nbatch:(q + 1) * nbatch]
            sr, si = ar * sr - ai * si + vr, ar * si + ai * sr + vi
            after.append((sr, si))
        sre_ref[rows, :] = jnp.concatenate(starts_r, axis=0) if cpi > 1 else starts_r[0]
        sim_ref[rows, :] = jnp.concatenate(starts_i, axis=0) if cpi > 1 else starts_i[0]
        return (sr, si), after

    carry = lax.fori_loop(0, n_iter - 1, lambda i, c: iteration(i, c)[0],
                          (s0re_ref[...], s0im_ref[...]))
    _, after = iteration(n_iter - 1, carry)
    fre_ref[...], fim_ref[...] = after[(nchunks - 1) % cpi]


def _s5_out_kernel(uf_ref, toep_ref, dvec_ref, cre_ref, cim_ref, sre_ref, sim_ref, y_ref):
    sre = sre_ref[...].astype(BF16)
    sim = sim_ref[...].astype(BF16)
    for q in range(GP):
        u = uf_ref[q]
        y = (jnp.dot(u.astype(BF16), toep_ref[q], preferred_element_type=F32)
             + jnp.dot(sre, cre_ref[q], preferred_element_type=F32)
             + jnp.dot(sim, cim_ref[q], preferred_element_type=F32))
        y_ref[q] = y + dvec_ref[q] * u


def s5_group(u, s0_re, s0_im, tab, *, nbatch, seq, chunk):
    G, P, c = C_GROUPS, C_STATE, C_GROUP
    nchunks = seq // chunk
    assert seq % chunk == 0
    cpi = max(1, SUBLANES // nbatch)
    nch_pad = _cdiv(nchunks, cpi) * cpi
    R = nch_pad * nbatch
    cr = chunk * c
    uf = u.reshape(nbatch, nchunks, chunk, G, c).transpose(3, 1, 0, 2, 4)
    uf = jnp.pad(uf, ((0, 0), (0, nch_pad - nchunks), (0, 0), (0, 0), (0, 0))).reshape(G, R, cr)
    grp = lambda shape: pl.BlockSpec((GP,) + shape, lambda g: (g, 0, 0))
    lane_blk = pl.BlockSpec((R, GP * P), lambda g: (0, g))
    v_re, v_im = pl.pallas_call(
        _s5_in_kernel,
        out_shape=[jax.ShapeDtypeStruct((R, SP), F32)] * 2,
        grid=(G // GP,),
        in_specs=[grp((R, cr)), grp((cr, GP * P)), grp((cr, GP * P))],
        out_specs=[lane_blk, lane_blk],
        compiler_params=_cparams(("arbitrary",)),
        name="s5_in",
    )(uf, tab['bs_re'], tab['bs_im'])

    tl = 1024
    col = lambda rows: pl.BlockSpec((rows, tl), lambda i: (0, i))
    s_re, s_im, f_re, f_im = pl.pallas_call(
        functools.partial(_s5_scan_kernel, nbatch, nchunks, cpi),
        out_shape=[jax.ShapeDtypeStruct((R, SP), F32)] * 2 + [jax.ShapeDtypeStruct((nbatch, SP), F32)] * 2,
        grid=(SP // tl,),
        in_specs=[col(R), col(R), col(nbatch), col(nbatch), col(1), col(1)],
        out_specs=[col(R), col(R), col(nbatch), col(nbatch)],
        compiler_params=_cparams(("arbitrary",)),
        name="s5_scan",
    )(v_re, v_im, s0_re.reshape(nbatch, SP), s0_im.reshape(nbatch, SP), tab['ac_re'], tab['ac_im'])

    y = pl.pallas_call(
        _s5_out_kernel,
        out_shape=jax.ShapeDtypeStruct((G, R, cr), F32),
        grid=(G // GP,),
        in_specs=[grp((R, cr)), grp((cr, cr)), grp((1, cr)), grp((GP * P, cr)), grp((GP * P, cr)),
                  lane_blk, lane_blk],
        out_specs=grp((R, cr)),
        compiler_params=_cparams(("arbitrary",)),
        name="s5_out",
    )(uf, tab['toep'], tab['dvec'], tab['cs_re'], tab['cs_im'], s_re, s_im)
    y = y.reshape(G, nch_pad, nbatch, chunk, c)[:, :nchunks]
    y = y.transpose(2, 1, 3, 0, 4).reshape(nbatch * seq, G * c)
    return y, f_re.reshape(nbatch, G, P), f_im.reshape(nbatch, G, P)


def _glu_kernel(y_ref, yc_ref, w_ref, b_ref, o_ref):
    z = jax.nn.gelu(y_ref[...])
    zc = jax.nn.gelu(yc_ref[...])
    acc = jnp.dot(z.astype(BF16), w_ref[...], preferred_element_type=F32) + b_ref[...]
    o_ref[...] = (zc * _sigmoid(acc)).astype(BF16)


def gelu_glu(y, w_bf16, b, tm, tn):
    m, k = y.shape
    n = w_bf16.shape[1]
    return pl.pallas_call(
        _glu_kernel,
        out_shape=jax.ShapeDtypeStruct((m, n), BF16),
        grid=(n // tn, _cdiv(m, tm)),
        in_specs=[pl.BlockSpec((tm, k), lambda j, i: (i, 0)),
                  pl.BlockSpec((tm, tn), lambda j, i: (i, j)),
                  pl.BlockSpec((k, tn), lambda j, i: (0, j)),
                  pl.BlockSpec((1, tn), lambda j, i: (0, j))],
        out_specs=pl.BlockSpec((tm, tn), lambda j, i: (i, j)),
        compiler_params=_cparams(("arbitrary", "arbitrary")),
        name="gelu_glu",
    )(y, y, w_bf16, b.reshape(1, n))


MOE_TM = 256
MOE_TF = 512
MOE_TN = 1024


def _router_kernel(x_ref, wr_ref, e_ref, g_ref):
    logits = lax.dot_general(wr_ref[...], x_ref[...], (((1,), (1,)), ((), ())),
                             preferred_element_type=F32, precision=lax.Precision.HIGHEST)
    mx = jnp.max(logits, axis=0, keepdims=True)
    ex = jnp.exp(logits - mx)
    probs = ex / jnp.sum(ex, axis=0, keepdims=True)
    neg = jnp.float32(-jnp.inf)
    best = None
    for gi in range(N_EXPERT_GROUPS):
        v = [probs[gi * EXPERTS_PER_GROUP + r:gi * EXPERTS_PER_GROUP + r + 1, :]
             for r in range(EXPERTS_PER_GROUP)]
        m1 = jnp.maximum(jnp.maximum(v[0], v[1]), jnp.maximum(v[2], v[3]))
        i1 = jnp.where(v[0] == m1, 0, jnp.where(v[1] == m1, 1, jnp.where(v[2] == m1, 2, 3)))
        w = [jnp.where(i1 == r, neg, v[r]) for r in range(EXPERTS_PER_GROUP)]
        m2 = jnp.maximum(jnp.maximum(w[0], w[1]), jnp.maximum(w[2], w[3]))
        i2 = jnp.where(w[0] == m2, 0, jnp.where(w[1] == m2, 1, jnp.where(w[2] == m2, 2, 3)))
        score = m1 + m2
        cand = (score, m1, m2, i1 + gi * EXPERTS_PER_GROUP, i2 + gi * EXPERTS_PER_GROUP)
        if best is None:
            best = cand
        else:
            take = cand[0] > best[0]
            best = tuple(jnp.where(take, cn, bs) for cn, bs in zip(cand, best))
    _, m1, m2, e1, e2 = best
    tot = m1 + m2
    e_ref[...] = jnp.concatenate([e1, e2], axis=0)
    g_ref[...] = jnp.concatenate([m1 / tot, m2 / tot], axis=0)


def router(x, w_router, tm):
    t, d = x.shape
    return pl.pallas_call(
        _router_kernel,
        out_shape=[jax.ShapeDtypeStruct((TOP_K, t), jnp.int32), jax.ShapeDtypeStruct((TOP_K, t), F32)],
        grid=(_cdiv(t, tm),),
        in_specs=[pl.BlockSpec((tm, d), lambda i: (i, 0)),
                  pl.BlockSpec((N_EXPERTS, d), lambda i: (0, 0))],
        out_specs=[pl.BlockSpec((TOP_K, tm), lambda i: (0, i)),
                   pl.BlockSpec((TOP_K, tm), lambda i: (0, i))],
        compiler_params=_cparams(("arbitrary",)),
        name="router",
    )(x, w_router.T)


def _moe_up_kernel(st_ref, sf_ref, se_ref, nu_ref, xs_ref, w1_ref, w2_ref, gate_ref, h_ref):
    @pl.when(pl.program_id(0) < nu_ref[0])
    def _():
        x = xs_ref[...]
        a = jnp.dot(x, w1_ref[0].astype(BF16), preferred_element_type=F32)
        b = jnp.dot(x, w2_ref[0].astype(BF16), preferred_element_type=F32)
        h_ref[...] = (a * _sigmoid(a) * b * gate_ref[...]).astype(BF16)


def _moe_down_kernel(st_ref, sf_ref, se_ref, nu_ref, h_ref, w_ref, y_ref):
    @pl.when(pl.program_id(0) < nu_ref[0])
    def _():
        y_ref[...] = jnp.dot(h_ref[...], w_ref[0].astype(BF16), preferred_element_type=F32)


def _moe_schedule(tiles_e, tile_start, nblocks, n_steps):
    steps_e = tiles_e * nblocks
    cum = jnp.cumsum(steps_e)
    used = cum[-1]
    s = jnp.minimum(jnp.arange(n_steps, dtype=jnp.int32), used - 1)
    e = jnp.minimum(jnp.searchsorted(cum, s, side='right').astype(jnp.int32), N_EXPERTS - 1)
    local = s - (cum[e] - steps_e[e])
    te = jnp.maximum(tiles_e[e], 1)
    return (tile_start[e] + local % te).astype(jnp.int32), (local // te).astype(jnp.int32), e, \
        used.reshape(1).astype(jnp.int32)


def moe_layer(x, xb, w_router, w_up, w_down):
    t, d = x.shape
    tm = MOE_TM
    eid, gates = router(x, w_router, 1024)
    eid = eid.reshape(-1)
    na = TOP_K * t
    n_tiles = _cdiv(na + N_EXPERTS * (tm - 1), tm)
    npad = n_tiles * tm
    onehot = (eid[None, :] == jnp.arange(N_EXPERTS, dtype=jnp.int32)[:, None]).astype(jnp.int32)
    csum = jnp.cumsum(onehot, axis=1)
    counts = csum[:, -1]
    rank = jnp.sum(csum * onehot, axis=0) - 1
    tiles_e = (counts + tm - 1) // tm
    tile_start = jnp.cumsum(tiles_e) - tiles_e
    pos = tile_start[eid] * tm + rank
    tok = jnp.tile(jnp.arange(t, dtype=jnp.int32), TOP_K)
    row_src = jnp.zeros((npad,), jnp.int32).at[pos].set(tok)
    row_gate = jnp.zeros((npad,), F32).at[pos].set(gates.reshape(-1))
    xs = jnp.take(xb, row_src, axis=0)

    nf = D_EXPERT // MOE_TF
    st, sf, se, nu = _moe_schedule(tiles_e, tile_start, nf, n_tiles * nf)
    h = pl.pallas_call(
        _moe_up_kernel,
        out_shape=jax.ShapeDtypeStruct((npad, D_EXPERT), BF16),
        grid_spec=pltpu.PrefetchScalarGridSpec(
            num_scalar_prefetch=4,
            grid=(n_tiles * nf,),
            in_specs=[pl.BlockSpec((tm, d), lambda s, st, sf, se, nu: (st[s], 0)),
                      pl.BlockSpec((1, d, MOE_TF), lambda s, st, sf, se, nu: (se[s], 0, sf[s])),
                      pl.BlockSpec((1, d, MOE_TF), lambda s, st, sf, se, nu: (se[s], 0, nf + sf[s])),
                      pl.BlockSpec((tm, 1), lambda s, st, sf, se, nu: (st[s], 0))],
            out_specs=pl.BlockSpec((tm, MOE_TF), lambda s, st, sf, se, nu: (st[s], sf[s]))),
        compiler_params=_cparams(("arbitrary",)),
        name="moe_up",
    )(st, sf, se, nu, xs, w_up, w_up, row_gate.reshape(npad, 1))

    nn = d // MOE_TN
    st, sf, se, nu = _moe_schedule(tiles_e, tile_start, nn, n_tiles * nn)
    ys = pl.pallas_call(
        _moe_down_kernel,
        out_shape=jax.ShapeDtypeStruct((npad, d), F32),
        grid_spec=pltpu.PrefetchScalarGridSpec(
            num_scalar_prefetch=4,
            grid=(n_tiles * nn,),
            in_specs=[pl.BlockSpec((tm, D_EXPERT), lambda s, st, sf, se, nu: (st[s], 0)),
                      pl.BlockSpec((1, D_EXPERT, MOE_TN), lambda s, st, sf, se, nu: (se[s], 0, sf[s]))],
            out_specs=pl.BlockSpec((tm, MOE_TN), lambda s, st, sf, se, nu: (st[s], sf[s]))),
        compiler_params=_cparams(("arbitrary",)),
        name="moe_down",
    )(st, sf, se, nu, h, w_down)
    return jnp.take(ys, pos[:t], axis=0), jnp.take(ys, pos[t:], axis=0)


TOK_TM = 512


def kernel(x_prompt, x_sample, state_gla, state_rwkv, state_shift, state_s5_re, state_s5_im, meta, ev_w_in, ev_w_out, a_gate_up, a_gate_b, a_norm_g, b_mu, b_w0, b_w_up, b_a0, b_a_up, b_g_up, b_k_k, b_k_a, b_r_k, b_ln_g, b_ln_b, od_w_in, c_a_re, c_a_im, c_log_dt, c_b_re, c_b_im, c_c_re, c_c_im, c_d, c_w_glu, c_b_glu, od_w_out, w_router, moe_w_up, moe_w_down, ln_mix_g, ln_mix_b, ln_ffn_g, ln_ffn_b):
    bp, sp, d = x_prompt.shape
    bs, ss, _ = x_sample.shape
    lp = sp + N_META
    tp = bp * lp
    t = tp + bs * ss
    xp = jnp.concatenate([jnp.broadcast_to(meta[None], (bp, N_META, d)), x_prompt], axis=1)
    x = jnp.concatenate([xp.reshape(tp, d), x_sample.reshape(bs * ss, d)], axis=0)
    xb = x.astype(BF16)

    def ffn(x, xb, layer):
        f0, f1 = moe_layer(x, xb, w_router, moe_w_up[layer], moe_w_down[layer])
        return add_ln(x, f0, f1, ln_ffn_g[layer], ln_ffn_b[layer], TOK_TM)

    w_in = ev_w_in[0]
    zpad = jnp.zeros((d, P_B_OFF - A_COLS), w_in.dtype)
    w_in = jnp.concatenate([w_in[:, :A_COLS], zpad, w_in[:, A_COLS:]], axis=1).astype(BF16)
    p = matmul(xb, w_in, TOK_TM, 512)

    gla_w = (a_gate_up[0], a_gate_b[0], a_norm_g[0])
    oa_p, gla_p = gla_group(p, jnp.zeros((bp, A_HEADS, A_DK, A_DV), F32), *gla_w,
                            row_off=0, nbatch=bp, seq=lp, nb=4, lb=48, chunk=16)
    oa_s, gla_s = gla_group(p, state_gla[0], *gla_w,
                            row_off=tp, nbatch=bs, seq=ss, nb=4, lb=ss, chunk=ss)
    rw_w = (b_mu[0], b_w0[0], b_w_up[0], b_a0[0], b_a_up[0], b_g_up[0], b_k_k[0], b_k_a[0],
            b_r_k[0], b_ln_g[0], b_ln_b[0])
    ob_p, rwkv_p = rwkv_group(p, jnp.zeros((bp, B_HEADS, B_HEAD, B_HEAD), F32),
                              jnp.zeros((bp, B_COLS), F32), *rw_w,
                              row_off=0, nbatch=bp, seq=lp, nb=4, lb=48)
    ob_s, rwkv_s = rwkv_group(p, state_rwkv[0], state_shift[0], *rw_w,
                              row_off=tp, nbatch=bs, seq=ss, nb=4, lb=ss)
    shift_p = p[lp - 1:tp:lp, P_B_OFF:]
    shift_s = p[tp + ss - 1::ss, P_B_OFF:]
    o_a = jnp.concatenate([oa_p, oa_s], axis=0)
    o_b = jnp.concatenate([ob_p, ob_s], axis=0)
    x, xb = proj_ln([o_a, o_b], ev_w_out[0].astype(BF16), x, ln_mix_g[0], ln_mix_b[0], TOK_TM)
    x, xb = ffn(x, xb, 0)

    u = matmul(xb, od_w_in[0].astype(BF16), TOK_TM, 512)
    s5_w = (c_a_re[0], c_a_im[0], c_log_dt[0], c_b_re[0], c_b_im[0], c_c_re[0], c_c_im[0], c_d[0])
    zero_state = jnp.zeros((bp, C_GROUPS, C_STATE), F32)
    y_p, re_p, im_p = s5_group(u[:tp], zero_state, zero_state, s5_tables(*s5_w, 16),
                               nbatch=bp, seq=lp, chunk=16)
    y_s, re_s, im_s = s5_group(u[tp:], state_s5_re[0], state_s5_im[0], s5_tables(*s5_w, ss),
                               nbatch=bs, seq=ss, chunk=ss)
    y = jnp.concatenate([y_p, y_s], axis=0)
    z = gelu_glu(y, c_w_glu[0].astype(BF16), c_b_glu[0], TOK_TM, 512)
    x, xb = proj_ln([z], od_w_out[0].astype(BF16), x, ln_mix_g[1], ln_mix_b[1], TOK_TM)
    x, xb = ffn(x, xb, 1)

    y_prompt = x[:tp].reshape(bp, lp, d)[:, N_META:]
    y_sample = x[tp:].reshape(bs, ss, d)
    return (y_prompt, y_sample, gla_p[None], gla_s[None], rwkv_p[None], rwkv_s[None],
            shift_p[None], shift_s[None], re_p[None], re_s[None], im_p[None], im_s[None])
```

```python
import functools
import math

import jax
import jax.numpy as jnp
from jax import lax
from jax.experimental import pallas as pl
from jax.experimental.pallas import tpu as pltpu

F32 = jnp.float32
BF16 = jnp.bfloat16

D_MODEL = 2048
DEPTH = 2
N_META = 16

A_WIDTH = 1024
A_HEADS = 4
A_DV = 256
A_DK = 128
A_GATE_RANK = 16
A_GATE_TAU = 16.0
A_QK = A_HEADS * A_DK
A_COLS = 2 * A_QK + 2 * A_WIDTH + A_GATE_RANK

B_WIDTH = 1024
B_HEAD = 64
B_HEADS = 16
B_DECAY_RANK = 64
B_AAA_RANK = 64
B_GATE_RANK = 128
B_COLS = 3 * B_WIDTH + B_DECAY_RANK + B_AAA_RANK + B_GATE_RANK

C_GROUP = 16
C_GROUPS = 128
C_STATE = 64

N_EXPERTS = 16
N_EXPERT_GROUPS = 4
EXPERTS_PER_GROUP = 4
TOP_K = 2
D_EXPERT = 1024

ALPHA = (2.0 * DEPTH) ** 0.25
LN_EPS = 1e-5
HEAD_NORM_EPS = 1e-5
RWKV_GN_EPS = 64e-5

P_B_OFF = B_COLS
P_COLS = 2 * B_COLS

VMEM_LIMIT = 56 * 1024 * 1024
MXU_TILE = 256
SUBLANES = 8


def _cparams(sem):
    return pltpu.CompilerParams(dimension_semantics=sem, vmem_limit_bytes=VMEM_LIMIT)


def _cdiv(a, b):
    return (a + b - 1) // b


def _softplus(x):
    return jnp.maximum(x, 0.0) + jnp.log(1.0 + jnp.exp(-jnp.abs(x)))


def _sigmoid(x):
    return 1.0 / (1.0 + jnp.exp(-x))


def _mm_kernel(x_ref, w_ref, o_ref):
    o_ref[...] = jnp.dot(x_ref[...].astype(BF16), w_ref[...].astype(BF16),
                         preferred_element_type=F32)


def matmul(x, w, tm, tn):
    m, k = x.shape
    n = w.shape[1]
    assert n % tn == 0
    return pl.pallas_call(
        _mm_kernel,
        out_shape=jax.ShapeDtypeStruct((m, n), F32),
        grid=(n // tn, _cdiv(m, tm)),
        in_specs=[pl.BlockSpec((tm, k), lambda j, i: (i, 0)),
                  pl.BlockSpec((k, tn), lambda j, i: (0, j))],
        out_specs=pl.BlockSpec((tm, tn), lambda j, i: (i, j)),
        compiler_params=_cparams(("arbitrary", "arbitrary")),
        name="matmul",
    )(x, w)


def _layer_norm_rows(y, g, b):
    mu = jnp.mean(y, axis=-1, keepdims=True)
    yc = y - mu
    var = jnp.mean(yc * yc, axis=-1, keepdims=True)
    return yc * lax.rsqrt(var + LN_EPS) * g + b


def _proj_ln_kernel(n_lhs, *refs):
    lhs = refs[:n_lhs]
    w_ref, x_ref, g_ref, b_ref, o_ref, ob_ref = refs[n_lhs:]
    acc = None
    off = 0
    for r in lhs:
        kk = r.shape[1]
        part = jnp.dot(r[...], w_ref[off:off + kk, :], preferred_element_type=F32)
        acc = part if acc is None else acc + part
        off += kk
    y = _layer_norm_rows(ALPHA * x_ref[...] + acc, g_ref[...], b_ref[...])
    o_ref[...] = y
    ob_ref[...] = y.astype(BF16)


def proj_ln(lhs_list, w_bf16, x, g, b, tm):
    m, d = x.shape
    kin = w_bf16.shape[0]
    in_specs = [pl.BlockSpec((tm, l.shape[1]), lambda i: (i, 0)) for l in lhs_list]
    in_specs += [pl.BlockSpec((kin, d), lambda i: (0, 0)),
                 pl.BlockSpec((tm, d), lambda i: (i, 0)),
                 pl.BlockSpec((1, d), lambda i: (0, 0)),
                 pl.BlockSpec((1, d), lambda i: (0, 0))]
    return pl.pallas_call(
        functools.partial(_proj_ln_kernel, len(lhs_list)),
        out_shape=(jax.ShapeDtypeStruct((m, d), F32), jax.ShapeDtypeStruct((m, d), BF16)),
        grid=(_cdiv(m, tm),),
        in_specs=in_specs,
        out_specs=(pl.BlockSpec((tm, d), lambda i: (i, 0)),
                   pl.BlockSpec((tm, d), lambda i: (i, 0))),
        compiler_params=_cparams(("arbitrary",)),
        name="proj_ln",
    )(*lhs_list, w_bf16, x, g.reshape(1, d), b.reshape(1, d))


def _add_ln_kernel(x_ref, f0_ref, f1_ref, g_ref, b_ref, o_ref, ob_ref):
    y = _layer_norm_rows(ALPHA * x_ref[...] + (f0_ref[...] + f1_ref[...]), g_ref[...], b_ref[...])
    o_ref[...] = y
    ob_ref[...] = y.astype(BF16)


def add_ln(x, f0, f1, g, b, tm):
    m, d = x.shape
    row = pl.BlockSpec((tm, d), lambda i: (i, 0))
    vec = pl.BlockSpec((1, d), lambda i: (0, 0))
    return pl.pallas_call(
        _add_ln_kernel,
        out_shape=(jax.ShapeDtypeStruct((m, d), F32), jax.ShapeDtypeStruct((m, d), BF16)),
        grid=(_cdiv(m, tm),),
        in_specs=[row, row, row, vec, vec],
        out_specs=(row, row),
        compiler_params=_cparams(("arbitrary",)),
        name="add_ln",
    )(x, f0, f1, g.reshape(1, d), b.reshape(1, d))


def _gla_kernel(nb, lb, chunk, *refs):
    p_refs = refs[:nb]
    (s0_ref, gup_ref, gb_ref, ng_ref) = refs[nb:nb + 4]
    o_ref = refs[nb + 4]
    sout_ref = refs[nb + 5]
    s_ref = refs[nb + 6]
    blk = pl.program_id(1)

    @pl.when(blk == 0)
    def _():
        s_ref[...] = s0_ref[...]

    rows = lax.broadcasted_iota(jnp.int32, (chunk, chunk), 0)
    cols = lax.broadcasted_iota(jnp.int32, (chunk, chunk), 1)
    tril = rows >= cols
    tril_f = tril.astype(F32)
    for j in range(nb):
        p_ref = p_refs[j]
        gd = p_ref[:, 2 * A_QK + 2 * A_WIDTH:2 * A_QK + 2 * A_WIDTH + A_GATE_RANK]
        z = jnp.dot(gd, gup_ref[...], preferred_element_type=F32) + gb_ref[...]
        lg_all = -_softplus(-z) * (1.0 / A_GATE_TAU)
        for c in range(lb // chunk):
            r0 = c * chunk
            lg = lg_all[r0:r0 + chunk, :]
            bc = jnp.dot(tril_f, lg, preferred_element_type=F32, precision=lax.Precision.HIGHEST)
            for h in range(A_HEADS):
                q = p_ref[r0:r0 + chunk, h * A_DK:(h + 1) * A_DK] * (A_DK ** -0.5)
                k = p_ref[r0:r0 + chunk, A_QK + h * A_DK:A_QK + (h + 1) * A_DK]
                v = p_ref[r0:r0 + chunk, 2 * A_QK + h * A_DV:2 * A_QK + (h + 1) * A_DV]
                rg = p_ref[r0:r0 + chunk,
                           2 * A_QK + A_WIDTH + h * A_DV:2 * A_QK + A_WIDTH + (h + 1) * A_DV]
                b = bc[:, h * A_DK:(h + 1) * A_DK]
                bl = b[chunk - 1:chunk, :]
                qd = (q * jnp.exp(b)).astype(BF16)
                kd = (k * jnp.exp(-b)).astype(BF16)
                vb = v.astype(BF16)
                att = lax.dot_general(qd, kd, (((1,), (1,)), ((), ())), preferred_element_type=F32)
                att = jnp.where(tril, att, 0.0).astype(BF16)
                s = s_ref[j, h]
                o = (jnp.dot(att, vb, preferred_element_type=F32)
                     + jnp.dot(qd, s.astype(BF16), preferred_element_type=F32))
                kl = (k * jnp.exp(bl - b)).astype(BF16)
                kv = lax.dot_general(kl, vb, (((0,), (0,)), ((), ())), preferred_element_type=F32)
                dec = jnp.broadcast_to(jnp.exp(bl), (A_DK, A_DK)).T
                s_ref[j, h] = s * jnp.concatenate([dec] * (A_DV // A_DK), axis=1) + kv
                o = o * lax.rsqrt(jnp.mean(o * o, axis=-1, keepdims=True) + HEAD_NORM_EPS)
                o = o * ng_ref[:, h * A_DV:(h + 1) * A_DV] * (rg * _sigmoid(rg))
                o_ref[j, r0:r0 + chunk, h * A_DV:(h + 1) * A_DV] = o.astype(BF16)

    @pl.when(blk == pl.num_programs(1) - 1)
    def _():
        sout_ref[...] = s_ref[...]


def gla_group(p, s0, gate_up, gate_b, norm_g, *, row_off, nbatch, seq, nb, lb, chunk):
    t = p.shape[0]
    nblk = seq // lb
    assert seq % lb == 0 and lb % chunk == 0 and nbatch % nb == 0 and row_off % lb == 0
    base = row_off // lb

    def p_map(j):
        return lambda bi, blk: (base + (bi * nb + j) * nblk + blk, 0)

    in_specs = [pl.BlockSpec((lb, P_B_OFF), p_map(j)) for j in range(nb)]
    in_specs += [pl.BlockSpec((nb, A_HEADS, A_DK, A_DV), lambda bi, blk: (bi, 0, 0, 0)),
                 pl.BlockSpec((A_GATE_RANK, A_QK), lambda bi, blk: (0, 0)),
                 pl.BlockSpec((1, A_QK), lambda bi, blk: (0, 0)),
                 pl.BlockSpec((1, A_WIDTH), lambda bi, blk: (0, 0))]

    out_specs = [pl.BlockSpec((nb, lb, A_WIDTH), lambda bi, blk: (bi, blk, 0)),
                 pl.BlockSpec((nb, A_HEADS, A_DK, A_DV), lambda bi, blk: (bi, 0, 0, 0))]
    out_shape = [jax.ShapeDtypeStruct((nbatch, seq, A_WIDTH), BF16),
                 jax.ShapeDtypeStruct((nbatch, A_HEADS, A_DK, A_DV), F32)]
    o, s_new = pl.pallas_call(
        functools.partial(_gla_kernel, nb, lb, chunk),
        out_shape=out_shape,
        grid=(nbatch // nb, nblk),
        in_specs=in_specs,
        out_specs=out_specs,
        scratch_shapes=[pltpu.VMEM((nb, A_HEADS, A_DK, A_DV), F32)],
        compiler_params=_cparams(("arbitrary", "arbitrary")),
        name="gla",
    )(*([p] * nb), s0, gate_up, gate_b.reshape(1, A_QK), norm_g.reshape(1, A_WIDTH))
    return o.reshape(nbatch * seq, A_WIDTH), s_new


VT_SLOT = 128


def _head_ones(dtype):
    r = lax.broadcasted_iota(jnp.int32, (MXU_TILE, MXU_TILE), 0) // B_HEAD
    c = lax.broadcasted_iota(jnp.int32, (MXU_TILE, MXU_TILE), 1) // B_HEAD
    return (r == c).astype(dtype)


def _head_sum(x, ones):
    xh = x.astype(BF16)
    xl = (x - xh.astype(F32)).astype(BF16)
    parts = [jnp.dot(xh[:, c * MXU_TILE:(c + 1) * MXU_TILE], ones, preferred_element_type=F32)
             + jnp.dot(xl[:, c * MXU_TILE:(c + 1) * MXU_TILE], ones, preferred_element_type=F32)
             for c in range(B_WIDTH // MXU_TILE)]
    return jnp.concatenate(parts, axis=-1)


def _rwkv_kernel(nb, lb, *refs):
    p_refs = refs[:nb]
    (shift0_ref, s0_ref, mu_ref, w0_ref, wup_ref, a0_ref, aup_ref, gup_ref,
     kk_ref, ka_ref, rk_ref, lng_ref, lnb_ref) = refs[nb:nb + 13]
    o_ref, sout_ref = refs[nb + 13:nb + 15]
    (s_ref, prev_ref, r_s, w_s, kh_s, kl_s, kk_s, kka_s, on_s, g_s, bon_s, vmh_s, vml_s, vt_s
     ) = refs[nb + 15:]
    blk = pl.program_id(1)
    W = B_WIDTH
    H = B_HEADS
    NT = (((1,), (1,)), ((), ()))

    @pl.when(blk == 0)
    def _():
        s_ref[...] = s0_ref[...]
        prev_ref[...] = shift0_ref[...]

    ones = _head_ones(BF16)
    hmask = (lax.broadcasted_iota(jnp.int32, (H, W), 1) // B_HEAD
             == lax.broadcasted_iota(jnp.int32, (H, W), 0))

    def to_head_rows(ref, j, x):
        ref[j] = jnp.zeros(ref.shape[1:], F32)
        for h in range(H):
            ref[j, pl.ds(h, lb, stride=H), 0:B_HEAD] = x[:, h * B_HEAD:(h + 1) * B_HEAD]

    def split(x):
        xh = x.astype(BF16).astype(F32)
        return xh, x - xh

    for j in range(nb):
        pb = p_refs[j][...]
        first = lax.broadcasted_iota(jnp.int32, (lb, 1), 0) == 0
        prev = jnp.where(first, prev_ref[j], pltpu.roll(pb, 1, axis=0))
        prev_ref[j] = pb[lb - 1:lb, :]
        xm = pb + (prev - pb) * mu_ref[...]
        r = xm[:, :W]
        k = xm[:, W:2 * W]
        v = xm[:, 2 * W:3 * W]
        wd = xm[:, 3 * W:3 * W + B_DECAY_RANK]
        ad = xm[:, 3 * W + B_DECAY_RANK:3 * W + B_DECAY_RANK + B_AAA_RANK]
        gd = xm[:, 3 * W + B_DECAY_RANK + B_AAA_RANK:]
        w = -_softplus(-(w0_ref[...] + jnp.dot(jnp.tanh(wd).astype(BF16), wup_ref[...].astype(BF16),
                                               preferred_element_type=F32))) - 0.5
        a = _sigmoid(a0_ref[...] + jnp.dot(ad.astype(BF16), aup_ref[...].astype(BF16),
                                           preferred_element_type=F32))
        g = jnp.dot(_sigmoid(gd).astype(BF16), gup_ref[...].astype(BF16), preferred_element_type=F32)
        kk = k * kk_ref[...]
        nrm = jnp.sqrt(_head_sum(kk * kk, ones))
        kk = kk / jnp.maximum(nrm, 1e-12)
        k2 = k * (1.0 + (a - 1.0) * ka_ref[...])
        r_s[j] = r
        w_s[j] = jnp.exp(-jnp.exp(w))
        kh_s[j], kl_s[j] = split(k2)
        kk_s[j] = kk
        kka_s[j] = -(kk * a)
        vh, vl = split(v)
        to_head_rows(vmh_s, j, vh)
        to_head_rows(vml_s, j, vl)
        g_s[j] = g
        bon_s[j] = _head_sum(r * k2 * rk_ref[...], ones) * v

    def head_rows(ref, j, t):
        return jnp.where(hmask, jnp.broadcast_to(ref[j, pl.ds(t, 1), :], (H, W)), 0.0).astype(BF16)

    def readout(j, t, sb):
        rows = pl.ds(t * H if isinstance(t, int) else pl.multiple_of(t * H, H), H)
        on_s[j, rows, :] = lax.dot_general(head_rows(r_s, j, t), sb, NT, preferred_element_type=F32)

    zrows = lambda n: jnp.zeros((n * H, W), BF16)
    lane = lax.broadcasted_iota(jnp.int32, (B_HEAD, VT_SLOT), 1)

    def vt_step(t, carry):
        rows = pl.ds(pl.multiple_of(t * H, H), H)
        for j in range(nb):
            vmh, vml = vmh_s[j, rows, :], vml_s[j, rows, :]
            tile = jnp.concatenate([vmh, vml, vmh, jnp.zeros((VT_SLOT - 3 * H, VT_SLOT), F32)], axis=0)
            vt_s[j, t] = tile.T[:B_HEAD, :].astype(BF16)
        return carry

    lax.fori_loop(0, lb, vt_step, 0)

    def step(t, carry):
        sbs, sas = [], []
        for j in range(nb):
            sb = s_ref[j].astype(BF16)
            kkw = head_rows(kk_s, j, t)
            sas.append(lax.dot_general(
                sb, jnp.concatenate([zrows(3), kkw, kkw, zrows(VT_SLOT // H - 5)], axis=0),
                NT, preferred_element_type=F32))
            sbs.append(sb)
        for j in range(nb):
            readout(j, jnp.maximum(t - 1, 0), sbs[j])
        for j in range(nb):
            sa = sas[j]
            sa_hi = sa.astype(BF16)
            sa_lo = (sa - sa_hi.astype(F32)).astype(BF16)
            x = jnp.where(lane < 3 * H, vt_s[j, t], jnp.where(lane < 4 * H, sa_hi, sa_lo))
            khw, klw, kaw = head_rows(kh_s, j, t), head_rows(kl_s, j, t), head_rows(kka_s, j, t)
            upd = jnp.dot(x[:, :5 * H], jnp.concatenate([khw, khw, klw, kaw, kaw], axis=0),
                          preferred_element_type=F32)
            s_ref[j] = s_ref[j] * w_s[j, pl.ds(t, 1), :] + upd
        return carry

    lax.fori_loop(0, lb, step, 0)

    for j in range(nb):
        readout(j, lb - 1, s_ref[j].astype(BF16))
        o = jnp.concatenate([on_s[j, pl.ds(h, lb, stride=H), :] for h in range(H)], axis=-1)
        oc = o - _head_sum(o, ones) * (1.0 / B_HEAD)
        var = _head_sum(oc * oc, ones) * (1.0 / B_HEAD)
        o = oc * lax.rsqrt(var + RWKV_GN_EPS) * lng_ref[...] + lnb_ref[...]
        o_ref[j] = ((o + bon_s[j]) * g_s[j]).astype(BF16)

    @pl.when(blk == pl.num_programs(1) - 1)
    def _():
        sout_ref[...] = s_ref[...]


def rwkv_group(p, s0, shift0, mu, w0, w_up, a0, a_up, g_up, k_k, k_a, r_k, ln_g, ln_b,
               *, row_off, nbatch, seq, nb, lb):
    nblk = seq // lb
    assert seq % lb == 0 and nbatch % nb == 0 and row_off % lb == 0
    base = row_off // lb
    W = B_WIDTH
    s0 = s0.transpose(0, 2, 1, 3).reshape(nbatch, B_HEAD, W)

    def p_map(j):
        return lambda bi, blk: (base + (bi * nb + j) * nblk + blk, 1)

    const = lambda shape: pl.BlockSpec(shape, lambda bi, blk: (0,) * len(shape))
    in_specs = [pl.BlockSpec((lb, B_COLS), p_map(j)) for j in range(nb)]
    in_specs += [pl.BlockSpec((nb, 1, B_COLS), lambda bi, blk: (bi, 0, 0)),
                 pl.BlockSpec((nb, B_HEAD, W), lambda bi, blk: (bi, 0, 0)),
                 const((1, B_COLS)), const((1, W)), const((B_DECAY_RANK, W)), const((1, W)),
                 const((B_AAA_RANK, W)), const((B_GATE_RANK, W)),
                 const((1, W)), const((1, W)), const((1, W)), const((1, W)), const((1, W))]
    out_specs = [pl.BlockSpec((nb, lb, W), lambda bi, blk: (bi, blk, 0)),
                 pl.BlockSpec((nb, B_HEAD, W), lambda bi, blk: (bi, 0, 0))]
    out_shape = [jax.ShapeDtypeStruct((nbatch, seq, W), BF16),
                 jax.ShapeDtypeStruct((nbatch, B_HEAD, W), F32)]
    tok = pltpu.VMEM((nb, lb, W), F32)
    o, s_new = pl.pallas_call(
        functools.partial(_rwkv_kernel, nb, lb),
        out_shape=out_shape,
        grid=(nbatch // nb, nblk),
        in_specs=in_specs,
        out_specs=out_specs,
        scratch_shapes=[pltpu.VMEM((nb, B_HEAD, W), F32), pltpu.VMEM((nb, 1, B_COLS), F32)]
        + [tok] * 6 + [pltpu.VMEM((nb, lb * B_HEADS, B_HEAD), F32), tok, tok,
                       pltpu.VMEM((nb, lb * B_HEADS, VT_SLOT), F32),
                       pltpu.VMEM((nb, lb * B_HEADS, VT_SLOT), F32),
                       pltpu.VMEM((nb, lb, B_HEAD, VT_SLOT), BF16)],
        compiler_params=_cparams(("arbitrary", "arbitrary")),
        name="rwkv",
    )(*([p] * nb), shift0.reshape(nbatch, 1, B_COLS), s0, mu.reshape(1, B_COLS),
      w0.reshape(1, W), w_up, a0.reshape(1, W), a_up, g_up, k_k.reshape(1, W),
      k_a.reshape(1, W), r_k.reshape(1, W), ln_g.reshape(1, W), ln_b.reshape(1, W))
    s_new = s_new.reshape(nbatch, B_HEAD, B_HEADS, B_HEAD).transpose(0, 2, 1, 3)
    return o.reshape(nbatch * seq, W), s_new


GP = 2
SP = C_GROUPS * C_STATE


def s5_tables(a_re, a_im, log_dt, b_re, b_im, c_re, c_im, d, chunk):
    G, P, c = C_GROUPS, C_STATE, C_GROUP
    hi = lax.Precision.HIGHEST
    dt = jnp.exp(log_dt)[:, None]
    m = jnp.arange(chunk + 1, dtype=F32)[:, None, None]
    mag = jnp.exp(m * (dt * a_re))
    ang = m * (dt * a_im)
    pw_re, pw_im = mag * jnp.cos(ang), mag * jnp.sin(ang)
    num_re, num_im = pw_re[1] - 1.0, pw_im[1]
    den = a_re * a_re + a_im * a_im
    q_re = (num_re * a_re + num_im * a_im) / den
    q_im = (num_im * a_re - num_re * a_im) / den
    bb_re = q_re[..., None] * b_re - q_im[..., None] * b_im
    bb_im = q_re[..., None] * b_im + q_im[..., None] * b_re
    ca_re = c_re[None] * pw_re[:, :, None, :] - c_im[None] * pw_im[:, :, None, :]
    ca_im = c_re[None] * pw_im[:, :, None, :] + c_im[None] * pw_re[:, :, None, :]
    kern = (jnp.einsum('mgcp,gpd->mgdc', ca_re[:chunk], bb_re, precision=hi)
            - jnp.einsum('mgcp,gpd->mgdc', ca_im[:chunk], bb_im, precision=hi))
    tj = jnp.arange(chunk)
    lag = tj[None, :] - tj[:, None]
    toep = jnp.where((lag >= 0)[:, :, None, None, None], kern[jnp.clip(lag, 0, chunk - 1)], 0.0)
    toep = toep.transpose(2, 0, 3, 1, 4).reshape(G, chunk * c, chunk * c)
    dvec = jnp.tile(d.reshape(G, 1, c), (1, chunk, 1)).reshape(G, 1, chunk * c)
    rev_re, rev_im = pw_re[:chunk][::-1], pw_im[:chunk][::-1]
    bs_re = rev_re[..., None] * bb_re[None] - rev_im[..., None] * bb_im[None]
    bs_im = rev_re[..., None] * bb_im[None] + rev_im[..., None] * bb_re[None]
    to_in = lambda z: z.transpose(1, 0, 3, 2).reshape(G, chunk * c, P)
    to_out = lambda z: z.transpose(1, 3, 0, 2).reshape(G, P, chunk * c)
    cs_re, cs_im = to_out(ca_re[1:]), to_out(-ca_im[1:])
    even = (jnp.arange(G) % GP == 0)[:, None, None]

    def pad_lanes(z):
        zero = jnp.zeros_like(z)
        return jnp.where(even, jnp.concatenate([z, zero], -1), jnp.concatenate([zero, z], -1))

    def pad_rows(z):
        zero = jnp.zeros_like(z)
        return jnp.where(even, jnp.concatenate([z, zero], 1), jnp.concatenate([zero, z], 1))

    return dict(toep=toep.astype(BF16), dvec=dvec,
                bs_re=pad_lanes(to_in(bs_re)).astype(BF16), bs_im=pad_lanes(to_in(bs_im)).astype(BF16),
                cs_re=pad_rows(cs_re).astype(BF16), cs_im=pad_rows(cs_im).astype(BF16),
                ac_re=pw_re[chunk].reshape(1, SP), ac_im=pw_im[chunk].reshape(1, SP))


def _s5_in_kernel(uf_ref, bre_ref, bim_ref, vre_ref, vim_ref):
    re = None
    im = None
    for q in range(GP):
        u = uf_ref[q].astype(BF16)
        pr = jnp.dot(u, bre_ref[q], preferred_element_type=F32)
        pi = jnp.dot(u, bim_ref[q], preferred_element_type=F32)
        re = pr if re is None else re + pr
        im = pi if im is None else im + pi
    vre_ref[...] = re
    vim_ref[...] = im


def _s5_scan_kernel(nbatch, nchunks, cpi, vre_ref, vim_ref, s0re_ref, s0im_ref, acre_ref, acim_ref,
                    sre_ref, sim_ref, fre_ref, fim_ref):
    ar = acre_ref[...]
    ai = acim_ref[...]
    n_iter = _cdiv(nchunks, cpi)
    nrow = cpi * nbatch

    def iteration(i, carry):
        sr, si = carry
        start = i * nrow
        rows = pl.ds(start if isinstance(start, int) else pl.multiple_of(start, nrow), nrow)
        vre = vre_ref[rows, :]
        vim = vim_ref[rows, :]
        starts_r, starts_i, after = [], [], []
        for q in range(cpi):
            starts_r.append(sr)
            starts_i.append(si)
            vr = vre[q * nbatch:(q + 1) * nbatch]
            vi = vim[q * nbatch:(q + 1) * nbatch]
            sr, si = ar * sr - ai * si + vr, ar * si + ai * sr + vi
            after.append((sr, si))
        sre_ref[rows, :] = jnp.concatenate(starts_r, axis=0) if cpi > 1 else starts_r[0]
        sim_ref[rows, :] = jnp.concatenate(starts_i, axis=0) if cpi > 1 else starts_i[0]
        return (sr, si), after

    carry = lax.fori_loop(0, n_iter - 1, lambda i, c: iteration(i, c)[0],
                          (s0re_ref[...], s0im_ref[...]))
    _, after = iteration(n_iter - 1, carry)
    fre_ref[...], fim_ref[...] = after[(nchunks - 1) % cpi]


def _s5_out_kernel(uf_ref, toep_ref, dvec_ref, cre_ref, cim_ref, sre_ref, sim_ref, y_ref):
    sre = sre_ref[...].astype(BF16)
    sim = sim_ref[...].astype(BF16)
    for q in range(GP):
        u = uf_ref[q]
        y = (jnp.dot(u.astype(BF16), toep_ref[q], preferred_element_type=F32)
             + jnp.dot(sre, cre_ref[q], preferred_element_type=F32)
             + jnp.dot(sim, cim_ref[q], preferred_element_type=F32))
        y_ref[q] = y + dvec_ref[q] * u


def s5_group(u, s0_re, s0_im, tab, *, nbatch, seq, chunk):
    G, P, c = C_GROUPS, C_STATE, C_GROUP
    nchunks = seq // chunk
    assert seq % chunk == 0
    cpi = max(1, SUBLANES // nbatch)
    nch_pad = _cdiv(nchunks, cpi) * cpi
    R = nch_pad * nbatch
    cr = chunk * c
    uf = u.reshape(nbatch, nchunks, chunk, G, c).transpose(3, 1, 0, 2, 4)
    uf = jnp.pad(uf, ((0, 0), (0, nch_pad - nchunks), (0, 0), (0, 0), (0, 0))).reshape(G, R, cr)
    grp = lambda shape: pl.BlockSpec((GP,) + shape, lambda g: (g, 0, 0))
    lane_blk = pl.BlockSpec((R, GP * P), lambda g: (0, g))
    v_re, v_im = pl.pallas_call(
        _s5_in_kernel,
        out_shape=[jax.ShapeDtypeStruct((R, SP), F32)] * 2,
        grid=(G // GP,),
        in_specs=[grp((R, cr)), grp((cr, GP * P)), grp((cr, GP * P))],
        out_specs=[lane_blk, lane_blk],
        compiler_params=_cparams(("arbitrary",)),
        name="s5_in",
    )(uf, tab['bs_re'], tab['bs_im'])

    tl = 1024
    col = lambda rows: pl.BlockSpec((rows, tl), lambda i: (0, i))
    s_re, s_im, f_re, f_im = pl.pallas_call(
        functools.partial(_s5_scan_kernel, nbatch, nchunks, cpi),
        out_shape=[jax.ShapeDtypeStruct((R, SP), F32)] * 2 + [jax.ShapeDtypeStruct((nbatch, SP), F32)] * 2,
        grid=(SP // tl,),
        in_specs=[col(R), col(R), col(nbatch), col(nbatch), col(1), col(1)],
        out_specs=[col(R), col(R), col(nbatch), col(nbatch)],
        compiler_params=_cparams(("arbitrary",)),
        name="s5_scan",
    )(v_re, v_im, s0_re.reshape(nbatch, SP), s0_im.reshape(nbatch, SP), tab['ac_re'], tab['ac_im'])

    y = pl.pallas_call(
        _s5_out_kernel,
        out_shape=jax.ShapeDtypeStruct((G, R, cr), F32),
        grid=(G // GP,),
        in_specs=[grp((R, cr)), grp((cr, cr)), grp((1, cr)), grp((GP * P, cr)), grp((GP * P, cr)),
                  lane_blk, lane_blk],
        out_specs=grp((R, cr)),
        compiler_params=_cparams(("arbitrary",)),
        name="s5_out",
    )(uf, tab['toep'], tab['dvec'], tab['cs_re'], tab['cs_im'], s_re, s_im)
    y = y.reshape(G, nch_pad, nbatch, chunk, c)[:, :nchunks]
    y = y.transpose(2, 1, 3, 0, 4).reshape(nbatch * seq, G * c)
    return y, f_re.reshape(nbatch, G, P), f_im.reshape(nbatch, G, P)


def _glu_kernel(y_ref, w_ref, b_ref, o_ref):
    z = jax.nn.gelu(y_ref[...])
    acc = jnp.dot(z.astype(BF16), w_ref[...], preferred_element_type=F32) + b_ref[...]
    o_ref[...] = (z * _sigmoid(acc)).astype(BF16)


def gelu_glu(y, w_bf16, b, tm):
    m, k = y.shape
    n = w_bf16.shape[1]
    assert n == k
    return pl.pallas_call(
        _glu_kernel,
        out_shape=jax.ShapeDtypeStruct((m, n), BF16),
        grid=(_cdiv(m, tm),),
        in_specs=[pl.BlockSpec((tm, k), lambda i: (i, 0)),
                  pl.BlockSpec((k, n), lambda i: (0, 0)),
                  pl.BlockSpec((1, n), lambda i: (0, 0))],
        out_specs=pl.BlockSpec((tm, n), lambda i: (i, 0)),
        compiler_params=_cparams(("arbitrary",)),
        name="gelu_glu",
    )(y, w_bf16, b.reshape(1, n))


MOE_TM = 256
MOE_TF = 512
MOE_TN = 1024


def _router_kernel(x_ref, wr_ref, e_ref, g_ref):
    logits = lax.dot_general(wr_ref[...], x_ref[...], (((1,), (1,)), ((), ())),
                             preferred_element_type=F32, precision=lax.Precision.HIGHEST)
    mx = jnp.max(logits, axis=0, keepdims=True)
    ex = jnp.exp(logits - mx)
    probs = ex / jnp.sum(ex, axis=0, keepdims=True)
    neg = jnp.float32(-jnp.inf)
    best = None
    for gi in range(N_EXPERT_GROUPS):
        v = [probs[gi * EXPERTS_PER_GROUP + r:gi * EXPERTS_PER_GROUP + r + 1, :]
             for r in range(EXPERTS_PER_GROUP)]
        m1 = jnp.maximum(jnp.maximum(v[0], v[1]), jnp.maximum(v[2], v[3]))
        i1 = jnp.where(v[0] == m1, 0, jnp.where(v[1] == m1, 1, jnp.where(v[2] == m1, 2, 3)))
        w = [jnp.where(i1 == r, neg, v[r]) for r in range(EXPERTS_PER_GROUP)]
        m2 = jnp.maximum(jnp.maximum(w[0], w[1]), jnp.maximum(w[2], w[3]))
        i2 = jnp.where(w[0] == m2, 0, jnp.where(w[1] == m2, 1, jnp.where(w[2] == m2, 2, 3)))
        score = m1 + m2
        cand = (score, m1, m2, i1 + gi * EXPERTS_PER_GROUP, i2 + gi * EXPERTS_PER_GROUP)
        if best is None:
            best = cand
        else:
            take = cand[0] > best[0]
            best = tuple(jnp.where(take, cn, bs) for cn, bs in zip(cand, best))
    _, m1, m2, e1, e2 = best
    tot = m1 + m2
    e_ref[...] = jnp.concatenate([e1, e2], axis=0)
    g_ref[...] = jnp.concatenate([m1 / tot, m2 / tot], axis=0)


def router(x, w_router, tm):
    t, d = x.shape
    return pl.pallas_call(
        _router_kernel,
        out_shape=[jax.ShapeDtypeStruct((TOP_K, t), jnp.int32), jax.ShapeDtypeStruct((TOP_K, t), F32)],
        grid=(_cdiv(t, tm),),
        in_specs=[pl.BlockSpec((tm, d), lambda i: (i, 0)),
                  pl.BlockSpec((N_EXPERTS, d), lambda i: (0, 0))],
        out_specs=[pl.BlockSpec((TOP_K, tm), lambda i: (0, i)),
                   pl.BlockSpec((TOP_K, tm), lambda i: (0, i))],
        compiler_params=_cparams(("arbitrary",)),
        name="router",
    )(x, w_router.T)


def _moe_up_kernel(st_ref, sf_ref, se_ref, nu_ref, xs_ref, w1_ref, w2_ref, gate_ref, h_ref):
    @pl.when(pl.program_id(0) < nu_ref[0])
    def _():
        x = xs_ref[...]
        a = jnp.dot(x, w1_ref[0].astype(BF16), preferred_element_type=F32)
        b = jnp.dot(x, w2_ref[0].astype(BF16), preferred_element_type=F32)
        h_ref[...] = (a * _sigmoid(a) * b * gate_ref[...]).astype(BF16)


def _moe_down_kernel(st_ref, sf_ref, se_ref, nu_ref, h_ref, w_ref, y_ref):
    @pl.when(pl.program_id(0) < nu_ref[0])
    def _():
        y_ref[...] = jnp.dot(h_ref[...], w_ref[0].astype(BF16), preferred_element_type=F32)


def _moe_schedule(tiles_e, tile_start, nblocks, n_steps):
    steps_e = tiles_e * nblocks
    cum = jnp.cumsum(steps_e)
    used = cum[-1]
    s = jnp.minimum(jnp.arange(n_steps, dtype=jnp.int32), used - 1)
    e = jnp.minimum(jnp.sum((cum[None, :] <= s[:, None]).astype(jnp.int32), axis=1), N_EXPERTS - 1)
    local = s - (cum[e] - steps_e[e])
    te = jnp.maximum(tiles_e[e], 1)
    return (tile_start[e] + local % te).astype(jnp.int32), (local // te).astype(jnp.int32), e, \
        used.reshape(1).astype(jnp.int32)


def moe_layer(x, xb, w_router, w_up, w_down):
    t, d = x.shape
    tm = MOE_TM
    eid, gates = router(x, w_router, 1024)
    eid = eid.reshape(-1)
    na = TOP_K * t
    n_tiles = _cdiv(na + N_EXPERTS * (tm - 1), tm)
    npad = n_tiles * tm
    onehot = (eid[None, :] == jnp.arange(N_EXPERTS, dtype=jnp.int32)[:, None]).astype(jnp.int32)
    csum = jnp.cumsum(onehot, axis=1)
    counts = csum[:, -1]
    rank = jnp.sum(csum * onehot, axis=0) - 1
    tiles_e = (counts + tm - 1) // tm
    tile_start = jnp.cumsum(tiles_e) - tiles_e
    pos = tile_start[eid] * tm + rank
    row_a = jnp.full((npad,), -1, jnp.int32).at[pos].set(
        jnp.arange(na, dtype=jnp.int32), mode="promise_in_bounds", unique_indices=True)
    a_c = jnp.maximum(row_a, 0)
    row_src = jnp.where(a_c >= t, a_c - t, a_c)
    row_gate = jnp.where(row_a >= 0, gates.reshape(-1).at[a_c].get(mode="promise_in_bounds"), 0.0)
    xs = xb.at[row_src].get(mode="promise_in_bounds")

    nf = D_EXPERT // MOE_TF
    st, sf, se, nu = _moe_schedule(tiles_e, tile_start, nf, n_tiles * nf)
    h = pl.pallas_call(
        _moe_up_kernel,
        out_shape=jax.ShapeDtypeStruct((npad, D_EXPERT), BF16),
        grid_spec=pltpu.PrefetchScalarGridSpec(
            num_scalar_prefetch=4,
            grid=(n_tiles * nf,),
            in_specs=[pl.BlockSpec((tm, d), lambda s, st, sf, se, nu: (st[s], 0)),
                      pl.BlockSpec((1, d, MOE_TF), lambda s, st, sf, se, nu: (se[s], 0, sf[s])),
                      pl.BlockSpec((1, d, MOE_TF), lambda s, st, sf, se, nu: (se[s], 0, nf + sf[s])),
                      pl.BlockSpec((tm, 1), lambda s, st, sf, se, nu: (st[s], 0))],
            out_specs=pl.BlockSpec((tm, MOE_TF), lambda s, st, sf, se, nu: (st[s], sf[s]))),
        compiler_params=_cparams(("arbitrary",)),
        name="moe_up",
    )(st, sf, se, nu, xs, w_up, w_up, row_gate.reshape(npad, 1))

    nn = d // MOE_TN
    st, sf, se, nu = _moe_schedule(tiles_e, tile_start, nn, n_tiles * nn)
    ys = pl.pallas_call(
        _moe_down_kernel,
        out_shape=jax.ShapeDtypeStruct((npad, d), F32),
        grid_spec=pltpu.PrefetchScalarGridSpec(
            num_scalar_prefetch=4,
            grid=(n_tiles * nn,),
            in_specs=[pl.BlockSpec((tm, D_EXPERT), lambda s, st, sf, se, nu: (st[s], 0)),
                      pl.BlockSpec((1, D_EXPERT, MOE_TN), lambda s, st, sf, se, nu: (se[s], 0, sf[s]))],
            out_specs=pl.BlockSpec((tm, MOE_TN), lambda s, st, sf, se, nu: (st[s], sf[s]))),
        compiler_params=_cparams(("arbitrary",)),
        name="moe_down",
    )(st, sf, se, nu, h, w_down)
    return (ys.at[pos[:t]].get(mode="promise_in_bounds", unique_indices=True),
            ys.at[pos[t:]].get(mode="promise_in_bounds", unique_indices=True))


TOK_TM = 512
MM_TM = 1024


def kernel(x_prompt, x_sample, state_gla, state_rwkv, state_shift, state_s5_re, state_s5_im, meta, ev_w_in, ev_w_out, a_gate_up, a_gate_b, a_norm_g, b_mu, b_w0, b_w_up, b_a0, b_a_up, b_g_up, b_k_k, b_k_a, b_r_k, b_ln_g, b_ln_b, od_w_in, c_a_re, c_a_im, c_log_dt, c_b_re, c_b_im, c_c_re, c_c_im, c_d, c_w_glu, c_b_glu, od_w_out, w_router, moe_w_up, moe_w_down, ln_mix_g, ln_mix_b, ln_ffn_g, ln_ffn_b):
    bp, sp, d = x_prompt.shape
    bs, ss, _ = x_sample.shape
    lp = sp + N_META
    tp = bp * lp
    t = tp + bs * ss
    xp = jnp.concatenate([jnp.broadcast_to(meta[None], (bp, N_META, d)), x_prompt], axis=1)
    x = jnp.concatenate([xp.reshape(tp, d), x_sample.reshape(bs * ss, d)], axis=0)
    xb = x.astype(BF16)

    def ffn(x, xb, layer):
        f0, f1 = moe_layer(x, xb, w_router, moe_w_up[layer], moe_w_down[layer])
        return add_ln(x, f0, f1, ln_ffn_g[layer], ln_ffn_b[layer], TOK_TM)

    w_in = ev_w_in[0]
    zpad = jnp.zeros((d, P_B_OFF - A_COLS), w_in.dtype)
    w_in = jnp.concatenate([w_in[:, :A_COLS], zpad, w_in[:, A_COLS:]], axis=1).astype(BF16)
    p = matmul(xb, w_in, MM_TM, P_COLS // 4)

    gla_w = (a_gate_up[0], a_gate_b[0], a_norm_g[0])
    oa_p, gla_p = gla_group(p, jnp.zeros((bp, A_HEADS, A_DK, A_DV), F32), *gla_w,
                            row_off=0, nbatch=bp, seq=lp, nb=4, lb=48, chunk=16)
    oa_s, gla_s = gla_group(p, state_gla[0], *gla_w,
                            row_off=tp, nbatch=bs, seq=ss, nb=4, lb=ss, chunk=ss)
    rw_w = (b_mu[0], b_w0[0], b_w_up[0], b_a0[0], b_a_up[0], b_g_up[0], b_k_k[0], b_k_a[0],
            b_r_k[0], b_ln_g[0], b_ln_b[0])
    ob_p, rwkv_p = rwkv_group(p, jnp.zeros((bp, B_HEADS, B_HEAD, B_HEAD), F32),
                              jnp.zeros((bp, B_COLS), F32), *rw_w,
                              row_off=0, nbatch=bp, seq=lp, nb=4, lb=48)
    ob_s, rwkv_s = rwkv_group(p, state_rwkv[0], state_shift[0], *rw_w,
                              row_off=tp, nbatch=bs, seq=ss, nb=4, lb=ss)
    shift_p = p[lp - 1:tp:lp, P_B_OFF:]
    shift_s = p[tp + ss - 1::ss, P_B_OFF:]
    o_a = jnp.concatenate([oa_p, oa_s], axis=0)
    o_b = jnp.concatenate([ob_p, ob_s], axis=0)
    x, xb = proj_ln([o_a, o_b], ev_w_out[0].astype(BF16), x, ln_mix_g[0], ln_mix_b[0], TOK_TM)
    x, xb = ffn(x, xb, 0)

    u = matmul(xb, od_w_in[0].astype(BF16), MM_TM, d // 2)
    s5_w = (c_a_re[0], c_a_im[0], c_log_dt[0], c_b_re[0], c_b_im[0], c_c_re[0], c_c_im[0], c_d[0])
    zero_state = jnp.zeros((bp, C_GROUPS, C_STATE), F32)
    y_p, re_p, im_p = s5_group(u[:tp], zero_state, zero_state, s5_tables(*s5_w, 16),
                               nbatch=bp, seq=lp, chunk=16)
    y_s, re_s, im_s = s5_group(u[tp:], state_s5_re[0], state_s5_im[0], s5_tables(*s5_w, ss),
                               nbatch=bs, seq=ss, chunk=ss)
    y = jnp.concatenate([y_p, y_s], axis=0)
    z = gelu_glu(y, c_w_glu[0].astype(BF16), c_b_glu[0], TOK_TM)
    x, xb = proj_ln([z], od_w_out[0].astype(BF16), x, ln_mix_g[1], ln_mix_b[1], TOK_TM)
    x, xb = ffn(x, xb, 1)

    y_prompt = x[:tp].reshape(bp, lp, d)[:, N_META:]
    y_sample = x[tp:].reshape(bs, ss, d)
    return (y_prompt, y_sample, gla_p[None], gla_s[None], rwkv_p[None], rwkv_s[None],
            shift_p[None], shift_s[None], re_p[None], re_s[None], im_p[None], im_s[None])
```

```python
import functools
import math

import jax
import jax.numpy as jnp
from jax import lax
from jax.experimental import pallas as pl
from jax.experimental.pallas import tpu as pltpu

F32 = jnp.float32
BF16 = jnp.bfloat16

D_MODEL = 2048
DEPTH = 2
N_META = 16

A_WIDTH = 1024
A_HEADS = 4
A_DV = 256
A_DK = 128
A_GATE_RANK = 16
A_GATE_TAU = 16.0
A_QK = A_HEADS * A_DK
A_COLS = 2 * A_QK + 2 * A_WIDTH + A_GATE_RANK

B_WIDTH = 1024
B_HEAD = 64
B_HEADS = 16
B_DECAY_RANK = 64
B_AAA_RANK = 64
B_GATE_RANK = 128
B_COLS = 3 * B_WIDTH + B_DECAY_RANK + B_AAA_RANK + B_GATE_RANK

C_GROUP = 16
C_GROUPS = 128
C_STATE = 64

N_EXPERTS = 16
N_EXPERT_GROUPS = 4
EXPERTS_PER_GROUP = 4
TOP_K = 2
D_EXPERT = 1024

ALPHA = (2.0 * DEPTH) ** 0.25
LN_EPS = 1e-5
HEAD_NORM_EPS = 1e-5
RWKV_GN_EPS = 64e-5

P_B_OFF = B_COLS
P_COLS = 2 * B_COLS

VMEM_LIMIT = 56 * 1024 * 1024
MXU_TILE = 256
LANES = 128


def _cparams(sem):
    return pltpu.CompilerParams(dimension_semantics=sem, vmem_limit_bytes=VMEM_LIMIT)


def _cdiv(a, b):
    return (a + b - 1) // b


def _softplus(x):
    return jnp.maximum(x, 0.0) + jnp.log(1.0 + jnp.exp(-jnp.abs(x)))


def _sigmoid(x):
    return 1.0 / (1.0 + jnp.exp(-x))


def _mm_kernel(x_ref, w_ref, o_ref):
    o_ref[...] = jnp.dot(x_ref[...].astype(BF16), w_ref[...].astype(BF16),
                         preferred_element_type=F32)


def matmul(x, w, tm, tn):
    m, k = x.shape
    n = w.shape[1]
    assert n % tn == 0
    return pl.pallas_call(
        _mm_kernel,
        out_shape=jax.ShapeDtypeStruct((m, n), F32),
        grid=(n // tn, _cdiv(m, tm)),
        in_specs=[pl.BlockSpec((tm, k), lambda j, i: (i, 0)),
                  pl.BlockSpec((k, tn), lambda j, i: (0, j))],
        out_specs=pl.BlockSpec((tm, tn), lambda j, i: (i, j)),
        compiler_params=_cparams(("arbitrary", "arbitrary")),
        name="matmul",
    )(x, w)


def _layer_norm_rows(y, g, b):
    mu = jnp.mean(y, axis=-1, keepdims=True)
    yc = y - mu
    var = jnp.mean(yc * yc, axis=-1, keepdims=True)
    return yc * lax.rsqrt(var + LN_EPS) * g + b


def _proj_ln_kernel(n_lhs, *refs):
    lhs = refs[:n_lhs]
    w_ref, x_ref, g_ref, b_ref, o_ref, ob_ref = refs[n_lhs:]
    acc = None
    off = 0
    for r in lhs:
        kk = r.shape[1]
        part = jnp.dot(r[...], w_ref[off:off + kk, :], preferred_element_type=F32)
        acc = part if acc is None else acc + part
        off += kk
    y = _layer_norm_rows(ALPHA * x_ref[...] + acc, g_ref[...], b_ref[...])
    o_ref[...] = y
    ob_ref[...] = y.astype(BF16)


def proj_ln(lhs_list, w_bf16, x, g, b, tm):
    m, d = x.shape
    kin = w_bf16.shape[0]
    in_specs = [pl.BlockSpec((tm, l.shape[1]), lambda i: (i, 0)) for l in lhs_list]
    in_specs += [pl.BlockSpec((kin, d), lambda i: (0, 0)),
                 pl.BlockSpec((tm, d), lambda i: (i, 0)),
                 pl.BlockSpec((1, d), lambda i: (0, 0)),
                 pl.BlockSpec((1, d), lambda i: (0, 0))]
    return pl.pallas_call(
        functools.partial(_proj_ln_kernel, len(lhs_list)),
        out_shape=(jax.ShapeDtypeStruct((m, d), F32), jax.ShapeDtypeStruct((m, d), BF16)),
        grid=(_cdiv(m, tm),),
        in_specs=in_specs,
        out_specs=(pl.BlockSpec((tm, d), lambda i: (i, 0)),
                   pl.BlockSpec((tm, d), lambda i: (i, 0))),
        compiler_params=_cparams(("arbitrary",)),
        name="proj_ln",
    )(*lhs_list, w_bf16, x, g.reshape(1, d), b.reshape(1, d))


def _add_ln_kernel(x_ref, f0_ref, f1_ref, g_ref, b_ref, o_ref, ob_ref):
    y = _layer_norm_rows(ALPHA * x_ref[...] + (f0_ref[...] + f1_ref[...]), g_ref[...], b_ref[...])
    o_ref[...] = y
    ob_ref[...] = y.astype(BF16)


def add_ln(x, f0, f1, g, b, tm):
    m, d = x.shape
    row = pl.BlockSpec((tm, d), lambda i: (i, 0))
    vec = pl.BlockSpec((1, d), lambda i: (0, 0))
    return pl.pallas_call(
        _add_ln_kernel,
        out_shape=(jax.ShapeDtypeStruct((m, d), F32), jax.ShapeDtypeStruct((m, d), BF16)),
        grid=(_cdiv(m, tm),),
        in_specs=[row, row, row, vec, vec],
        out_specs=(row, row),
        compiler_params=_cparams(("arbitrary",)),
        name="add_ln",
    )(x, f0, f1, g.reshape(1, d), b.reshape(1, d))


def _gla_kernel(nb, lb, chunk, *refs):
    p_refs = refs[:nb]
    (s0_ref, gup_ref, gb_ref, ng_ref) = refs[nb:nb + 4]
    o_ref = refs[nb + 4]
    sout_ref = refs[nb + 5]
    s_ref = refs[nb + 6]
    blk = pl.program_id(1)

    @pl.when(blk == 0)
    def _():
        s_ref[...] = s0_ref[...]

    rows = lax.broadcasted_iota(jnp.int32, (chunk, chunk), 0)
    cols = lax.broadcasted_iota(jnp.int32, (chunk, chunk), 1)
    tril = rows >= cols
    tril_f = tril.astype(F32)
    lgs = []
    for j in range(nb):
        gd = p_refs[j][:, 2 * A_QK + 2 * A_WIDTH:2 * A_QK + 2 * A_WIDTH + A_GATE_RANK]
        z = jnp.dot(gd, gup_ref[...], preferred_element_type=F32) + gb_ref[...]
        lgs.append(-_softplus(-z) * (1.0 / A_GATE_TAU))
    for c in range(lb // chunk):
        r0 = c * chunk
        for j in range(nb):
            p_ref = p_refs[j]
            bc = jnp.dot(tril_f, lgs[j][r0:r0 + chunk, :], preferred_element_type=F32,
                         precision=lax.Precision.HIGHEST)
            for h in range(A_HEADS):
                q = p_ref[r0:r0 + chunk, h * A_DK:(h + 1) * A_DK] * (A_DK ** -0.5)
                k = p_ref[r0:r0 + chunk, A_QK + h * A_DK:A_QK + (h + 1) * A_DK]
                v = p_ref[r0:r0 + chunk, 2 * A_QK + h * A_DV:2 * A_QK + (h + 1) * A_DV]
                rg = p_ref[r0:r0 + chunk,
                           2 * A_QK + A_WIDTH + h * A_DV:2 * A_QK + A_WIDTH + (h + 1) * A_DV]
                b = bc[:, h * A_DK:(h + 1) * A_DK]
                bl = b[chunk - 1:chunk, :]
                qd = (q * jnp.exp(b)).astype(BF16)
                kd = (k * jnp.exp(-b)).astype(BF16)
                vb = v.astype(BF16)
                att = lax.dot_general(qd, kd, (((1,), (1,)), ((), ())), preferred_element_type=F32)
                att = jnp.where(tril, att, 0.0).astype(BF16)
                s = s_ref[j, h]
                o = (jnp.dot(att, vb, preferred_element_type=F32)
                     + jnp.dot(qd, s.astype(BF16), preferred_element_type=F32))
                kl = (k * jnp.exp(bl - b)).astype(BF16)
                kv = lax.dot_general(kl, vb, (((0,), (0,)), ((), ())), preferred_element_type=F32)
                dec = jnp.broadcast_to(jnp.exp(bl), (A_DK, A_DK)).T
                s_ref[j, h] = s * jnp.concatenate([dec] * (A_DV // A_DK), axis=1) + kv
                o = o * lax.rsqrt(jnp.mean(o * o, axis=-1, keepdims=True) + HEAD_NORM_EPS)
                o = o * ng_ref[:, h * A_DV:(h + 1) * A_DV] * (rg * _sigmoid(rg))
                o_ref[j, r0:r0 + chunk, h * A_DV:(h + 1) * A_DV] = o.astype(BF16)

    @pl.when(blk == pl.num_programs(1) - 1)
    def _():
        sout_ref[...] = s_ref[...]


def gla_group(p, s0, gate_up, gate_b, norm_g, *, row_off, nbatch, seq, nb, lb, chunk):
    t = p.shape[0]
    nblk = seq // lb
    assert seq % lb == 0 and lb % chunk == 0 and nbatch % nb == 0 and row_off % lb == 0
    base = row_off // lb

    def p_map(j):
        return lambda bi, blk: (base + (bi * nb + j) * nblk + blk, 0)

    in_specs = [pl.BlockSpec((lb, P_B_OFF), p_map(j)) for j in range(nb)]
    in_specs += [pl.BlockSpec((nb, A_HEADS, A_DK, A_DV), lambda bi, blk: (bi, 0, 0, 0)),
                 pl.BlockSpec((A_GATE_RANK, A_QK), lambda bi, blk: (0, 0)),
                 pl.BlockSpec((1, A_QK), lambda bi, blk: (0, 0)),
                 pl.BlockSpec((1, A_WIDTH), lambda bi, blk: (0, 0))]

    out_specs = [pl.BlockSpec((nb, lb, A_WIDTH), lambda bi, blk: (bi, blk, 0)),
                 pl.BlockSpec((nb, A_HEADS, A_DK, A_DV), lambda bi, blk: (bi, 0, 0, 0))]
    out_shape = [jax.ShapeDtypeStruct((nbatch, seq, A_WIDTH), BF16),
                 jax.ShapeDtypeStruct((nbatch, A_HEADS, A_DK, A_DV), F32)]
    o, s_new = pl.pallas_call(
        functools.partial(_gla_kernel, nb, lb, chunk),
        out_shape=out_shape,
        grid=(nbatch // nb, nblk),
        in_specs=in_specs,
        out_specs=out_specs,
        scratch_shapes=[pltpu.VMEM((nb, A_HEADS, A_DK, A_DV), F32)],
        compiler_params=_cparams(("arbitrary", "arbitrary")),
        name="gla",
    )(*([p] * nb), s0, gate_up, gate_b.reshape(1, A_QK), norm_g.reshape(1, A_WIDTH))
    return o.reshape(nbatch * seq, A_WIDTH), s_new


VT_SLOT = 128


def _head_ones(dtype):
    r = lax.broadcasted_iota(jnp.int32, (MXU_TILE, MXU_TILE), 0) // B_HEAD
    c = lax.broadcasted_iota(jnp.int32, (MXU_TILE, MXU_TILE), 1) // B_HEAD
    return (r == c).astype(dtype)


def _head_sum(x, ones):
    xh = x.astype(BF16)
    xl = (x - xh.astype(F32)).astype(BF16)
    parts = [jnp.dot(xh[:, c * MXU_TILE:(c + 1) * MXU_TILE], ones, preferred_element_type=F32)
             + jnp.dot(xl[:, c * MXU_TILE:(c + 1) * MXU_TILE], ones, preferred_element_type=F32)
             for c in range(B_WIDTH // MXU_TILE)]
    return jnp.concatenate(parts, axis=-1)


def _rwkv_kernel(nb, lb, *refs):
    p_refs = refs[:nb]
    (shift0_ref, s0_ref, mu_ref, w0_ref, wup_ref, a0_ref, aup_ref, gup_ref,
     kk_ref, ka_ref, rk_ref, lng_ref, lnb_ref) = refs[nb:nb + 13]
    o_ref, sout_ref = refs[nb + 13:nb + 15]
    (s_ref, prev_ref, r_s, w_s, kh_s, kl_s, kk_s, kka_s, on_s, g_s, bon_s, vmh_s, vml_s, vt_s
     ) = refs[nb + 15:]
    blk = pl.program_id(1)
    W = B_WIDTH
    H = B_HEADS
    NT = (((1,), (1,)), ((), ()))

    @pl.when(blk == 0)
    def _():
        s_ref[...] = s0_ref[...]
        prev_ref[...] = shift0_ref[...]

    ones = _head_ones(BF16)
    hmask = (lax.broadcasted_iota(jnp.int32, (H, W), 1) // B_HEAD
             == lax.broadcasted_iota(jnp.int32, (H, W), 0))

    def to_head_rows(ref, j, x):
        ref[j] = jnp.zeros(ref.shape[1:], F32)
        for h in range(H):
            ref[j, pl.ds(h, lb, stride=H), 0:B_HEAD] = x[:, h * B_HEAD:(h + 1) * B_HEAD]

    def split(x):
        xh = x.astype(BF16).astype(F32)
        return xh, x - xh

    for j in range(nb):
        pb = p_refs[j][...]
        first = lax.broadcasted_iota(jnp.int32, (lb, 1), 0) == 0
        prev = jnp.where(first, prev_ref[j], pltpu.roll(pb, 1, axis=0))
        prev_ref[j] = pb[lb - 1:lb, :]
        xm = pb + (prev - pb) * mu_ref[...]
        r = xm[:, :W]
        k = xm[:, W:2 * W]
        v = xm[:, 2 * W:3 * W]
        wd = xm[:, 3 * W:3 * W + B_DECAY_RANK]
        ad = xm[:, 3 * W + B_DECAY_RANK:3 * W + B_DECAY_RANK + B_AAA_RANK]
        gd = xm[:, 3 * W + B_DECAY_RANK + B_AAA_RANK:]
        w = -_softplus(-(w0_ref[...] + jnp.dot(jnp.tanh(wd).astype(BF16), wup_ref[...].astype(BF16),
                                               preferred_element_type=F32))) - 0.5
        a = _sigmoid(a0_ref[...] + jnp.dot(ad.astype(BF16), aup_ref[...].astype(BF16),
                                           preferred_element_type=F32))
        g = jnp.dot(_sigmoid(gd).astype(BF16), gup_ref[...].astype(BF16), preferred_element_type=F32)
        kk = k * kk_ref[...]
        nrm = jnp.sqrt(_head_sum(kk * kk, ones))
        kk = kk / jnp.maximum(nrm, 1e-12)
        k2 = k * (1.0 + (a - 1.0) * ka_ref[...])
        r_s[j] = r
        w_s[j] = jnp.exp(-jnp.exp(w))
        kh_s[j], kl_s[j] = split(k2)
        kk_s[j] = kk
        kka_s[j] = -(kk * a)
        vh, vl = split(v)
        to_head_rows(vmh_s, j, vh)
        to_head_rows(vml_s, j, vl)
        g_s[j] = g
        bon_s[j] = _head_sum(r * k2 * rk_ref[...], ones) * v

    def head_rows(ref, j, t):
        return jnp.where(hmask, jnp.broadcast_to(ref[j, pl.ds(t, 1), :], (H, W)), 0.0).astype(BF16)

    def readout(j, t, sb):
        rows = pl.ds(t * H if isinstance(t, int) else pl.multiple_of(t * H, H), H)
        on_s[j, rows, :] = lax.dot_general(head_rows(r_s, j, t), sb, NT, preferred_element_type=F32)

    zrows = lambda n: jnp.zeros((n * H, W), BF16)
    lane = lax.broadcasted_iota(jnp.int32, (B_HEAD, VT_SLOT), 1)

    def vt_step(t, carry):
        rows = pl.ds(pl.multiple_of(t * H, H), H)
        for j in range(nb):
            vmh, vml = vmh_s[j, rows, :], vml_s[j, rows, :]
            tile = jnp.concatenate([vmh, vml, vmh, jnp.zeros((VT_SLOT - 3 * H, VT_SLOT), F32)], axis=0)
            vt_s[j, t] = tile.T[:B_HEAD, :].astype(BF16)
        return carry

    lax.fori_loop(0, lb, vt_step, 0)

    def step(t, carry):
        sbs, sas = [], []
        for j in range(nb):
            sb = s_ref[j].astype(BF16)
            kkw = head_rows(kk_s, j, t)
            sas.append(lax.dot_general(
                sb, jnp.concatenate([zrows(3), kkw, kkw, zrows(VT_SLOT // H - 5)], axis=0),
                NT, preferred_element_type=F32))
            sbs.append(sb)
        for j in range(nb):
            readout(j, jnp.maximum(t - 1, 0), sbs[j])
        for j in range(nb):
            sa = sas[j]
            sa_hi = sa.astype(BF16)
            sa_lo = (sa - sa_hi.astype(F32)).astype(BF16)
            x = jnp.where(lane < 3 * H, vt_s[j, t], jnp.where(lane < 4 * H, sa_hi, sa_lo))
            khw, klw, kaw = head_rows(kh_s, j, t), head_rows(kl_s, j, t), head_rows(kka_s, j, t)
            upd = jnp.dot(x[:, :5 * H], jnp.concatenate([khw, khw, klw, kaw, kaw], axis=0),
                          preferred_element_type=F32)
            s_ref[j] = s_ref[j] * w_s[j, pl.ds(t, 1), :] + upd
        return carry

    lax.fori_loop(0, lb, step, 0)

    for j in range(nb):
        readout(j, lb - 1, s_ref[j].astype(BF16))
        o = jnp.concatenate([on_s[j, pl.ds(h, lb, stride=H), :] for h in range(H)], axis=-1)
        oc = o - _head_sum(o, ones) * (1.0 / B_HEAD)
        var = _head_sum(oc * oc, ones) * (1.0 / B_HEAD)
        o = oc * lax.rsqrt(var + RWKV_GN_EPS) * lng_ref[...] + lnb_ref[...]
        o_ref[j] = ((o + bon_s[j]) * g_s[j]).astype(BF16)

    @pl.when(blk == pl.num_programs(1) - 1)
    def _():
        sout_ref[...] = s_ref[...]


def rwkv_group(p, s0, shift0, mu, w0, w_up, a0, a_up, g_up, k_k, k_a, r_k, ln_g, ln_b,
               *, row_off, nbatch, seq, nb, lb):
    nblk = seq // lb
    assert seq % lb == 0 and nbatch % nb == 0 and row_off % lb == 0
    base = row_off // lb
    W = B_WIDTH
    s0 = s0.transpose(0, 2, 1, 3).reshape(nbatch, B_HEAD, W)

    def p_map(j):
        return lambda bi, blk: (base + (bi * nb + j) * nblk + blk, 1)

    const = lambda shape: pl.BlockSpec(shape, lambda bi, blk: (0,) * len(shape))
    in_specs = [pl.BlockSpec((lb, B_COLS), p_map(j)) for j in range(nb)]
    in_specs += [pl.BlockSpec((nb, 1, B_COLS), lambda bi, blk: (bi, 0, 0)),
                 pl.BlockSpec((nb, B_HEAD, W), lambda bi, blk: (bi, 0, 0)),
                 const((1, B_COLS)), const((1, W)), const((B_DECAY_RANK, W)), const((1, W)),
                 const((B_AAA_RANK, W)), const((B_GATE_RANK, W)),
                 const((1, W)), const((1, W)), const((1, W)), const((1, W)), const((1, W))]
    out_specs = [pl.BlockSpec((nb, lb, W), lambda bi, blk: (bi, blk, 0)),
                 pl.BlockSpec((nb, B_HEAD, W), lambda bi, blk: (bi, 0, 0))]
    out_shape = [jax.ShapeDtypeStruct((nbatch, seq, W), BF16),
                 jax.ShapeDtypeStruct((nbatch, B_HEAD, W), F32)]
    tok = pltpu.VMEM((nb, lb, W), F32)
    o, s_new = pl.pallas_call(
        functools.partial(_rwkv_kernel, nb, lb),
        out_shape=out_shape,
        grid=(nbatch // nb, nblk),
        in_specs=in_specs,
        out_specs=out_specs,
        scratch_shapes=[pltpu.VMEM((nb, B_HEAD, W), F32), pltpu.VMEM((nb, 1, B_COLS), F32)]
        + [tok] * 6 + [pltpu.VMEM((nb, lb * B_HEADS, B_HEAD), F32), tok, tok,
                       pltpu.VMEM((nb, lb * B_HEADS, VT_SLOT), F32),
                       pltpu.VMEM((nb, lb * B_HEADS, VT_SLOT), F32),
                       pltpu.VMEM((nb, lb, B_HEAD, VT_SLOT), BF16)],
        compiler_params=_cparams(("arbitrary", "arbitrary")),
        name="rwkv",
    )(*([p] * nb), shift0.reshape(nbatch, 1, B_COLS), s0, mu.reshape(1, B_COLS),
      w0.reshape(1, W), w_up, a0.reshape(1, W), a_up, g_up, k_k.reshape(1, W),
      k_a.reshape(1, W), r_k.reshape(1, W), ln_g.reshape(1, W), ln_b.reshape(1, W))
    s_new = s_new.reshape(nbatch, B_HEAD, B_HEADS, B_HEAD).transpose(0, 2, 1, 3)
    return o.reshape(nbatch * seq, W), s_new


S5_GT = LANES // C_GROUP
SP = C_GROUPS * C_STATE


def s5_tables(a_re, a_im, log_dt, b_re, b_im, c_re, c_im, d, chunk):
    G, P, c = C_GROUPS, C_STATE, C_GROUP
    dt = jnp.exp(log_dt)[:, None]
    m = jnp.arange(chunk + 1, dtype=F32)[:, None, None]
    mag = jnp.exp(m * (dt * a_re))
    ang = m * (dt * a_im)
    pw_re, pw_im = mag * jnp.cos(ang), mag * jnp.sin(ang)
    num_re, num_im = pw_re[1] - 1.0, pw_im[1]
    den = a_re * a_re + a_im * a_im
    q_re = (num_re * a_re + num_im * a_im) / den
    q_im = (num_im * a_re - num_re * a_im) / den
    bb_re = q_re[..., None] * b_re - q_im[..., None] * b_im
    bb_im = q_re[..., None] * b_im + q_im[..., None] * b_re
    ca_re = c_re[None] * pw_re[:, :, None, :] - c_im[None] * pw_im[:, :, None, :]
    ca_im = c_re[None] * pw_im[:, :, None, :] + c_im[None] * pw_re[:, :, None, :]
    cm = lambda z: z[:chunk].transpose(1, 3, 0, 2).reshape(G, P, chunk * c)
    toep = s5_toeplitz(bb_re.transpose(0, 2, 1), bb_im.transpose(0, 2, 1), cm(ca_re), cm(ca_im), chunk)
    dvec = jnp.tile(d.reshape(G, 1, c), (1, chunk, 1)).reshape(G, 1, chunk * c)
    rev_re, rev_im = pw_re[:chunk][::-1], pw_im[:chunk][::-1]
    bs_re = rev_re[..., None] * bb_re[None] - rev_im[..., None] * bb_im[None]
    bs_im = rev_re[..., None] * bb_im[None] + rev_im[..., None] * bb_re[None]
    to_in = lambda z: z.transpose(1, 0, 3, 2).reshape(G, chunk * c, P)
    to_out = lambda z: z.transpose(1, 3, 0, 2).reshape(G, P, chunk * c)
    cs_re, cs_im = to_out(ca_re[1:]), to_out(-ca_im[1:])
    return dict(toep=toep, dvec=dvec,
                bs_re=to_in(bs_re).astype(BF16), bs_im=to_in(bs_im).astype(BF16),
                cs_re=cs_re.astype(BF16), cs_im=cs_im.astype(BF16),
                ac_re=pw_re[chunk].reshape(1, SP), ac_im=pw_im[chunk].reshape(1, SP))


def _s5_toeplitz_kernel(chunk, bre_ref, bim_ref, cre_ref, cim_ref, o_ref):
    c = C_GROUP
    cr = chunk * c
    hi = lax.Precision.HIGHEST
    lane = lax.broadcasted_iota(jnp.int32, (c, cr), 1)
    for q in range(S5_GT):
        k0 = (jnp.dot(bre_ref[q], cre_ref[q], preferred_element_type=F32, precision=hi)
              - jnp.dot(bim_ref[q], cim_ref[q], preferred_element_type=F32, precision=hi))
        for j in range(chunk):
            blk = k0 if j == 0 else jnp.where(lane >= j * c, pltpu.roll(k0, j * c, axis=1), 0.0)
            o_ref[q, j * c:(j + 1) * c, :] = blk.astype(BF16)


def s5_toeplitz(bbt_re, bbt_im, cm_re, cm_im, chunk):
    G, P, c = C_GROUPS, C_STATE, C_GROUP
    cr = chunk * c
    grp = lambda shape: pl.BlockSpec((S5_GT,) + shape, lambda g: (g, 0, 0))
    return pl.pallas_call(
        functools.partial(_s5_toeplitz_kernel, chunk),
        out_shape=jax.ShapeDtypeStruct((G, cr, cr), BF16),
        grid=(G // S5_GT,),
        in_specs=[grp((c, P)), grp((c, P)), grp((P, cr)), grp((P, cr))],
        out_specs=grp((cr, cr)),
        compiler_params=_cparams(("arbitrary",)),
        name="s5_toeplitz",
    )(bbt_re, bbt_im, cm_re, cm_im)


def _swap_blocks(xs, blk):
    n = len(xs)
    w = xs[0].shape[-1] // n
    s = n // 2
    while s >= 1:
        nxt = list(xs)
        for i in range(n):
            if i & s == 0:
                j = i + s
                low = (blk & s) == 0
                nxt[i] = jnp.where(low, xs[i], pltpu.roll(xs[j], s * w, axis=1))
                nxt[j] = jnp.where(low, pltpu.roll(xs[i], (n - s) * w, axis=1), xs[j])
        xs = nxt
        s //= 2
    return xs


def _s5_kernel(nbatch, nchunks, chunk, u_ref, toep_ref, bre_ref, bim_ref, cre_ref, cim_ref, dvec_ref,
               acre_ref, acim_ref, s0re_ref, s0im_ref, y_ref, fre_ref, fim_ref,
               uf_s, vre_s, vim_s, sre_s, sim_s):
    c, P, GT = C_GROUP, C_STATE, S5_GT
    R = nbatch * nchunks
    cr = chunk * c
    halves = cr // LANES
    per_half = LANES // c
    blk = lax.broadcasted_iota(jnp.int32, (R, LANES), 1) // c

    for h in range(halves):
        xs = _swap_blocks([u_ref[pl.ds(h * per_half + q, R, stride=chunk), :] for q in range(per_half)], blk)
        for g in range(GT):
            uf_s[g, :, h * LANES:(h + 1) * LANES] = xs[g]

    for g in range(GT):
        ub = uf_s[g].astype(BF16)
        vre_s[:, g * P:(g + 1) * P] = jnp.dot(ub, bre_ref[g], preferred_element_type=F32)
        vim_s[:, g * P:(g + 1) * P] = jnp.dot(ub, bim_ref[g], preferred_element_type=F32)

    ar, ai = acre_ref[...], acim_ref[...]
    if nchunks == 1:
        sre_s[...] = s0re_ref[...]
        sim_s[...] = s0im_ref[...]
        sr, si = s0re_ref[...], s0im_ref[...]
        fre_ref[...] = ar * sr - ai * si + vre_s[...]
        fim_ref[...] = ar * si + ai * sr + vim_s[...]
    else:
        def step(ci, carry):
            sr, si = carry
            vr = jnp.concatenate([vre_s[pl.ds(b * nchunks + ci, 1), :] for b in range(nbatch)], axis=0)
            vi = jnp.concatenate([vim_s[pl.ds(b * nchunks + ci, 1), :] for b in range(nbatch)], axis=0)
            for b in range(nbatch):
                sre_s[pl.ds(b * nchunks + ci, 1), :] = sr[b:b + 1]
                sim_s[pl.ds(b * nchunks + ci, 1), :] = si[b:b + 1]
            return ar * sr - ai * si + vr, ar * si + ai * sr + vi

        sr, si = lax.fori_loop(0, nchunks, step, (s0re_ref[...], s0im_ref[...]))
        fre_ref[...] = sr
        fim_ref[...] = si

    for g in range(GT):
        u = uf_s[g]
        y = (jnp.dot(u.astype(BF16), toep_ref[g], preferred_element_type=F32)
             + jnp.dot(sre_s[:, g * P:(g + 1) * P].astype(BF16), cre_ref[g], preferred_element_type=F32)
             + jnp.dot(sim_s[:, g * P:(g + 1) * P].astype(BF16), cim_ref[g], preferred_element_type=F32))
        uf_s[g] = y + dvec_ref[g] * u
    for h in range(halves):
        ys = _swap_blocks([uf_s[g, :, h * LANES:(h + 1) * LANES] for g in range(GT)], blk)
        for q in range(per_half):
            y_ref[pl.ds(h * per_half + q, R, stride=chunk), :] = ys[q]


def s5_group(u, s0_re, s0_im, tab, *, nbatch, seq, chunk):
    G, P, c, GT = C_GROUPS, C_STATE, C_GROUP, S5_GT
    nchunks = seq // chunk
    assert seq % chunk == 0 and u.shape[0] >= nbatch * seq and u.shape[1] == G * c
    R = nbatch * nchunks
    cr = chunk * c
    rows = nbatch * seq
    grp = lambda shape: pl.BlockSpec((GT,) + shape, lambda i: (i, 0, 0))
    lanes = lambda r, w: pl.BlockSpec((r, w), lambda i: (0, i))
    y, f_re, f_im = pl.pallas_call(
        functools.partial(_s5_kernel, nbatch, nchunks, chunk),
        out_shape=[jax.ShapeDtypeStruct((rows, G * c), F32)] + [jax.ShapeDtypeStruct((nbatch, SP), F32)] * 2,
        grid=(G // GT,),
        in_specs=[lanes(rows, GT * c), grp((cr, cr)), grp((cr, P)), grp((cr, P)), grp((P, cr)), grp((P, cr)),
                  grp((1, cr)), lanes(1, GT * P), lanes(1, GT * P), lanes(nbatch, GT * P), lanes(nbatch, GT * P)],
        out_specs=[lanes(rows, GT * c), lanes(nbatch, GT * P), lanes(nbatch, GT * P)],
        scratch_shapes=[pltpu.VMEM((GT, R, cr), F32)] + [pltpu.VMEM((R, GT * P), F32)] * 4,
        compiler_params=_cparams(("arbitrary",)),
        name="s5",
    )(u, tab['toep'], tab['bs_re'], tab['bs_im'], tab['cs_re'], tab['cs_im'], tab['dvec'],
      tab['ac_re'], tab['ac_im'], s0_re.reshape(nbatch, SP), s0_im.reshape(nbatch, SP))
    return y, f_re.reshape(nbatch, G, P), f_im.reshape(nbatch, G, P)


def _glu_kernel(y_ref, w_ref, b_ref, o_ref):
    z = jax.nn.gelu(y_ref[...])
    acc = jnp.dot(z.astype(BF16), w_ref[...], preferred_element_type=F32) + b_ref[...]
    o_ref[...] = (z * _sigmoid(acc)).astype(BF16)


def gelu_glu(y, w_bf16, b, tm):
    m, k = y.shape
    n = w_bf16.shape[1]
    assert n == k
    return pl.pallas_call(
        _glu_kernel,
        out_shape=jax.ShapeDtypeStruct((m, n), BF16),
        grid=(_cdiv(m, tm),),
        in_specs=[pl.BlockSpec((tm, k), lambda i: (i, 0)),
                  pl.BlockSpec((k, n), lambda i: (0, 0)),
                  pl.BlockSpec((1, n), lambda i: (0, 0))],
        out_specs=pl.BlockSpec((tm, n), lambda i: (i, 0)),
        compiler_params=_cparams(("arbitrary",)),
        name="gelu_glu",
    )(y, w_bf16, b.reshape(1, n))


MOE_TM = 256
MOE_TF = 512
MOE_TN = 1024


def _router_kernel(x_ref, wr_ref, e_ref, g_ref):
    logits = lax.dot_general(wr_ref[...], x_ref[...], (((1,), (1,)), ((), ())),
                             preferred_element_type=F32, precision=lax.Precision.HIGHEST)
    mx = jnp.max(logits, axis=0, keepdims=True)
    ex = jnp.exp(logits - mx)
    probs = ex / jnp.sum(ex, axis=0, keepdims=True)
    neg = jnp.float32(-jnp.inf)
    best = None
    for gi in range(N_EXPERT_GROUPS):
        v = [probs[gi * EXPERTS_PER_GROUP + r:gi * EXPERTS_PER_GROUP + r + 1, :]
             for r in range(EXPERTS_PER_GROUP)]
        m1 = jnp.maximum(jnp.maximum(v[0], v[1]), jnp.maximum(v[2], v[3]))
        i1 = jnp.where(v[0] == m1, 0, jnp.where(v[1] == m1, 1, jnp.where(v[2] == m1, 2, 3)))
        w = [jnp.where(i1 == r, neg, v[r]) for r in range(EXPERTS_PER_GROUP)]
        m2 = jnp.maximum(jnp.maximum(w[0], w[1]), jnp.maximum(w[2], w[3]))
        i2 = jnp.where(w[0] == m2, 0, jnp.where(w[1] == m2, 1, jnp.where(w[2] == m2, 2, 3)))
        score = m1 + m2
        cand = (score, m1, m2, i1 + gi * EXPERTS_PER_GROUP, i2 + gi * EXPERTS_PER_GROUP)
        if best is None:
            best = cand
        else:
            take = cand[0] > best[0]
            best = tuple(jnp.where(take, cn, bs) for cn, bs in zip(cand, best))
    _, m1, m2, e1, e2 = best
    tot = m1 + m2
    e_ref[...] = jnp.concatenate([e1, e2], axis=0)
    g_ref[...] = jnp.concatenate([m1 / tot, m2 / tot], axis=0)


def router(x, w_router, tm):
    t, d = x.shape
    return pl.pallas_call(
        _router_kernel,
        out_shape=[jax.ShapeDtypeStruct((TOP_K, t), jnp.int32), jax.ShapeDtypeStruct((TOP_K, t), F32)],
        grid=(_cdiv(t, tm),),
        in_specs=[pl.BlockSpec((tm, d), lambda i: (i, 0)),
                  pl.BlockSpec((N_EXPERTS, d), lambda i: (0, 0))],
        out_specs=[pl.BlockSpec((TOP_K, tm), lambda i: (0, i)),
                   pl.BlockSpec((TOP_K, tm), lambda i: (0, i))],
        compiler_params=_cparams(("arbitrary",)),
        name="router",
    )(x, w_router.T)


def _moe_up_kernel(st_ref, sf_ref, se_ref, nu_ref, xs_ref, w1_ref, w2_ref, gate_ref, h_ref):
    @pl.when(pl.program_id(0) < nu_ref[0])
    def _():
        x = xs_ref[...].astype(BF16)
        a = jnp.dot(x, w1_ref[0, 0].astype(BF16), preferred_element_type=F32)
        b = jnp.dot(x, w2_ref[0, 0].astype(BF16), preferred_element_type=F32)
        h_ref[...] = (a * _sigmoid(a) * b * gate_ref[...]).astype(BF16)

    @pl.when(pl.program_id(0) >= nu_ref[0])
    def _():
        h_ref[...] = jnp.zeros(h_ref.shape, h_ref.dtype)


def _moe_down_kernel(st_ref, sf_ref, se_ref, nu_ref, h_ref, w_ref, y_ref):
    @pl.when(pl.program_id(0) < nu_ref[0])
    def _():
        y_ref[...] = jnp.dot(h_ref[...], w_ref[0, 0].astype(BF16), preferred_element_type=F32)

    @pl.when(pl.program_id(0) >= nu_ref[0])
    def _():
        y_ref[...] = jnp.zeros(y_ref.shape, y_ref.dtype)


def _moe_schedule(tiles_e, tile_start, nblocks, n_tiles):
    steps_e = tiles_e * nblocks
    cum = jnp.cumsum(steps_e)
    used = cum[-1]
    s = jnp.arange(n_tiles * nblocks, dtype=jnp.int32)
    sc = jnp.minimum(s, used - 1)
    e = jnp.minimum(jnp.sum((cum[None, :] <= sc[:, None]).astype(jnp.int32), axis=1), N_EXPERTS - 1)
    local = sc - (cum[e] - steps_e[e])
    te = jnp.maximum(tiles_e[e], 1)
    rest = s - used
    tile = jnp.where(s < used, tile_start[e] + local % te, jnp.sum(tiles_e) + rest // nblocks)
    blk = jnp.where(s < used, local // te, rest % nblocks)
    return tile.astype(jnp.int32), blk.astype(jnp.int32), e, used.reshape(1).astype(jnp.int32)


def moe_layer(x, w_router, w_up, w_down, layer):
    t, d = x.shape
    tm = MOE_TM
    eid, gates = router(x, w_router, 1024)
    eid = eid.reshape(-1)
    na = TOP_K * t
    n_tiles = _cdiv(na + N_EXPERTS * (tm - 1), tm)
    npad = n_tiles * tm
    onehot = (eid[None, :] == jnp.arange(N_EXPERTS, dtype=jnp.int32)[:, None]).astype(jnp.int32)
    csum = jnp.cumsum(onehot, axis=1)
    counts = csum[:, -1]
    rank = jnp.sum(csum * onehot, axis=0) - 1
    tiles_e = (counts + tm - 1) // tm
    tile_start = jnp.cumsum(tiles_e) - tiles_e
    pos = tile_start[eid] * tm + rank
    row_a = jnp.full((npad,), -1, jnp.int32).at[pos].set(
        jnp.arange(na, dtype=jnp.int32), mode="promise_in_bounds", unique_indices=True)
    a_c = jnp.maximum(row_a, 0)
    row_src = jnp.where(a_c >= t, a_c - t, a_c)
    row_gate = jnp.where(row_a >= 0, gates.reshape(-1).at[a_c].get(mode="promise_in_bounds"), 0.0)
    xs = x.at[row_src].get(mode="promise_in_bounds")

    nf = D_EXPERT // MOE_TF
    st, sf, se, nu = _moe_schedule(tiles_e, tile_start, nf, n_tiles)
    h = pl.pallas_call(
        _moe_up_kernel,
        out_shape=jax.ShapeDtypeStruct((npad, D_EXPERT), BF16),
        grid_spec=pltpu.PrefetchScalarGridSpec(
            num_scalar_prefetch=4,
            grid=(n_tiles * nf,),
            in_specs=[pl.BlockSpec((tm, d), lambda s, st, sf, se, nu: (st[s], 0)),
                      pl.BlockSpec((1, 1, d, MOE_TF), lambda s, st, sf, se, nu: (layer, se[s], 0, sf[s])),
                      pl.BlockSpec((1, 1, d, MOE_TF), lambda s, st, sf, se, nu: (layer, se[s], 0, nf + sf[s])),
                      pl.BlockSpec((tm, 1), lambda s, st, sf, se, nu: (st[s], 0))],
            out_specs=pl.BlockSpec((tm, MOE_TF), lambda s, st, sf, se, nu: (st[s], sf[s]))),
        compiler_params=_cparams(("arbitrary",)),
        name="moe_up",
    )(st, sf, se, nu, xs, w_up, w_up, row_gate.reshape(npad, 1))

    nn = d // MOE_TN
    st, sf, se, nu = _moe_schedule(tiles_e, tile_start, nn, n_tiles)
    ys = pl.pallas_call(
        _moe_down_kernel,
        out_shape=jax.ShapeDtypeStruct((npad, d), F32),
        grid_spec=pltpu.PrefetchScalarGridSpec(
            num_scalar_prefetch=4,
            grid=(n_tiles * nn,),
            in_specs=[pl.BlockSpec((tm, D_EXPERT), lambda s, st, sf, se, nu: (st[s], 0)),
                      pl.BlockSpec((1, 1, D_EXPERT, MOE_TN),
                                   lambda s, st, sf, se, nu: (layer, se[s], 0, sf[s]))],
            out_specs=pl.BlockSpec((tm, MOE_TN), lambda s, st, sf, se, nu: (st[s], sf[s]))),
        compiler_params=_cparams(("arbitrary",)),
        name="moe_down",
    )(st, sf, se, nu, h, w_down)
    return (ys.at[pos[:t]].get(mode="promise_in_bounds", unique_indices=True),
            ys.at[pos[t:]].get(mode="promise_in_bounds", unique_indices=True))


TOK_TM = 512
MM_TM = 1024


def kernel(x_prompt, x_sample, state_gla, state_rwkv, state_shift, state_s5_re, state_s5_im, meta, ev_w_in, ev_w_out, a_gate_up, a_gate_b, a_norm_g, b_mu, b_w0, b_w_up, b_a0, b_a_up, b_g_up, b_k_k, b_k_a, b_r_k, b_ln_g, b_ln_b, od_w_in, c_a_re, c_a_im, c_log_dt, c_b_re, c_b_im, c_c_re, c_c_im, c_d, c_w_glu, c_b_glu, od_w_out, w_router, moe_w_up, moe_w_down, ln_mix_g, ln_mix_b, ln_ffn_g, ln_ffn_b):
    bp, sp, d = x_prompt.shape
    bs, ss, _ = x_sample.shape
    lp = sp + N_META
    tp = bp * lp
    t = tp + bs * ss
    xp = jnp.concatenate([jnp.broadcast_to(meta[None], (bp, N_META, d)), x_prompt], axis=1)
    x = jnp.concatenate([xp.reshape(tp, d), x_sample.reshape(bs * ss, d)], axis=0)
    xb = x.astype(BF16)

    def ffn(x, xb, layer):
        f0, f1 = moe_layer(x, w_router, moe_w_up, moe_w_down, layer)
        return add_ln(x, f0, f1, ln_ffn_g[layer], ln_ffn_b[layer], TOK_TM)

    w_in = ev_w_in[0]
    zpad = jnp.zeros((d, P_B_OFF - A_COLS), w_in.dtype)
    w_in = jnp.concatenate([w_in[:, :A_COLS], zpad, w_in[:, A_COLS:]], axis=1).astype(BF16)
    p = matmul(xb, w_in, MM_TM, P_COLS // 4)

    gla_w = (a_gate_up[0], a_gate_b[0], a_norm_g[0])
    oa_p, gla_p = gla_group(p, jnp.zeros((bp, A_HEADS, A_DK, A_DV), F32), *gla_w,
                            row_off=0, nbatch=bp, seq=lp, nb=4, lb=48, chunk=16)
    oa_s, gla_s = gla_group(p, state_gla[0], *gla_w,
                            row_off=tp, nbatch=bs, seq=ss, nb=4, lb=ss, chunk=ss)
    rw_w = (b_mu[0], b_w0[0], b_w_up[0], b_a0[0], b_a_up[0], b_g_up[0], b_k_k[0], b_k_a[0],
            b_r_k[0], b_ln_g[0], b_ln_b[0])
    ob_p, rwkv_p = rwkv_group(p, jnp.zeros((bp, B_HEADS, B_HEAD, B_HEAD), F32),
                              jnp.zeros((bp, B_COLS), F32), *rw_w,
                              row_off=0, nbatch=bp, seq=lp, nb=4, lb=48)
    ob_s, rwkv_s = rwkv_group(p, state_rwkv[0], state_shift[0], *rw_w,
                              row_off=tp, nbatch=bs, seq=ss, nb=4, lb=ss)
    shift_p = p[lp - 1:tp:lp, P_B_OFF:]
    shift_s = p[tp + ss - 1::ss, P_B_OFF:]
    o_a = jnp.concatenate([oa_p, oa_s], axis=0)
    o_b = jnp.concatenate([ob_p, ob_s], axis=0)
    x, xb = proj_ln([o_a, o_b], ev_w_out[0].astype(BF16), x, ln_mix_g[0], ln_mix_b[0], TOK_TM)
    x, xb = ffn(x, xb, 0)

    u = matmul(xb, od_w_in[0].astype(BF16), MM_TM, d // 2)
    s5_w = (c_a_re[0], c_a_im[0], c_log_dt[0], c_b_re[0], c_b_im[0], c_c_re[0], c_c_im[0], c_d[0])
    zero_state = jnp.zeros((bp, C_GROUPS, C_STATE), F32)
    y_p, re_p, im_p = s5_group(u, zero_state, zero_state, s5_tables(*s5_w, 16),
                               nbatch=bp, seq=lp, chunk=16)
    y_s, re_s, im_s = s5_group(u[tp:], state_s5_re[0], state_s5_im[0], s5_tables(*s5_w, ss),
                               nbatch=bs, seq=ss, chunk=ss)
    y = jnp.concatenate([y_p, y_s], axis=0)
    z = gelu_glu(y, c_w_glu[0].astype(BF16), c_b_glu[0], TOK_TM)
    x, xb = proj_ln([z], od_w_out[0].astype(BF16), x, ln_mix_g[1], ln_mix_b[1], TOK_TM)
    x, xb = ffn(x, xb, 1)

    y_prompt = x[:tp].reshape(bp, lp, d)[:, N_META:]
    y_sample = x[tp:].reshape(bs, ss, d)
    return (y_prompt, y_sample, gla_p[None], gla_s[None], rwkv_p[None], rwkv_s[None],
            shift_p[None], shift_s[None], re_p[None], re_s[None], im_p[None], im_s[None])
```

```python
import functools
import math

import jax
import jax.numpy as jnp
from jax import lax
from jax.experimental import pallas as pl
from jax.experimental.pallas import tpu as pltpu

F32 = jnp.float32
BF16 = jnp.bfloat16

D_MODEL = 2048
DEPTH = 2
N_META = 16

A_WIDTH = 1024
A_HEADS = 4
A_DV = 256
A_DK = 128
A_GATE_RANK = 16
A_GATE_TAU = 16.0
A_QK = A_HEADS * A_DK
A_COLS = 2 * A_QK + 2 * A_WIDTH + A_GATE_RANK

B_WIDTH = 1024
B_HEAD = 64
B_HEADS = 16
B_DECAY_RANK = 64
B_AAA_RANK = 64
B_GATE_RANK = 128
B_COLS = 3 * B_WIDTH + B_DECAY_RANK + B_AAA_RANK + B_GATE_RANK

C_GROUP = 16
C_GROUPS = 128
C_STATE = 64

N_EXPERTS = 16
N_EXPERT_GROUPS = 4
EXPERTS_PER_GROUP = 4
TOP_K = 2
D_EXPERT = 1024

ALPHA = (2.0 * DEPTH) ** 0.25
LN_EPS = 1e-5
HEAD_NORM_EPS = 1e-5
RWKV_GN_EPS = 64e-5

P_B_OFF = B_COLS
P_COLS = 2 * B_COLS

VMEM_LIMIT = 56 * 1024 * 1024
MXU_TILE = 256
LANES = 128


def _cparams(sem):
    return pltpu.CompilerParams(dimension_semantics=sem, vmem_limit_bytes=VMEM_LIMIT)


def _cdiv(a, b):
    return (a + b - 1) // b


def _softplus(x):
    return jnp.maximum(x, 0.0) + jnp.log(1.0 + jnp.exp(-jnp.abs(x)))


def _sigmoid(x):
    return 1.0 / (1.0 + jnp.exp(-x))


def _mm_kernel(x_ref, w_ref, o_ref):
    o_ref[...] = jnp.dot(x_ref[...].astype(BF16), w_ref[...].astype(BF16),
                         preferred_element_type=F32)


def matmul(x, w, tm, tn):
    m, k = x.shape
    n = w.shape[1]
    assert n % tn == 0
    return pl.pallas_call(
        _mm_kernel,
        out_shape=jax.ShapeDtypeStruct((m, n), F32),
        grid=(n // tn, _cdiv(m, tm)),
        in_specs=[pl.BlockSpec((tm, k), lambda j, i: (i, 0)),
                  pl.BlockSpec((k, tn), lambda j, i: (0, j))],
        out_specs=pl.BlockSpec((tm, tn), lambda j, i: (i, j)),
        compiler_params=_cparams(("arbitrary", "arbitrary")),
        name="matmul",
    )(x, w)


def _layer_norm_rows(y, g, b):
    mu = jnp.mean(y, axis=-1, keepdims=True)
    yc = y - mu
    var = jnp.mean(yc * yc, axis=-1, keepdims=True)
    return yc * lax.rsqrt(var + LN_EPS) * g + b


def _proj_ln_kernel(n_lhs, *refs):
    lhs = refs[:n_lhs]
    w_ref, x_ref, g_ref, b_ref, o_ref, ob_ref = refs[n_lhs:]
    acc = None
    off = 0
    for r in lhs:
        kk = r.shape[1]
        part = jnp.dot(r[...], w_ref[off:off + kk, :], preferred_element_type=F32)
        acc = part if acc is None else acc + part
        off += kk
    y = _layer_norm_rows(ALPHA * x_ref[...] + acc, g_ref[...], b_ref[...])
    o_ref[...] = y
    ob_ref[...] = y.astype(BF16)


def proj_ln(lhs_list, w_bf16, x, g, b, tm):
    m, d = x.shape
    kin = w_bf16.shape[0]
    in_specs = [pl.BlockSpec((tm, l.shape[1]), lambda i: (i, 0)) for l in lhs_list]
    in_specs += [pl.BlockSpec((kin, d), lambda i: (0, 0)),
                 pl.BlockSpec((tm, d), lambda i: (i, 0)),
                 pl.BlockSpec((1, d), lambda i: (0, 0)),
                 pl.BlockSpec((1, d), lambda i: (0, 0))]
    return pl.pallas_call(
        functools.partial(_proj_ln_kernel, len(lhs_list)),
        out_shape=(jax.ShapeDtypeStruct((m, d), F32), jax.ShapeDtypeStruct((m, d), BF16)),
        grid=(_cdiv(m, tm),),
        in_specs=in_specs,
        out_specs=(pl.BlockSpec((tm, d), lambda i: (i, 0)),
                   pl.BlockSpec((tm, d), lambda i: (i, 0))),
        compiler_params=_cparams(("arbitrary",)),
        name="proj_ln",
    )(*lhs_list, w_bf16, x, g.reshape(1, d), b.reshape(1, d))


def _add_ln_kernel(x_ref, f0_ref, f1_ref, g_ref, b_ref, o_ref, ob_ref):
    y = _layer_norm_rows(ALPHA * x_ref[...] + (f0_ref[...] + f1_ref[...]), g_ref[...], b_ref[...])
    o_ref[...] = y
    ob_ref[...] = y.astype(BF16)


def add_ln(x, f0, f1, g, b, tm):
    m, d = x.shape
    row = pl.BlockSpec((tm, d), lambda i: (i, 0))
    vec = pl.BlockSpec((1, d), lambda i: (0, 0))
    return pl.pallas_call(
        _add_ln_kernel,
        out_shape=(jax.ShapeDtypeStruct((m, d), F32), jax.ShapeDtypeStruct((m, d), BF16)),
        grid=(_cdiv(m, tm),),
        in_specs=[row, row, row, vec, vec],
        out_specs=(row, row),
        compiler_params=_cparams(("arbitrary",)),
        name="add_ln",
    )(x, f0, f1, g.reshape(1, d), b.reshape(1, d))


def _gla_kernel(nb, lb, chunk, *refs):
    p_refs = refs[:nb]
    (s0_ref, gup_ref, gb_ref, ng_ref) = refs[nb:nb + 4]
    o_ref = refs[nb + 4]
    sout_ref = refs[nb + 5]
    s_ref = refs[nb + 6]
    blk = pl.program_id(1)

    @pl.when(blk == 0)
    def _():
        s_ref[...] = s0_ref[...]

    rows = lax.broadcasted_iota(jnp.int32, (chunk, chunk), 0)
    cols = lax.broadcasted_iota(jnp.int32, (chunk, chunk), 1)
    tril = rows >= cols
    tril_f = tril.astype(F32)
    lgs = []
    for j in range(nb):
        gd = p_refs[j][:, 2 * A_QK + 2 * A_WIDTH:2 * A_QK + 2 * A_WIDTH + A_GATE_RANK]
        z = jnp.dot(gd, gup_ref[...], preferred_element_type=F32) + gb_ref[...]
        lgs.append(-_softplus(-z) * (1.0 / A_GATE_TAU))
    for c in range(lb // chunk):
        r0 = c * chunk
        for j in range(nb):
            p_ref = p_refs[j]
            bc = jnp.dot(tril_f, lgs[j][r0:r0 + chunk, :], preferred_element_type=F32,
                         precision=lax.Precision.HIGHEST)
            for h in range(A_HEADS):
                q = p_ref[r0:r0 + chunk, h * A_DK:(h + 1) * A_DK] * (A_DK ** -0.5)
                k = p_ref[r0:r0 + chunk, A_QK + h * A_DK:A_QK + (h + 1) * A_DK]
                v = p_ref[r0:r0 + chunk, 2 * A_QK + h * A_DV:2 * A_QK + (h + 1) * A_DV]
                rg = p_ref[r0:r0 + chunk,
                           2 * A_QK + A_WIDTH + h * A_DV:2 * A_QK + A_WIDTH + (h + 1) * A_DV]
                b = bc[:, h * A_DK:(h + 1) * A_DK]
                bl = b[chunk - 1:chunk, :]
                qd = (q * jnp.exp(b)).astype(BF16)
                kd = (k * jnp.exp(-b)).astype(BF16)
                vb = v.astype(BF16)
                att = lax.dot_general(qd, kd, (((1,), (1,)), ((), ())), preferred_element_type=F32)
                att = jnp.where(tril, att, 0.0).astype(BF16)
                s = s_ref[j, h]
                o = (jnp.dot(att, vb, preferred_element_type=F32)
                     + jnp.dot(qd, s.astype(BF16), preferred_element_type=F32))
                kl = (k * jnp.exp(bl - b)).astype(BF16)
                kv = lax.dot_general(kl, vb, (((0,), (0,)), ((), ())), preferred_element_type=F32)
                dec = jnp.broadcast_to(jnp.exp(bl), (A_DK, A_DK)).T
                s_ref[j, h] = s * jnp.concatenate([dec] * (A_DV // A_DK), axis=1) + kv
                o = o * lax.rsqrt(jnp.mean(o * o, axis=-1, keepdims=True) + HEAD_NORM_EPS)
                o = o * ng_ref[:, h * A_DV:(h + 1) * A_DV] * (rg * _sigmoid(rg))
                o_ref[j, r0:r0 + chunk, h * A_DV:(h + 1) * A_DV] = o.astype(BF16)

    @pl.when(blk == pl.num_programs(1) - 1)
    def _():
        sout_ref[...] = s_ref[...]


def gla_group(p, s0, gate_up, gate_b, norm_g, *, row_off, nbatch, seq, nb, lb, chunk):
    t = p.shape[0]
    nblk = seq // lb
    assert seq % lb == 0 and lb % chunk == 0 and nbatch % nb == 0 and row_off % lb == 0
    base = row_off // lb

    def p_map(j):
        return lambda bi, blk: (base + (bi * nb + j) * nblk + blk, 0)

    in_specs = [pl.BlockSpec((lb, P_B_OFF), p_map(j)) for j in range(nb)]
    in_specs += [pl.BlockSpec((nb, A_HEADS, A_DK, A_DV), lambda bi, blk: (bi, 0, 0, 0)),
                 pl.BlockSpec((A_GATE_RANK, A_QK), lambda bi, blk: (0, 0)),
                 pl.BlockSpec((1, A_QK), lambda bi, blk: (0, 0)),
                 pl.BlockSpec((1, A_WIDTH), lambda bi, blk: (0, 0))]

    out_specs = [pl.BlockSpec((nb, lb, A_WIDTH), lambda bi, blk: (bi, blk, 0)),
                 pl.BlockSpec((nb, A_HEADS, A_DK, A_DV), lambda bi, blk: (bi, 0, 0, 0))]
    out_shape = [jax.ShapeDtypeStruct((nbatch, seq, A_WIDTH), BF16),
                 jax.ShapeDtypeStruct((nbatch, A_HEADS, A_DK, A_DV), F32)]
    o, s_new = pl.pallas_call(
        functools.partial(_gla_kernel, nb, lb, chunk),
        out_shape=out_shape,
        grid=(nbatch // nb, nblk),
        in_specs=in_specs,
        out_specs=out_specs,
        scratch_shapes=[pltpu.VMEM((nb, A_HEADS, A_DK, A_DV), F32)],
        compiler_params=_cparams(("arbitrary", "arbitrary")),
        name="gla",
    )(*([p] * nb), s0, gate_up, gate_b.reshape(1, A_QK), norm_g.reshape(1, A_WIDTH))
    return o.reshape(nbatch * seq, A_WIDTH), s_new


VT_SLOT = 128
RWKV_UNROLL = 8


def _head_ones(dtype):
    r = lax.broadcasted_iota(jnp.int32, (MXU_TILE, MXU_TILE), 0) // B_HEAD
    c = lax.broadcasted_iota(jnp.int32, (MXU_TILE, MXU_TILE), 1) // B_HEAD
    return (r == c).astype(dtype)


def _head_sum(x, ones):
    xh = x.astype(BF16)
    xl = (x - xh.astype(F32)).astype(BF16)
    parts = [jnp.dot(xh[:, c * MXU_TILE:(c + 1) * MXU_TILE], ones, preferred_element_type=F32)
             + jnp.dot(xl[:, c * MXU_TILE:(c + 1) * MXU_TILE], ones, preferred_element_type=F32)
             for c in range(B_WIDTH // MXU_TILE)]
    return jnp.concatenate(parts, axis=-1)


def _rwkv_kernel(nb, lb, *refs):
    p_refs = refs[:nb]
    (shift0_ref, s0_ref, mu_ref, w0_ref, wup_ref, a0_ref, aup_ref, gup_ref,
     kk_ref, ka_ref, rk_ref, lng_ref, lnb_ref) = refs[nb:nb + 13]
    o_ref, sout_ref = refs[nb + 13:nb + 15]
    (s_ref, prev_ref, r_s, w_s, kh_s, kl_s, kk_s, kka_s, on_s, g_s, bon_s, vmh_s, vml_s, vt_s
     ) = refs[nb + 15:]
    blk = pl.program_id(1)
    W = B_WIDTH
    H = B_HEADS
    NT = (((1,), (1,)), ((), ()))

    @pl.when(blk == 0)
    def _():
        s_ref[...] = s0_ref[...]
        prev_ref[...] = shift0_ref[...]

    ones = _head_ones(BF16)
    hmask = (lax.broadcasted_iota(jnp.int32, (H, W), 1) // B_HEAD
             == lax.broadcasted_iota(jnp.int32, (H, W), 0))

    def to_head_rows(ref, j, x):
        ref[j] = jnp.zeros(ref.shape[1:], F32)
        for h in range(H):
            ref[j, pl.ds(h, lb, stride=H), 0:B_HEAD] = x[:, h * B_HEAD:(h + 1) * B_HEAD]

    def split(x):
        xh = x.astype(BF16).astype(F32)
        return xh, x - xh

    xms = []
    for j in range(nb):
        pb = p_refs[j][...]
        first = lax.broadcasted_iota(jnp.int32, (lb, 1), 0) == 0
        prev = jnp.where(first, prev_ref[j], pltpu.roll(pb, 1, axis=0))
        prev_ref[j] = pb[lb - 1:lb, :]
        xms.append(pb + (prev - pb) * mu_ref[...])
    xm = jnp.concatenate(xms, axis=0)
    r = xm[:, :W]
    k = xm[:, W:2 * W]
    v = xm[:, 2 * W:3 * W]
    wd = xm[:, 3 * W:3 * W + B_DECAY_RANK]
    ad = xm[:, 3 * W + B_DECAY_RANK:3 * W + B_DECAY_RANK + B_AAA_RANK]
    gd = xm[:, 3 * W + B_DECAY_RANK + B_AAA_RANK:]
    w = -_softplus(-(w0_ref[...] + jnp.dot(jnp.tanh(wd).astype(BF16), wup_ref[...].astype(BF16),
                                           preferred_element_type=F32))) - 0.5
    a = _sigmoid(a0_ref[...] + jnp.dot(ad.astype(BF16), aup_ref[...].astype(BF16),
                                       preferred_element_type=F32))
    g = jnp.dot(_sigmoid(gd).astype(BF16), gup_ref[...].astype(BF16), preferred_element_type=F32)
    kk = k * kk_ref[...]
    nrm = jnp.sqrt(_head_sum(kk * kk, ones))
    kk = kk / jnp.maximum(nrm, 1e-12)
    k2 = k * (1.0 + (a - 1.0) * ka_ref[...])
    per_seq = lambda x: x.reshape(nb, lb, x.shape[-1])
    r_s[...] = per_seq(r)
    w_s[...] = per_seq(jnp.exp(-jnp.exp(w)))
    kh, kl = split(k2)
    kh_s[...], kl_s[...] = per_seq(kh), per_seq(kl)
    kk_s[...] = per_seq(kk)
    kka_s[...] = per_seq(-(kk * a))
    vh, vl = split(v)
    for j in range(nb):
        to_head_rows(vmh_s, j, vh[j * lb:(j + 1) * lb])
        to_head_rows(vml_s, j, vl[j * lb:(j + 1) * lb])
    g_s[...] = per_seq(g)
    bon_s[...] = per_seq(_head_sum(r * k2 * rk_ref[...], ones) * v)

    def head_rows(ref, j, t):
        return jnp.where(hmask, jnp.broadcast_to(ref[j, pl.ds(t, 1), :], (H, W)), 0.0).astype(BF16)

    def readout(j, t, sb):
        rows = pl.ds(t * H if isinstance(t, int) else pl.multiple_of(t * H, H), H)
        on_s[j, rows, :] = lax.dot_general(head_rows(r_s, j, t), sb, NT, preferred_element_type=F32)

    zrows = lambda n: jnp.zeros((n * H, W), BF16)
    lane = lax.broadcasted_iota(jnp.int32, (B_HEAD, VT_SLOT), 1)

    def vt_step(t, carry):
        rows = pl.ds(pl.multiple_of(t * H, H), H)
        for j in range(nb):
            vmh, vml = vmh_s[j, rows, :], vml_s[j, rows, :]
            tile = jnp.concatenate([vmh, vml, vmh, jnp.zeros((VT_SLOT - 3 * H, VT_SLOT), F32)], axis=0)
            vt_s[j, t] = tile.T[:B_HEAD, :].astype(BF16)
        return carry

    lax.fori_loop(0, lb, vt_step, 0)

    def step(t, carry):
        sbs, sas = [], []
        for j in range(nb):
            sb = s_ref[j].astype(BF16)
            kkw = head_rows(kk_s, j, t)
            sas.append(lax.dot_general(
                sb, jnp.concatenate([zrows(3), kkw, kkw, zrows(VT_SLOT // H - 5)], axis=0),
                NT, preferred_element_type=F32))
            sbs.append(sb)
        for j in range(nb):
            readout(j, jnp.maximum(t - 1, 0), sbs[j])
        for j in range(nb):
            sa = sas[j]
            sa_hi = sa.astype(BF16)
            sa_lo = (sa - sa_hi.astype(F32)).astype(BF16)
            x = jnp.where(lane < 3 * H, vt_s[j, t], jnp.where(lane < 4 * H, sa_hi, sa_lo))
            khw, klw, kaw = head_rows(kh_s, j, t), head_rows(kl_s, j, t), head_rows(kka_s, j, t)
            upd = jnp.dot(x[:, :5 * H], jnp.concatenate([khw, khw, klw, kaw, kaw], axis=0),
                          preferred_element_type=F32)
            s_ref[j] = s_ref[j] * w_s[j, pl.ds(t, 1), :] + upd
        return carry

    lax.fori_loop(0, lb, step, 0, unroll=RWKV_UNROLL)

    os = []
    for j in range(nb):
        readout(j, lb - 1, s_ref[j].astype(BF16))
        os.append(jnp.concatenate([on_s[j, pl.ds(h, lb, stride=H), :] for h in range(H)], axis=-1))
    o = jnp.concatenate(os, axis=0)
    oc = o - _head_sum(o, ones) * (1.0 / B_HEAD)
    var = _head_sum(oc * oc, ones) * (1.0 / B_HEAD)
    o = per_seq(oc * lax.rsqrt(var + RWKV_GN_EPS) * lng_ref[...] + lnb_ref[...])
    o_ref[...] = ((o + bon_s[...]) * g_s[...]).astype(BF16)

    @pl.when(blk == pl.num_programs(1) - 1)
    def _():
        sout_ref[...] = s_ref[...]


def rwkv_group(p, s0, shift0, mu, w0, w_up, a0, a_up, g_up, k_k, k_a, r_k, ln_g, ln_b,
               *, row_off, nbatch, seq, nb, lb):
    nblk = seq // lb
    assert seq % lb == 0 and nbatch % nb == 0 and row_off % lb == 0
    base = row_off // lb
    W = B_WIDTH
    s0 = s0.transpose(0, 2, 1, 3).reshape(nbatch, B_HEAD, W)

    def p_map(j):
        return lambda bi, blk: (base + (bi * nb + j) * nblk + blk, 1)

    const = lambda shape: pl.BlockSpec(shape, lambda bi, blk: (0,) * len(shape))
    in_specs = [pl.BlockSpec((lb, B_COLS), p_map(j)) for j in range(nb)]
    in_specs += [pl.BlockSpec((nb, 1, B_COLS), lambda bi, blk: (bi, 0, 0)),
                 pl.BlockSpec((nb, B_HEAD, W), lambda bi, blk: (bi, 0, 0)),
                 const((1, B_COLS)), const((1, W)), const((B_DECAY_RANK, W)), const((1, W)),
                 const((B_AAA_RANK, W)), const((B_GATE_RANK, W)),
                 const((1, W)), const((1, W)), const((1, W)), const((1, W)), const((1, W))]
    out_specs = [pl.BlockSpec((nb, lb, W), lambda bi, blk: (bi, blk, 0)),
                 pl.BlockSpec((nb, B_HEAD, W), lambda bi, blk: (bi, 0, 0))]
    out_shape = [jax.ShapeDtypeStruct((nbatch, seq, W), BF16),
                 jax.ShapeDtypeStruct((nbatch, B_HEAD, W), F32)]
    tok = pltpu.VMEM((nb, lb, W), F32)
    o, s_new = pl.pallas_call(
        functools.partial(_rwkv_kernel, nb, lb),
        out_shape=out_shape,
        grid=(nbatch // nb, nblk),
        in_specs=in_specs,
        out_specs=out_specs,
        scratch_shapes=[pltpu.VMEM((nb, B_HEAD, W), F32), pltpu.VMEM((nb, 1, B_COLS), F32)]
        + [tok] * 6 + [pltpu.VMEM((nb, lb * B_HEADS, B_HEAD), F32), tok, tok,
                       pltpu.VMEM((nb, lb * B_HEADS, VT_SLOT), F32),
                       pltpu.VMEM((nb, lb * B_HEADS, VT_SLOT), F32),
                       pltpu.VMEM((nb, lb, B_HEAD, VT_SLOT), BF16)],
        compiler_params=_cparams(("arbitrary", "arbitrary")),
        name="rwkv",
    )(*([p] * nb), shift0.reshape(nbatch, 1, B_COLS), s0, mu.reshape(1, B_COLS),
      w0.reshape(1, W), w_up, a0.reshape(1, W), a_up, g_up, k_k.reshape(1, W),
      k_a.reshape(1, W), r_k.reshape(1, W), ln_g.reshape(1, W), ln_b.reshape(1, W))
    s_new = s_new.reshape(nbatch, B_HEAD, B_HEADS, B_HEAD).transpose(0, 2, 1, 3)
    return o.reshape(nbatch * seq, W), s_new


S5_GT = LANES // C_GROUP
SP = C_GROUPS * C_STATE


def s5_tables(a_re, a_im, log_dt, b_re, b_im, c_re, c_im, d, chunk):
    G, P, c = C_GROUPS, C_STATE, C_GROUP
    dt = jnp.exp(log_dt)[:, None]
    m = jnp.arange(chunk + 1, dtype=F32)[:, None, None]
    mag = jnp.exp(m * (dt * a_re))
    ang = m * (dt * a_im)
    pw_re, pw_im = mag * jnp.cos(ang), mag * jnp.sin(ang)
    num_re, num_im = pw_re[1] - 1.0, pw_im[1]
    den = a_re * a_re + a_im * a_im
    q_re = (num_re * a_re + num_im * a_im) / den
    q_im = (num_im * a_re - num_re * a_im) / den
    bb_re = q_re[..., None] * b_re - q_im[..., None] * b_im
    bb_im = q_re[..., None] * b_im + q_im[..., None] * b_re
    ca_re = c_re[None] * pw_re[:, :, None, :] - c_im[None] * pw_im[:, :, None, :]
    ca_im = c_re[None] * pw_im[:, :, None, :] + c_im[None] * pw_re[:, :, None, :]
    cm = lambda z: z[:chunk].transpose(1, 3, 0, 2).reshape(G, P, chunk * c)
    toep = s5_toeplitz(bb_re.transpose(0, 2, 1), bb_im.transpose(0, 2, 1), cm(ca_re), cm(ca_im), chunk)
    dvec = jnp.tile(d.reshape(G, 1, c), (1, chunk, 1)).reshape(G, 1, chunk * c)
    rev_re, rev_im = pw_re[:chunk][::-1], pw_im[:chunk][::-1]
    bs_re = rev_re[..., None] * bb_re[None] - rev_im[..., None] * bb_im[None]
    bs_im = rev_re[..., None] * bb_im[None] + rev_im[..., None] * bb_re[None]
    to_in = lambda z: z.transpose(1, 0, 3, 2).reshape(G, chunk * c, P)
    to_out = lambda z: z.transpose(1, 3, 0, 2).reshape(G, P, chunk * c)
    cs_re, cs_im = to_out(ca_re[1:]), to_out(-ca_im[1:])
    return dict(toep=toep, dvec=dvec,
                bs_re=to_in(bs_re).astype(BF16), bs_im=to_in(bs_im).astype(BF16),
                cs_re=cs_re.astype(BF16), cs_im=cs_im.astype(BF16),
                ac_re=pw_re[chunk].reshape(1, SP), ac_im=pw_im[chunk].reshape(1, SP))


def _s5_toeplitz_kernel(chunk, bre_ref, bim_ref, cre_ref, cim_ref, o_ref):
    c = C_GROUP
    cr = chunk * c
    hi = lax.Precision.HIGHEST
    lane = lax.broadcasted_iota(jnp.int32, (c, cr), 1)
    for q in range(S5_GT):
        k0 = (jnp.dot(bre_ref[q], cre_ref[q], preferred_element_type=F32, precision=hi)
              - jnp.dot(bim_ref[q], cim_ref[q], preferred_element_type=F32, precision=hi))
        for j in range(chunk):
            blk = k0 if j == 0 else jnp.where(lane >= j * c, pltpu.roll(k0, j * c, axis=1), 0.0)
            o_ref[q, j * c:(j + 1) * c, :] = blk.astype(BF16)


def s5_toeplitz(bbt_re, bbt_im, cm_re, cm_im, chunk):
    G, P, c = C_GROUPS, C_STATE, C_GROUP
    cr = chunk * c
    grp = lambda shape: pl.BlockSpec((S5_GT,) + shape, lambda g: (g, 0, 0))
    return pl.pallas_call(
        functools.partial(_s5_toeplitz_kernel, chunk),
        out_shape=jax.ShapeDtypeStruct((G, cr, cr), BF16),
        grid=(G // S5_GT,),
        in_specs=[grp((c, P)), grp((c, P)), grp((P, cr)), grp((P, cr))],
        out_specs=grp((cr, cr)),
        compiler_params=_cparams(("arbitrary",)),
        name="s5_toeplitz",
    )(bbt_re, bbt_im, cm_re, cm_im)


def _swap_blocks(xs, blk):
    n = len(xs)
    w = xs[0].shape[-1] // n
    s = n // 2
    while s >= 1:
        nxt = list(xs)
        for i in range(n):
            if i & s == 0:
                j = i + s
                low = (blk & s) == 0
                nxt[i] = jnp.where(low, xs[i], pltpu.roll(xs[j], s * w, axis=1))
                nxt[j] = jnp.where(low, pltpu.roll(xs[i], (n - s) * w, axis=1), xs[j])
        xs = nxt
        s //= 2
    return xs


def _s5_kernel(nbatch, nchunks, chunk, u_ref, toep_ref, bre_ref, bim_ref, cre_ref, cim_ref, dvec_ref,
               acre_ref, acim_ref, s0re_ref, s0im_ref, y_ref, fre_ref, fim_ref,
               uf_s, vre_s, vim_s, sre_s, sim_s):
    c, P, GT = C_GROUP, C_STATE, S5_GT
    R = nbatch * nchunks
    cr = chunk * c
    halves = cr // LANES
    per_half = LANES // c
    blk = lax.broadcasted_iota(jnp.int32, (R, LANES), 1) // c

    for h in range(halves):
        xs = _swap_blocks([u_ref[pl.ds(h * per_half + q, R, stride=chunk), :] for q in range(per_half)], blk)
        for g in range(GT):
            uf_s[g, :, h * LANES:(h + 1) * LANES] = xs[g]

    for g in range(GT):
        ub = uf_s[g].astype(BF16)
        vre_s[:, g * P:(g + 1) * P] = jnp.dot(ub, bre_ref[g], preferred_element_type=F32)
        vim_s[:, g * P:(g + 1) * P] = jnp.dot(ub, bim_ref[g], preferred_element_type=F32)

    ar, ai = acre_ref[...], acim_ref[...]
    if nchunks == 1:
        sre_s[...] = s0re_ref[...]
        sim_s[...] = s0im_ref[...]
        sr, si = s0re_ref[...], s0im_ref[...]
        fre_ref[...] = ar * sr - ai * si + vre_s[...]
        fim_ref[...] = ar * si + ai * sr + vim_s[...]
    else:
        def step(ci, carry):
            sr, si = carry
            vr = jnp.concatenate([vre_s[pl.ds(b * nchunks + ci, 1), :] for b in range(nbatch)], axis=0)
            vi = jnp.concatenate([vim_s[pl.ds(b * nchunks + ci, 1), :] for b in range(nbatch)], axis=0)
            for b in range(nbatch):
                sre_s[pl.ds(b * nchunks + ci, 1), :] = sr[b:b + 1]
                sim_s[pl.ds(b * nchunks + ci, 1), :] = si[b:b + 1]
            return ar * sr - ai * si + vr, ar * si + ai * sr + vi

        sr, si = lax.fori_loop(0, nchunks, step, (s0re_ref[...], s0im_ref[...]))
        fre_ref[...] = sr
        fim_ref[...] = si

    for g in range(GT):
        u = uf_s[g]
        y = (jnp.dot(u.astype(BF16), toep_ref[g], preferred_element_type=F32)
             + jnp.dot(sre_s[:, g * P:(g + 1) * P].astype(BF16), cre_ref[g], preferred_element_type=F32)
             + jnp.dot(sim_s[:, g * P:(g + 1) * P].astype(BF16), cim_ref[g], preferred_element_type=F32))
        uf_s[g] = y + dvec_ref[g] * u
    for h in range(halves):
        ys = _swap_blocks([uf_s[g, :, h * LANES:(h + 1) * LANES] for g in range(GT)], blk)
        for q in range(per_half):
            y_ref[pl.ds(h * per_half + q, R, stride=chunk), :] = ys[q]


def s5_group(u, s0_re, s0_im, tab, *, nbatch, seq, chunk):
    G, P, c, GT = C_GROUPS, C_STATE, C_GROUP, S5_GT
    nchunks = seq // chunk
    assert seq % chunk == 0 and u.shape[0] >= nbatch * seq and u.shape[1] == G * c
    R = nbatch * nchunks
    cr = chunk * c
    rows = nbatch * seq
    grp = lambda shape: pl.BlockSpec((GT,) + shape, lambda i: (i, 0, 0))
    lanes = lambda r, w: pl.BlockSpec((r, w), lambda i: (0, i))
    y, f_re, f_im = pl.pallas_call(
        functools.partial(_s5_kernel, nbatch, nchunks, chunk),
        out_shape=[jax.ShapeDtypeStruct((rows, G * c), F32)] + [jax.ShapeDtypeStruct((nbatch, SP), F32)] * 2,
        grid=(G // GT,),
        in_specs=[lanes(rows, GT * c), grp((cr, cr)), grp((cr, P)), grp((cr, P)), grp((P, cr)), grp((P, cr)),
                  grp((1, cr)), lanes(1, GT * P), lanes(1, GT * P), lanes(nbatch, GT * P), lanes(nbatch, GT * P)],
        out_specs=[lanes(rows, GT * c), lanes(nbatch, GT * P), lanes(nbatch, GT * P)],
        scratch_shapes=[pltpu.VMEM((GT, R, cr), F32)] + [pltpu.VMEM((R, GT * P), F32)] * 4,
        compiler_params=_cparams(("arbitrary",)),
        name="s5",
    )(u, tab['toep'], tab['bs_re'], tab['bs_im'], tab['cs_re'], tab['cs_im'], tab['dvec'],
      tab['ac_re'], tab['ac_im'], s0_re.reshape(nbatch, SP), s0_im.reshape(nbatch, SP))
    return y, f_re.reshape(nbatch, G, P), f_im.reshape(nbatch, G, P)


def _glu_kernel(y_ref, w_ref, b_ref, o_ref):
    z = jax.nn.gelu(y_ref[...])
    acc = jnp.dot(z.astype(BF16), w_ref[...], preferred_element_type=F32) + b_ref[...]
    o_ref[...] = (z * _sigmoid(acc)).astype(BF16)


def gelu_glu(y, w_bf16, b, tm):
    m, k = y.shape
    n = w_bf16.shape[1]
    assert n == k
    return pl.pallas_call(
        _glu_kernel,
        out_shape=jax.ShapeDtypeStruct((m, n), BF16),
        grid=(_cdiv(m, tm),),
        in_specs=[pl.BlockSpec((tm, k), lambda i: (i, 0)),
                  pl.BlockSpec((k, n), lambda i: (0, 0)),
                  pl.BlockSpec((1, n), lambda i: (0, 0))],
        out_specs=pl.BlockSpec((tm, n), lambda i: (i, 0)),
        compiler_params=_cparams(("arbitrary",)),
        name="gelu_glu",
    )(y, w_bf16, b.reshape(1, n))


MOE_TM = 512
MOE_TF = 1024
MOE_TN = 1024


def _router_kernel(x_ref, wr_ref, e_ref, g_ref):
    logits = lax.dot_general(wr_ref[...], x_ref[...], (((1,), (1,)), ((), ())),
                             preferred_element_type=F32, precision=lax.Precision.HIGHEST)
    mx = jnp.max(logits, axis=0, keepdims=True)
    ex = jnp.exp(logits - mx)
    probs = ex / jnp.sum(ex, axis=0, keepdims=True)
    neg = jnp.float32(-jnp.inf)
    best = None
    for gi in range(N_EXPERT_GROUPS):
        v = [probs[gi * EXPERTS_PER_GROUP + r:gi * EXPERTS_PER_GROUP + r + 1, :]
             for r in range(EXPERTS_PER_GROUP)]
        m1 = jnp.maximum(jnp.maximum(v[0], v[1]), jnp.maximum(v[2], v[3]))
        i1 = jnp.where(v[0] == m1, 0, jnp.where(v[1] == m1, 1, jnp.where(v[2] == m1, 2, 3)))
        w = [jnp.where(i1 == r, neg, v[r]) for r in range(EXPERTS_PER_GROUP)]
        m2 = jnp.maximum(jnp.maximum(w[0], w[1]), jnp.maximum(w[2], w[3]))
        i2 = jnp.where(w[0] == m2, 0, jnp.where(w[1] == m2, 1, jnp.where(w[2] == m2, 2, 3)))
        score = m1 + m2
        cand = (score, m1, m2, i1 + gi * EXPERTS_PER_GROUP, i2 + gi * EXPERTS_PER_GROUP)
        if best is None:
            best = cand
        else:
            take = cand[0] > best[0]
            best = tuple(jnp.where(take, cn, bs) for cn, bs in zip(cand, best))
    _, m1, m2, e1, e2 = best
    tot = m1 + m2
    e_ref[...] = jnp.concatenate([e1, e2], axis=0)
    g_ref[...] = jnp.concatenate([m1 / tot, m2 / tot], axis=0)


def router(x, w_router, tm):
    t, d = x.shape
    return pl.pallas_call(
        _router_kernel,
        out_shape=[jax.ShapeDtypeStruct((TOP_K, t), jnp.int32), jax.ShapeDtypeStruct((TOP_K, t), F32)],
        grid=(_cdiv(t, tm),),
        in_specs=[pl.BlockSpec((tm, d), lambda i: (i, 0)),
                  pl.BlockSpec((N_EXPERTS, d), lambda i: (0, 0))],
        out_specs=[pl.BlockSpec((TOP_K, tm), lambda i: (0, i)),
                   pl.BlockSpec((TOP_K, tm), lambda i: (0, i))],
        compiler_params=_cparams(("arbitrary",)),
        name="router",
    )(x, w_router.T)


def _moe_up_kernel(rs_ref, st_ref, sf_ref, se_ref, nu_ref, x_hbm, w1_ref, w2_ref, gate_ref, h_ref,
                   xbuf0, xbuf1, sem):
    s = pl.program_id(0)
    last = pl.num_programs(0) - 1
    tm = xbuf0.shape[0]
    bufs = (xbuf0, xbuf1)

    def row_copy(step, slot, r):
        src = rs_ref[st_ref[step] * tm + r]
        return pltpu.make_async_copy(x_hbm.at[pl.ds(src, 1), :], bufs[slot].at[pl.ds(r, 1), :], sem.at[slot])

    def wait_rows(step, slot):
        def body(r, carry):
            row_copy(step, slot, r).wait()
            return carry
        lax.fori_loop(0, tm, body, 0, unroll=8)

    @pl.when(s == 0)
    def _():
        def body(r, carry):
            row_copy(0, 0, r).start()
            return carry
        lax.fori_loop(0, tm, body, 0, unroll=8)

    def step(slot):
        wait_rows(s, slot)
        nxt = jnp.minimum(s + 1, last)

        @pl.when(s < nu_ref[0])
        def _():
            for r in range(tm):
                row_copy(nxt, 1 - slot, r).start()
            x = bufs[slot][...].astype(BF16)
            a = jnp.dot(x, w1_ref[0, 0].astype(BF16), preferred_element_type=F32)
            b = jnp.dot(x, w2_ref[0, 0].astype(BF16), preferred_element_type=F32)
            h_ref[...] = (a * _sigmoid(a) * b * gate_ref[...]).astype(BF16)

        @pl.when(s >= nu_ref[0])
        def _():
            def body(r, carry):
                row_copy(nxt, 1 - slot, r).start()
                return carry
            lax.fori_loop(0, tm, body, 0, unroll=8)
            h_ref[...] = jnp.zeros(h_ref.shape, h_ref.dtype)

        @pl.when(s == last)
        def _():
            wait_rows(last, 1 - slot)

    for slot in range(2):
        pl.when(s % 2 == slot)(functools.partial(step, slot))


def _moe_down_kernel(st_ref, sf_ref, se_ref, nu_ref, h_ref, w_ref, y_ref):
    @pl.when(pl.program_id(0) < nu_ref[0])
    def _():
        y_ref[...] = jnp.dot(h_ref[...], w_ref[0, 0].astype(BF16), preferred_element_type=F32)

    @pl.when(pl.program_id(0) >= nu_ref[0])
    def _():
        y_ref[...] = jnp.zeros(y_ref.shape, y_ref.dtype)


def _moe_schedule(tiles_e, tile_start, nblocks, n_tiles):
    steps_e = tiles_e * nblocks
    cum = jnp.cumsum(steps_e)
    used = cum[-1]
    s = jnp.arange(n_tiles * nblocks, dtype=jnp.int32)
    sc = jnp.minimum(s, used - 1)
    e = jnp.minimum(jnp.sum((cum[None, :] <= sc[:, None]).astype(jnp.int32), axis=1), N_EXPERTS - 1)
    local = sc - (cum[e] - steps_e[e])
    te = jnp.maximum(tiles_e[e], 1)
    rest = s - used
    tile = jnp.where(s < used, tile_start[e] + local % te, jnp.sum(tiles_e) + rest // nblocks)
    blk = jnp.where(s < used, local // te, rest % nblocks)
    return tile.astype(jnp.int32), blk.astype(jnp.int32), e, used.reshape(1).astype(jnp.int32)


def moe_layer(x, w_router, w_up, w_down, layer):
    t, d = x.shape
    tm = MOE_TM
    eid, gates = router(x, w_router, 1024)
    eid = eid.reshape(-1)
    na = TOP_K * t
    n_tiles = _cdiv(na + N_EXPERTS * (tm - 1), tm)
    npad = n_tiles * tm
    onehot = (eid[None, :] == jnp.arange(N_EXPERTS, dtype=jnp.int32)[:, None]).astype(jnp.int32)
    csum = jnp.cumsum(onehot, axis=1)
    counts = csum[:, -1]
    rank = jnp.sum(csum * onehot, axis=0) - 1
    tiles_e = (counts + tm - 1) // tm
    tile_start = jnp.cumsum(tiles_e) - tiles_e
    pos = tile_start[eid] * tm + rank
    row_a = jnp.full((npad,), -1, jnp.int32).at[pos].set(
        jnp.arange(na, dtype=jnp.int32), mode="promise_in_bounds", unique_indices=True)
    a_c = jnp.maximum(row_a, 0)
    row_src = jnp.where(a_c >= t, a_c - t, a_c)
    row_gate = jnp.where(row_a >= 0, gates.reshape(-1).at[a_c].get(mode="promise_in_bounds"), 0.0)

    nf = D_EXPERT // MOE_TF
    st, sf, se, nu = _moe_schedule(tiles_e, tile_start, nf, n_tiles)
    h = pl.pallas_call(
        _moe_up_kernel,
        out_shape=jax.ShapeDtypeStruct((npad, D_EXPERT), BF16),
        grid_spec=pltpu.PrefetchScalarGridSpec(
            num_scalar_prefetch=5,
            grid=(n_tiles * nf,),
            in_specs=[pl.BlockSpec(memory_space=pl.ANY),
                      pl.BlockSpec((1, 1, d, MOE_TF), lambda s, rs, st, sf, se, nu: (layer, se[s], 0, sf[s])),
                      pl.BlockSpec((1, 1, d, MOE_TF), lambda s, rs, st, sf, se, nu: (layer, se[s], 0, nf + sf[s])),
                      pl.BlockSpec((tm, 1), lambda s, rs, st, sf, se, nu: (st[s], 0))],
            out_specs=pl.BlockSpec((tm, MOE_TF), lambda s, rs, st, sf, se, nu: (st[s], sf[s])),
            scratch_shapes=[pltpu.VMEM((tm, d), F32), pltpu.VMEM((tm, d), F32),
                            pltpu.SemaphoreType.DMA((2,))]),
        compiler_params=_cparams(("arbitrary",)),
        name="moe_up",
    )(row_src, st, sf, se, nu, x, w_up, w_up, row_gate.reshape(npad, 1))

    nn = d // MOE_TN
    st, sf, se, nu = _moe_schedule(tiles_e, tile_start, nn, n_tiles)
    ys = pl.pallas_call(
        _moe_down_kernel,
        out_shape=jax.ShapeDtypeStruct((npad, d), F32),
        grid_spec=pltpu.PrefetchScalarGridSpec(
            num_scalar_prefetch=4,
            grid=(n_tiles * nn,),
            in_specs=[pl.BlockSpec((tm, D_EXPERT), lambda s, st, sf, se, nu: (st[s], 0)),
                      pl.BlockSpec((1, 1, D_EXPERT, MOE_TN),
                                   lambda s, st, sf, se, nu: (layer, se[s], 0, sf[s]))],
            out_specs=pl.BlockSpec((tm, MOE_TN), lambda s, st, sf, se, nu: (st[s], sf[s]))),
        compiler_params=_cparams(("arbitrary",)),
        name="moe_down",
    )(st, sf, se, nu, h, w_down)
    return (ys.at[pos[:t]].get(mode="promise_in_bounds", unique_indices=True),
            ys.at[pos[t:]].get(mode="promise_in_bounds", unique_indices=True))


TOK_TM = 512
MM_TM = 1024


def kernel(x_prompt, x_sample, state_gla, state_rwkv, state_shift, state_s5_re, state_s5_im, meta, ev_w_in, ev_w_out, a_gate_up, a_gate_b, a_norm_g, b_mu, b_w0, b_w_up, b_a0, b_a_up, b_g_up, b_k_k, b_k_a, b_r_k, b_ln_g, b_ln_b, od_w_in, c_a_re, c_a_im, c_log_dt, c_b_re, c_b_im, c_c_re, c_c_im, c_d, c_w_glu, c_b_glu, od_w_out, w_router, moe_w_up, moe_w_down, ln_mix_g, ln_mix_b, ln_ffn_g, ln_ffn_b):
    bp, sp, d = x_prompt.shape
    bs, ss, _ = x_sample.shape
    lp = sp + N_META
    tp = bp * lp
    t = tp + bs * ss
    xp = jnp.concatenate([jnp.broadcast_to(meta[None], (bp, N_META, d)), x_prompt], axis=1)
    x = jnp.concatenate([xp.reshape(tp, d), x_sample.reshape(bs * ss, d)], axis=0)
    xb = x.astype(BF16)

    def ffn(x, xb, layer):
        f0, f1 = moe_layer(x, w_router, moe_w_up, moe_w_down, layer)
        return add_ln(x, f0, f1, ln_ffn_g[layer], ln_ffn_b[layer], TOK_TM)

    w_in = ev_w_in[0]
    zpad = jnp.zeros((d, P_B_OFF - A_COLS), w_in.dtype)
    w_in = jnp.concatenate([w_in[:, :A_COLS], zpad, w_in[:, A_COLS:]], axis=1).astype(BF16)
    p = matmul(xb, w_in, MM_TM, P_COLS // 4)

    gla_w = (a_gate_up[0], a_gate_b[0], a_norm_g[0])
    oa_p, gla_p = gla_group(p, jnp.zeros((bp, A_HEADS, A_DK, A_DV), F32), *gla_w,
                            row_off=0, nbatch=bp, seq=lp, nb=4, lb=48, chunk=16)
    oa_s, gla_s = gla_group(p, state_gla[0], *gla_w,
                            row_off=tp, nbatch=bs, seq=ss, nb=4, lb=ss, chunk=ss)
    rw_w = (b_mu[0], b_w0[0], b_w_up[0], b_a0[0], b_a_up[0], b_g_up[0], b_k_k[0], b_k_a[0],
            b_r_k[0], b_ln_g[0], b_ln_b[0])
    ob_p, rwkv_p = rwkv_group(p, jnp.zeros((bp, B_HEADS, B_HEAD, B_HEAD), F32),
                              jnp.zeros((bp, B_COLS), F32), *rw_w,
                              row_off=0, nbatch=bp, seq=lp, nb=4, lb=48)
    ob_s, rwkv_s = rwkv_group(p, state_rwkv[0], state_shift[0], *rw_w,
                              row_off=tp, nbatch=bs, seq=ss, nb=4, lb=ss)
    shift_p = p[lp - 1:tp:lp, P_B_OFF:]
    shift_s = p[tp + ss - 1::ss, P_B_OFF:]
    o_a = jnp.concatenate([oa_p, oa_s], axis=0)
    o_b = jnp.concatenate([ob_p, ob_s], axis=0)
    x, xb = proj_ln([o_a, o_b], ev_w_out[0].astype(BF16), x, ln_mix_g[0], ln_mix_b[0], TOK_TM)
    x, xb = ffn(x, xb, 0)

    u = matmul(xb, od_w_in[0].astype(BF16), MM_TM, d // 2)
    s5_w = (c_a_re[0], c_a_im[0], c_log_dt[0], c_b_re[0], c_b_im[0], c_c_re[0], c_c_im[0], c_d[0])
    zero_state = jnp.zeros((bp, C_GROUPS, C_STATE), F32)
    y_p, re_p, im_p = s5_group(u, zero_state, zero_state, s5_tables(*s5_w, 16),
                               nbatch=bp, seq=lp, chunk=16)
    y_s, re_s, im_s = s5_group(u[tp:], state_s5_re[0], state_s5_im[0], s5_tables(*s5_w, ss),
                               nbatch=bs, seq=ss, chunk=ss)
    y = jnp.concatenate([y_p, y_s], axis=0)
    z = gelu_glu(y, c_w_glu[0].astype(BF16), c_b_glu[0], TOK_TM)
    x, xb = proj_ln([z], od_w_out[0].astype(BF16), x, ln_mix_g[1], ln_mix_b[1], TOK_TM)
    x, xb = ffn(x, xb, 1)

    y_prompt = x[:tp].reshape(bp, lp, d)[:, N_META:]
    y_sample = x[tp:].reshape(bs, ss, d)
    return (y_prompt, y_sample, gla_p[None], gla_s[None], rwkv_p[None], rwkv_s[None],
            shift_p[None], shift_s[None], re_p[None], re_s[None], im_p[None], im_s[None])
```

```python
import functools
import math

import jax
import jax.numpy as jnp
from jax import lax
from jax.experimental import pallas as pl
from jax.experimental.pallas import tpu as pltpu

F32 = jnp.float32
BF16 = jnp.bfloat16

D_MODEL = 2048
DEPTH = 2
N_META = 16

A_WIDTH = 1024
A_HEADS = 4
A_DV = 256
A_DK = 128
A_GATE_RANK = 16
A_GATE_TAU = 16.0
A_QK = A_HEADS * A_DK
A_COLS = 2 * A_QK + 2 * A_WIDTH + A_GATE_RANK

B_WIDTH = 1024
B_HEAD = 64
B_HEADS = 16
B_DECAY_RANK = 64
B_AAA_RANK = 64
B_GATE_RANK = 128
B_COLS = 3 * B_WIDTH + B_DECAY_RANK + B_AAA_RANK + B_GATE_RANK

C_GROUP = 16
C_GROUPS = 128
C_STATE = 64

N_EXPERTS = 16
N_EXPERT_GROUPS = 4
EXPERTS_PER_GROUP = 4
TOP_K = 2
D_EXPERT = 1024

ALPHA = (2.0 * DEPTH) ** 0.25
LN_EPS = 1e-5
HEAD_NORM_EPS = 1e-5
RWKV_GN_EPS = 64e-5

P_B_OFF = B_COLS
P_COLS = 2 * B_COLS

VMEM_LIMIT = 56 * 1024 * 1024
MXU_TILE = 256
LANES = 128


def _cparams(sem):
    return pltpu.CompilerParams(dimension_semantics=sem, vmem_limit_bytes=VMEM_LIMIT)


def _cdiv(a, b):
    return (a + b - 1) // b


def _softplus(x):
    return jnp.maximum(x, 0.0) + jnp.log(1.0 + jnp.exp(-jnp.abs(x)))


def _sigmoid(x):
    return 1.0 / (1.0 + jnp.exp(-x))


def _mm_kernel(x_ref, w_ref, o_ref):
    o_ref[...] = jnp.dot(x_ref[...].astype(BF16), w_ref[...].astype(BF16),
                         preferred_element_type=F32)


def matmul(x, w, tm, tn):
    m, k = x.shape
    n = w.shape[1]
    assert n % tn == 0
    return pl.pallas_call(
        _mm_kernel,
        out_shape=jax.ShapeDtypeStruct((m, n), F32),
        grid=(n // tn, _cdiv(m, tm)),
        in_specs=[pl.BlockSpec((tm, k), lambda j, i: (i, 0)),
                  pl.BlockSpec((k, tn), lambda j, i: (0, j))],
        out_specs=pl.BlockSpec((tm, tn), lambda j, i: (i, j)),
        compiler_params=_cparams(("arbitrary", "arbitrary")),
        name="matmul",
    )(x, w)


def _layer_norm_rows(y, g, b):
    mu = jnp.mean(y, axis=-1, keepdims=True)
    yc = y - mu
    var = jnp.mean(yc * yc, axis=-1, keepdims=True)
    return yc * lax.rsqrt(var + LN_EPS) * g + b


def _proj_ln_kernel(n_lhs, *refs):
    lhs = refs[:n_lhs]
    w_ref, x_ref, g_ref, b_ref, o_ref, ol_ref = refs[n_lhs:]
    acc = None
    off = 0
    for r in lhs:
        kk = r.shape[1]
        part = jnp.dot(r[...], w_ref[off:off + kk, :], preferred_element_type=F32)
        acc = part if acc is None else acc + part
        off += kk
    y = _layer_norm_rows(ALPHA * x_ref[...] + acc, g_ref[...], b_ref[...])
    o_ref[...] = y
    tm, d = y.shape
    for s in range(d // LANES):
        ol_ref[pl.ds(s, tm, stride=d // LANES), :] = y[:, s * LANES:(s + 1) * LANES]


def proj_ln(lhs_list, w_bf16, x, g, b, tm):
    m, d = x.shape
    kin = w_bf16.shape[0]
    sub = d // LANES
    in_specs = [pl.BlockSpec((tm, l.shape[1]), lambda i: (i, 0)) for l in lhs_list]
    in_specs += [pl.BlockSpec((kin, d), lambda i: (0, 0)),
                 pl.BlockSpec((tm, d), lambda i: (i, 0)),
                 pl.BlockSpec((1, d), lambda i: (0, 0)),
                 pl.BlockSpec((1, d), lambda i: (0, 0))]
    return pl.pallas_call(
        functools.partial(_proj_ln_kernel, len(lhs_list)),
        out_shape=(jax.ShapeDtypeStruct((m, d), F32), jax.ShapeDtypeStruct((m * sub, LANES), F32)),
        grid=(_cdiv(m, tm),),
        in_specs=in_specs,
        out_specs=(pl.BlockSpec((tm, d), lambda i: (i, 0)),
                   pl.BlockSpec((tm * sub, LANES), lambda i: (i, 0))),
        compiler_params=_cparams(("arbitrary",)),
        name="proj_ln",
    )(*lhs_list, w_bf16, x, g.reshape(1, d), b.reshape(1, d))


def _add_ln_kernel(x_ref, f0_ref, f1_ref, g_ref, b_ref, o_ref, ob_ref):
    y = _layer_norm_rows(ALPHA * x_ref[...] + (f0_ref[...] + f1_ref[...]), g_ref[...], b_ref[...])
    o_ref[...] = y
    ob_ref[...] = y.astype(BF16)


def add_ln(x, f0, f1, g, b, tm):
    m, d = x.shape
    row = pl.BlockSpec((tm, d), lambda i: (i, 0))
    vec = pl.BlockSpec((1, d), lambda i: (0, 0))
    return pl.pallas_call(
        _add_ln_kernel,
        out_shape=(jax.ShapeDtypeStruct((m, d), F32), jax.ShapeDtypeStruct((m, d), BF16)),
        grid=(_cdiv(m, tm),),
        in_specs=[row, row, row, vec, vec],
        out_specs=(row, row),
        compiler_params=_cparams(("arbitrary",)),
        name="add_ln",
    )(x, f0, f1, g.reshape(1, d), b.reshape(1, d))


def _gla_kernel(nb, lb, chunk, *refs):
    p_refs = refs[:nb]
    (s0_ref, gup_ref, gb_ref, ng_ref) = refs[nb:nb + 4]
    o_ref = refs[nb + 4]
    sout_ref = refs[nb + 5]
    s_ref = refs[nb + 6]
    blk = pl.program_id(1)

    @pl.when(blk == 0)
    def _():
        s_ref[...] = s0_ref[...]

    rows = lax.broadcasted_iota(jnp.int32, (chunk, chunk), 0)
    cols = lax.broadcasted_iota(jnp.int32, (chunk, chunk), 1)
    tril = rows >= cols
    tril_f = tril.astype(F32)
    lgs = []
    for j in range(nb):
        gd = p_refs[j][:, 2 * A_QK + 2 * A_WIDTH:2 * A_QK + 2 * A_WIDTH + A_GATE_RANK]
        z = jnp.dot(gd, gup_ref[...], preferred_element_type=F32) + gb_ref[...]
        lgs.append(-_softplus(-z) * (1.0 / A_GATE_TAU))
    for c in range(lb // chunk):
        r0 = c * chunk
        for j in range(nb):
            p_ref = p_refs[j]
            bc = jnp.dot(tril_f, lgs[j][r0:r0 + chunk, :], preferred_element_type=F32,
                         precision=lax.Precision.HIGHEST)
            for h in range(A_HEADS):
                q = p_ref[r0:r0 + chunk, h * A_DK:(h + 1) * A_DK] * (A_DK ** -0.5)
                k = p_ref[r0:r0 + chunk, A_QK + h * A_DK:A_QK + (h + 1) * A_DK]
                v = p_ref[r0:r0 + chunk, 2 * A_QK + h * A_DV:2 * A_QK + (h + 1) * A_DV]
                rg = p_ref[r0:r0 + chunk,
                           2 * A_QK + A_WIDTH + h * A_DV:2 * A_QK + A_WIDTH + (h + 1) * A_DV]
                b = bc[:, h * A_DK:(h + 1) * A_DK]
                bl = b[chunk - 1:chunk, :]
                qd = (q * jnp.exp(b)).astype(BF16)
                kd = (k * jnp.exp(-b)).astype(BF16)
                vb = v.astype(BF16)
                att = lax.dot_general(qd, kd, (((1,), (1,)), ((), ())), preferred_element_type=F32)
                att = jnp.where(tril, att, 0.0).astype(BF16)
                s = s_ref[j, h]
                o = (jnp.dot(att, vb, preferred_element_type=F32)
                     + jnp.dot(qd, s.astype(BF16), preferred_element_type=F32))
                kl = (k * jnp.exp(bl - b)).astype(BF16)
                kv = lax.dot_general(kl, vb, (((0,), (0,)), ((), ())), preferred_element_type=F32)
                dec = jnp.broadcast_to(jnp.exp(bl), (A_DK, A_DK)).T
                s_ref[j, h] = s * jnp.concatenate([dec] * (A_DV // A_DK), axis=1) + kv
                o = o * lax.rsqrt(jnp.mean(o * o, axis=-1, keepdims=True) + HEAD_NORM_EPS)
                o = o * ng_ref[:, h * A_DV:(h + 1) * A_DV] * (rg * _sigmoid(rg))
                o_ref[j, r0:r0 + chunk, h * A_DV:(h + 1) * A_DV] = o.astype(BF16)

    @pl.when(blk == pl.num_programs(1) - 1)
    def _():
        sout_ref[...] = s_ref[...]


def gla_group(p, s0, gate_up, gate_b, norm_g, *, row_off, nbatch, seq, nb, lb, chunk):
    t = p.shape[0]
    nblk = seq // lb
    assert seq % lb == 0 and lb % chunk == 0 and nbatch % nb == 0 and row_off % lb == 0
    base = row_off // lb

    def p_map(j):
        return lambda bi, blk: (base + (bi * nb + j) * nblk + blk, 0)

    in_specs = [pl.BlockSpec((lb, P_B_OFF), p_map(j)) for j in range(nb)]
    in_specs += [pl.BlockSpec((nb, A_HEADS, A_DK, A_DV), lambda bi, blk: (bi, 0, 0, 0)),
                 pl.BlockSpec((A_GATE_RANK, A_QK), lambda bi, blk: (0, 0)),
                 pl.BlockSpec((1, A_QK), lambda bi, blk: (0, 0)),
                 pl.BlockSpec((1, A_WIDTH), lambda bi, blk: (0, 0))]

    out_specs = [pl.BlockSpec((nb, lb, A_WIDTH), lambda bi, blk: (bi, blk, 0)),
                 pl.BlockSpec((nb, A_HEADS, A_DK, A_DV), lambda bi, blk: (bi, 0, 0, 0))]
    out_shape = [jax.ShapeDtypeStruct((nbatch, seq, A_WIDTH), BF16),
                 jax.ShapeDtypeStruct((nbatch, A_HEADS, A_DK, A_DV), F32)]
    o, s_new = pl.pallas_call(
        functools.partial(_gla_kernel, nb, lb, chunk),
        out_shape=out_shape,
        grid=(nbatch // nb, nblk),
        in_specs=in_specs,
        out_specs=out_specs,
        scratch_shapes=[pltpu.VMEM((nb, A_HEADS, A_DK, A_DV), F32)],
        compiler_params=_cparams(("arbitrary", "arbitrary")),
        name="gla",
    )(*([p] * nb), s0, gate_up, gate_b.reshape(1, A_QK), norm_g.reshape(1, A_WIDTH))
    return o.reshape(nbatch * seq, A_WIDTH), s_new


VT_SLOT = 128
RWKV_UNROLL = 8


def _head_ones(dtype):
    r = lax.broadcasted_iota(jnp.int32, (MXU_TILE, MXU_TILE), 0) // B_HEAD
    c = lax.broadcasted_iota(jnp.int32, (MXU_TILE, MXU_TILE), 1) // B_HEAD
    return (r == c).astype(dtype)


def _head_sum(x, ones):
    xh = x.astype(BF16)
    xl = (x - xh.astype(F32)).astype(BF16)
    parts = [jnp.dot(xh[:, c * MXU_TILE:(c + 1) * MXU_TILE], ones, preferred_element_type=F32)
             + jnp.dot(xl[:, c * MXU_TILE:(c + 1) * MXU_TILE], ones, preferred_element_type=F32)
             for c in range(B_WIDTH // MXU_TILE)]
    return jnp.concatenate(parts, axis=-1)


def _rwkv_kernel(nb, lb, *refs):
    p_refs = refs[:nb]
    (shift0_ref, s0_ref, mu_ref, w0_ref, wup_ref, a0_ref, aup_ref, gup_ref,
     kk_ref, ka_ref, rk_ref, lng_ref, lnb_ref) = refs[nb:nb + 13]
    o_ref, sout_ref = refs[nb + 13:nb + 15]
    (s_ref, prev_ref, r_s, w_s, kh_s, kl_s, kk_s, kka_s, on_s, g_s, bon_s, vmh_s, vml_s, vt_s
     ) = refs[nb + 15:]
    blk = pl.program_id(1)
    W = B_WIDTH
    H = B_HEADS
    NT = (((1,), (1,)), ((), ()))

    @pl.when(blk == 0)
    def _():
        s_ref[...] = s0_ref[...]
        prev_ref[...] = shift0_ref[...]

    ones = _head_ones(BF16)
    hmask = (lax.broadcasted_iota(jnp.int32, (H, W), 1) // B_HEAD
             == lax.broadcasted_iota(jnp.int32, (H, W), 0))

    def to_head_rows(ref, j, x):
        ref[j] = jnp.zeros(ref.shape[1:], F32)
        for h in range(H):
            ref[j, pl.ds(h, lb, stride=H), 0:B_HEAD] = x[:, h * B_HEAD:(h + 1) * B_HEAD]

    def split(x):
        xh = x.astype(BF16).astype(F32)
        return xh, x - xh

    xms = []
    for j in range(nb):
        pb = p_refs[j][...]
        first = lax.broadcasted_iota(jnp.int32, (lb, 1), 0) == 0
        prev = jnp.where(first, prev_ref[j], pltpu.roll(pb, 1, axis=0))
        prev_ref[j] = pb[lb - 1:lb, :]
        xms.append(pb + (prev - pb) * mu_ref[...])
    xm = jnp.concatenate(xms, axis=0)
    r = xm[:, :W]
    k = xm[:, W:2 * W]
    v = xm[:, 2 * W:3 * W]
    wd = xm[:, 3 * W:3 * W + B_DECAY_RANK]
    ad = xm[:, 3 * W + B_DECAY_RANK:3 * W + B_DECAY_RANK + B_AAA_RANK]
    gd = xm[:, 3 * W + B_DECAY_RANK + B_AAA_RANK:]
    w = -_softplus(-(w0_ref[...] + jnp.dot(jnp.tanh(wd).astype(BF16), wup_ref[...].astype(BF16),
                                           preferred_element_type=F32))) - 0.5
    a = _sigmoid(a0_ref[...] + jnp.dot(ad.astype(BF16), aup_ref[...].astype(BF16),
                                       preferred_element_type=F32))
    g = jnp.dot(_sigmoid(gd).astype(BF16), gup_ref[...].astype(BF16), preferred_element_type=F32)
    kk = k * kk_ref[...]
    nrm = jnp.sqrt(_head_sum(kk * kk, ones))
    kk = kk / jnp.maximum(nrm, 1e-12)
    k2 = k * (1.0 + (a - 1.0) * ka_ref[...])
    per_seq = lambda x: x.reshape(nb, lb, x.shape[-1])
    r_s[...] = per_seq(r)
    w_s[...] = per_seq(jnp.exp(-jnp.exp(w)))
    kh, kl = split(k2)
    kh_s[...], kl_s[...] = per_seq(kh), per_seq(kl)
    kk_s[...] = per_seq(kk)
    kka_s[...] = per_seq(-(kk * a))
    vh, vl = split(v)
    for j in range(nb):
        to_head_rows(vmh_s, j, vh[j * lb:(j + 1) * lb])
        to_head_rows(vml_s, j, vl[j * lb:(j + 1) * lb])
    g_s[...] = per_seq(g)
    bon_s[...] = per_seq(_head_sum(r * k2 * rk_ref[...], ones) * v)

    def head_rows(ref, j, t):
        return jnp.where(hmask, jnp.broadcast_to(ref[j, pl.ds(t, 1), :], (H, W)), 0.0).astype(BF16)

    def readout(j, t, sb):
        rows = pl.ds(t * H if isinstance(t, int) else pl.multiple_of(t * H, H), H)
        on_s[j, rows, :] = lax.dot_general(head_rows(r_s, j, t), sb, NT, preferred_element_type=F32)

    zrows = lambda n: jnp.zeros((n * H, W), BF16)
    lane = lax.broadcasted_iota(jnp.int32, (B_HEAD, VT_SLOT), 1)

    def vt_step(t, carry):
        rows = pl.ds(pl.multiple_of(t * H, H), H)
        for j in range(nb):
            vmh, vml = vmh_s[j, rows, :], vml_s[j, rows, :]
            tile = jnp.concatenate([vmh, vml, vmh, jnp.zeros((VT_SLOT - 3 * H, VT_SLOT), F32)], axis=0)
            vt_s[j, t] = tile.T[:B_HEAD, :].astype(BF16)
        return carry

    lax.fori_loop(0, lb, vt_step, 0)

    def step(t, carry):
        sbs, sas = [], []
        for j in range(nb):
            sb = s_ref[j].astype(BF16)
            kkw = head_rows(kk_s, j, t)
            sas.append(lax.dot_general(
                sb, jnp.concatenate([zrows(3), kkw, kkw, zrows(VT_SLOT // H - 5)], axis=0),
                NT, preferred_element_type=F32))
            sbs.append(sb)
        for j in range(nb):
            readout(j, jnp.maximum(t - 1, 0), sbs[j])
        for j in range(nb):
            sa = sas[j]
            sa_hi = sa.astype(BF16)
            sa_lo = (sa - sa_hi.astype(F32)).astype(BF16)
            x = jnp.where(lane < 3 * H, vt_s[j, t], jnp.where(lane < 4 * H, sa_hi, sa_lo))
            khw, klw, kaw = head_rows(kh_s, j, t), head_rows(kl_s, j, t), head_rows(kka_s, j, t)
            upd = jnp.dot(x[:, :5 * H], jnp.concatenate([khw, khw, klw, kaw, kaw], axis=0),
                          preferred_element_type=F32)
            s_ref[j] = s_ref[j] * w_s[j, pl.ds(t, 1), :] + upd
        return carry

    lax.fori_loop(0, lb, step, 0, unroll=RWKV_UNROLL)

    os = []
    for j in range(nb):
        readout(j, lb - 1, s_ref[j].astype(BF16))
        os.append(jnp.concatenate([on_s[j, pl.ds(h, lb, stride=H), :] for h in range(H)], axis=-1))
    o = jnp.concatenate(os, axis=0)
    oc = o - _head_sum(o, ones) * (1.0 / B_HEAD)
    var = _head_sum(oc * oc, ones) * (1.0 / B_HEAD)
    o = per_seq(oc * lax.rsqrt(var + RWKV_GN_EPS) * lng_ref[...] + lnb_ref[...])
    o_ref[...] = ((o + bon_s[...]) * g_s[...]).astype(BF16)

    @pl.when(blk == pl.num_programs(1) - 1)
    def _():
        sout_ref[...] = s_ref[...]


def rwkv_group(p, s0, shift0, mu, w0, w_up, a0, a_up, g_up, k_k, k_a, r_k, ln_g, ln_b,
               *, row_off, nbatch, seq, nb, lb):
    nblk = seq // lb
    assert seq % lb == 0 and nbatch % nb == 0 and row_off % lb == 0
    base = row_off // lb
    W = B_WIDTH
    s0 = s0.transpose(0, 2, 1, 3).reshape(nbatch, B_HEAD, W)

    def p_map(j):
        return lambda bi, blk: (base + (bi * nb + j) * nblk + blk, 1)

    const = lambda shape: pl.BlockSpec(shape, lambda bi, blk: (0,) * len(shape))
    in_specs = [pl.BlockSpec((lb, B_COLS), p_map(j)) for j in range(nb)]
    in_specs += [pl.BlockSpec((nb, 1, B_COLS), lambda bi, blk: (bi, 0, 0)),
                 pl.BlockSpec((nb, B_HEAD, W), lambda bi, blk: (bi, 0, 0)),
                 const((1, B_COLS)), const((1, W)), const((B_DECAY_RANK, W)), const((1, W)),
                 const((B_AAA_RANK, W)), const((B_GATE_RANK, W)),
                 const((1, W)), const((1, W)), const((1, W)), const((1, W)), const((1, W))]
    out_specs = [pl.BlockSpec((nb, lb, W), lambda bi, blk: (bi, blk, 0)),
                 pl.BlockSpec((nb, B_HEAD, W), lambda bi, blk: (bi, 0, 0))]
    out_shape = [jax.ShapeDtypeStruct((nbatch, seq, W), BF16),
                 jax.ShapeDtypeStruct((nbatch, B_HEAD, W), F32)]
    tok = pltpu.VMEM((nb, lb, W), F32)
    o, s_new = pl.pallas_call(
        functools.partial(_rwkv_kernel, nb, lb),
        out_shape=out_shape,
        grid=(nbatch // nb, nblk),
        in_specs=in_specs,
        out_specs=out_specs,
        scratch_shapes=[pltpu.VMEM((nb, B_HEAD, W), F32), pltpu.VMEM((nb, 1, B_COLS), F32)]
        + [tok] * 6 + [pltpu.VMEM((nb, lb * B_HEADS, B_HEAD), F32), tok, tok,
                       pltpu.VMEM((nb, lb * B_HEADS, VT_SLOT), F32),
                       pltpu.VMEM((nb, lb * B_HEADS, VT_SLOT), F32),
                       pltpu.VMEM((nb, lb, B_HEAD, VT_SLOT), BF16)],
        compiler_params=_cparams(("arbitrary", "arbitrary")),
        name="rwkv",
    )(*([p] * nb), shift0.reshape(nbatch, 1, B_COLS), s0, mu.reshape(1, B_COLS),
      w0.reshape(1, W), w_up, a0.reshape(1, W), a_up, g_up, k_k.reshape(1, W),
      k_a.reshape(1, W), r_k.reshape(1, W), ln_g.reshape(1, W), ln_b.reshape(1, W))
    s_new = s_new.reshape(nbatch, B_HEAD, B_HEADS, B_HEAD).transpose(0, 2, 1, 3)
    return o.reshape(nbatch * seq, W), s_new


S5_GT = LANES // C_GROUP
SP = C_GROUPS * C_STATE


def s5_tables(a_re, a_im, log_dt, b_re, b_im, c_re, c_im, d, chunk):
    G, P, c = C_GROUPS, C_STATE, C_GROUP
    dt = jnp.exp(log_dt)[:, None]
    m = jnp.arange(chunk + 1, dtype=F32)[:, None, None]
    mag = jnp.exp(m * (dt * a_re))
    ang = m * (dt * a_im)
    pw_re, pw_im = mag * jnp.cos(ang), mag * jnp.sin(ang)
    num_re, num_im = pw_re[1] - 1.0, pw_im[1]
    den = a_re * a_re + a_im * a_im
    q_re = (num_re * a_re + num_im * a_im) / den
    q_im = (num_im * a_re - num_re * a_im) / den
    bb_re = q_re[..., None] * b_re - q_im[..., None] * b_im
    bb_im = q_re[..., None] * b_im + q_im[..., None] * b_re
    ca_re = c_re[None] * pw_re[:, :, None, :] - c_im[None] * pw_im[:, :, None, :]
    ca_im = c_re[None] * pw_im[:, :, None, :] + c_im[None] * pw_re[:, :, None, :]
    cm = lambda z: z[:chunk].transpose(1, 3, 0, 2).reshape(G, P, chunk * c)
    toep = s5_toeplitz(bb_re.transpose(0, 2, 1), bb_im.transpose(0, 2, 1), cm(ca_re), cm(ca_im), chunk)
    dvec = jnp.tile(d.reshape(G, 1, c), (1, chunk, 1)).reshape(G, 1, chunk * c)
    rev_re, rev_im = pw_re[:chunk][::-1], pw_im[:chunk][::-1]
    bs_re = rev_re[..., None] * bb_re[None] - rev_im[..., None] * bb_im[None]
    bs_im = rev_re[..., None] * bb_im[None] + rev_im[..., None] * bb_re[None]
    to_in = lambda z: z.transpose(1, 0, 3, 2).reshape(G, chunk * c, P)
    to_out = lambda z: z.transpose(1, 3, 0, 2).reshape(G, P, chunk * c)
    cs_re, cs_im = to_out(ca_re[1:]), to_out(-ca_im[1:])
    return dict(toep=toep, dvec=dvec,
                bs_re=to_in(bs_re).astype(BF16), bs_im=to_in(bs_im).astype(BF16),
                cs_re=cs_re.astype(BF16), cs_im=cs_im.astype(BF16),
                ac_re=pw_re[chunk].reshape(1, SP), ac_im=pw_im[chunk].reshape(1, SP))


def _s5_toeplitz_kernel(chunk, bre_ref, bim_ref, cre_ref, cim_ref, o_ref):
    c = C_GROUP
    cr = chunk * c
    hi = lax.Precision.HIGHEST
    lane = lax.broadcasted_iota(jnp.int32, (c, cr), 1)
    for q in range(S5_GT):
        k0 = (jnp.dot(bre_ref[q], cre_ref[q], preferred_element_type=F32, precision=hi)
              - jnp.dot(bim_ref[q], cim_ref[q], preferred_element_type=F32, precision=hi))
        for j in range(chunk):
            blk = k0 if j == 0 else jnp.where(lane >= j * c, pltpu.roll(k0, j * c, axis=1), 0.0)
            o_ref[q, j * c:(j + 1) * c, :] = blk.astype(BF16)


def s5_toeplitz(bbt_re, bbt_im, cm_re, cm_im, chunk):
    G, P, c = C_GROUPS, C_STATE, C_GROUP
    cr = chunk * c
    grp = lambda shape: pl.BlockSpec((S5_GT,) + shape, lambda g: (g, 0, 0))
    return pl.pallas_call(
        functools.partial(_s5_toeplitz_kernel, chunk),
        out_shape=jax.ShapeDtypeStruct((G, cr, cr), BF16),
        grid=(G // S5_GT,),
        in_specs=[grp((c, P)), grp((c, P)), grp((P, cr)), grp((P, cr))],
        out_specs=grp((cr, cr)),
        compiler_params=_cparams(("arbitrary",)),
        name="s5_toeplitz",
    )(bbt_re, bbt_im, cm_re, cm_im)


def _swap_blocks(xs, blk):
    n = len(xs)
    w = xs[0].shape[-1] // n
    s = n // 2
    while s >= 1:
        nxt = list(xs)
        for i in range(n):
            if i & s == 0:
                j = i + s
                low = (blk & s) == 0
                nxt[i] = jnp.where(low, xs[i], pltpu.roll(xs[j], s * w, axis=1))
                nxt[j] = jnp.where(low, pltpu.roll(xs[i], (n - s) * w, axis=1), xs[j])
        xs = nxt
        s //= 2
    return xs


def _s5_kernel(nbatch, nchunks, chunk, u_ref, toep_ref, bre_ref, bim_ref, cre_ref, cim_ref, dvec_ref,
               acre_ref, acim_ref, s0re_ref, s0im_ref, y_ref, fre_ref, fim_ref,
               uf_s, vre_s, vim_s, sre_s, sim_s):
    c, P, GT = C_GROUP, C_STATE, S5_GT
    R = nbatch * nchunks
    cr = chunk * c
    halves = cr // LANES
    per_half = LANES // c
    blk = lax.broadcasted_iota(jnp.int32, (R, LANES), 1) // c

    for h in range(halves):
        xs = _swap_blocks([u_ref[pl.ds(h * per_half + q, R, stride=chunk), :] for q in range(per_half)], blk)
        for g in range(GT):
            uf_s[g, :, h * LANES:(h + 1) * LANES] = xs[g]

    for g in range(GT):
        ub = uf_s[g].astype(BF16)
        vre_s[:, g * P:(g + 1) * P] = jnp.dot(ub, bre_ref[g], preferred_element_type=F32)
        vim_s[:, g * P:(g + 1) * P] = jnp.dot(ub, bim_ref[g], preferred_element_type=F32)

    ar, ai = acre_ref[...], acim_ref[...]
    if nchunks == 1:
        sre_s[...] = s0re_ref[...]
        sim_s[...] = s0im_ref[...]
        sr, si = s0re_ref[...], s0im_ref[...]
        fre_ref[...] = ar * sr - ai * si + vre_s[...]
        fim_ref[...] = ar * si + ai * sr + vim_s[...]
    else:
        def step(ci, carry):
            sr, si = carry
            vr = jnp.concatenate([vre_s[pl.ds(b * nchunks + ci, 1), :] for b in range(nbatch)], axis=0)
            vi = jnp.concatenate([vim_s[pl.ds(b * nchunks + ci, 1), :] for b in range(nbatch)], axis=0)
            for b in range(nbatch):
                sre_s[pl.ds(b * nchunks + ci, 1), :] = sr[b:b + 1]
                sim_s[pl.ds(b * nchunks + ci, 1), :] = si[b:b + 1]
            return ar * sr - ai * si + vr, ar * si + ai * sr + vi

        sr, si = lax.fori_loop(0, nchunks, step, (s0re_ref[...], s0im_ref[...]))
        fre_ref[...] = sr
        fim_ref[...] = si

    for g in range(GT):
        u = uf_s[g]
        y = (jnp.dot(u.astype(BF16), toep_ref[g], preferred_element_type=F32)
             + jnp.dot(sre_s[:, g * P:(g + 1) * P].astype(BF16), cre_ref[g], preferred_element_type=F32)
             + jnp.dot(sim_s[:, g * P:(g + 1) * P].astype(BF16), cim_ref[g], preferred_element_type=F32))
        uf_s[g] = y + dvec_ref[g] * u
    for h in range(halves):
        ys = _swap_blocks([uf_s[g, :, h * LANES:(h + 1) * LANES] for g in range(GT)], blk)
        for q in range(per_half):
            y_ref[pl.ds(h * per_half + q, R, stride=chunk), :] = ys[q]


def s5_group(u, s0_re, s0_im, tab, *, nbatch, seq, chunk):
    G, P, c, GT = C_GROUPS, C_STATE, C_GROUP, S5_GT
    nchunks = seq // chunk
    assert seq % chunk == 0 and u.shape[0] >= nbatch * seq and u.shape[1] == G * c
    R = nbatch * nchunks
    cr = chunk * c
    rows = nbatch * seq
    grp = lambda shape: pl.BlockSpec((GT,) + shape, lambda i: (i, 0, 0))
    lanes = lambda r, w: pl.BlockSpec((r, w), lambda i: (0, i))
    y, f_re, f_im = pl.pallas_call(
        functools.partial(_s5_kernel, nbatch, nchunks, chunk),
        out_shape=[jax.ShapeDtypeStruct((rows, G * c), F32)] + [jax.ShapeDtypeStruct((nbatch, SP), F32)] * 2,
        grid=(G // GT,),
        in_specs=[lanes(rows, GT * c), grp((cr, cr)), grp((cr, P)), grp((cr, P)), grp((P, cr)), grp((P, cr)),
                  grp((1, cr)), lanes(1, GT * P), lanes(1, GT * P), lanes(nbatch, GT * P), lanes(nbatch, GT * P)],
        out_specs=[lanes(rows, GT * c), lanes(nbatch, GT * P), lanes(nbatch, GT * P)],
        scratch_shapes=[pltpu.VMEM((GT, R, cr), F32)] + [pltpu.VMEM((R, GT * P), F32)] * 4,
        compiler_params=_cparams(("arbitrary",)),
        name="s5",
    )(u, tab['toep'], tab['bs_re'], tab['bs_im'], tab['cs_re'], tab['cs_im'], tab['dvec'],
      tab['ac_re'], tab['ac_im'], s0_re.reshape(nbatch, SP), s0_im.reshape(nbatch, SP))
    return y, f_re.reshape(nbatch, G, P), f_im.reshape(nbatch, G, P)


def _glu_kernel(y_ref, w_ref, b_ref, o_ref):
    z = jax.nn.gelu(y_ref[...])
    acc = jnp.dot(z.astype(BF16), w_ref[...], preferred_element_type=F32) + b_ref[...]
    o_ref[...] = (z * _sigmoid(acc)).astype(BF16)


def gelu_glu(y, w_bf16, b, tm):
    m, k = y.shape
    n = w_bf16.shape[1]
    assert n == k
    return pl.pallas_call(
        _glu_kernel,
        out_shape=jax.ShapeDtypeStruct((m, n), BF16),
        grid=(_cdiv(m, tm),),
        in_specs=[pl.BlockSpec((tm, k), lambda i: (i, 0)),
                  pl.BlockSpec((k, n), lambda i: (0, 0)),
                  pl.BlockSpec((1, n), lambda i: (0, 0))],
        out_specs=pl.BlockSpec((tm, n), lambda i: (i, 0)),
        compiler_params=_cparams(("arbitrary",)),
        name="gelu_glu",
    )(y, w_bf16, b.reshape(1, n))


MOE_TM = 512
MOE_TF = 1024
MOE_TN = 1024


def _router_kernel(x_ref, wr_ref, e_ref, g_ref):
    logits = lax.dot_general(wr_ref[...], x_ref[...], (((1,), (1,)), ((), ())),
                             preferred_element_type=F32, precision=lax.Precision.HIGHEST)
    mx = jnp.max(logits, axis=0, keepdims=True)
    ex = jnp.exp(logits - mx)
    probs = ex / jnp.sum(ex, axis=0, keepdims=True)
    neg = jnp.float32(-jnp.inf)
    best = None
    for gi in range(N_EXPERT_GROUPS):
        v = [probs[gi * EXPERTS_PER_GROUP + r:gi * EXPERTS_PER_GROUP + r + 1, :]
             for r in range(EXPERTS_PER_GROUP)]
        m1 = jnp.maximum(jnp.maximum(v[0], v[1]), jnp.maximum(v[2], v[3]))
        i1 = jnp.where(v[0] == m1, 0, jnp.where(v[1] == m1, 1, jnp.where(v[2] == m1, 2, 3)))
        w = [jnp.where(i1 == r, neg, v[r]) for r in range(EXPERTS_PER_GROUP)]
        m2 = jnp.maximum(jnp.maximum(w[0], w[1]), jnp.maximum(w[2], w[3]))
        i2 = jnp.where(w[0] == m2, 0, jnp.where(w[1] == m2, 1, jnp.where(w[2] == m2, 2, 3)))
        score = m1 + m2
        cand = (score, m1, m2, i1 + gi * EXPERTS_PER_GROUP, i2 + gi * EXPERTS_PER_GROUP)
        if best is None:
            best = cand
        else:
            take = cand[0] > best[0]
            best = tuple(jnp.where(take, cn, bs) for cn, bs in zip(cand, best))
    _, m1, m2, e1, e2 = best
    tot = m1 + m2
    e_ref[...] = jnp.concatenate([e1, e2], axis=0)
    g_ref[...] = jnp.concatenate([m1 / tot, m2 / tot], axis=0)


def router(x, w_router, tm):
    t, d = x.shape
    return pl.pallas_call(
        _router_kernel,
        out_shape=[jax.ShapeDtypeStruct((TOP_K, t), jnp.int32), jax.ShapeDtypeStruct((TOP_K, t), F32)],
        grid=(_cdiv(t, tm),),
        in_specs=[pl.BlockSpec((tm, d), lambda i: (i, 0)),
                  pl.BlockSpec((N_EXPERTS, d), lambda i: (0, 0))],
        out_specs=[pl.BlockSpec((TOP_K, tm), lambda i: (0, i)),
                   pl.BlockSpec((TOP_K, tm), lambda i: (0, i))],
        compiler_params=_cparams(("arbitrary",)),
        name="router",
    )(x, w_router.T)


def _moe_up_kernel(rs_ref, st_ref, sf_ref, se_ref, nu_ref, x_hbm, w1_ref, w2_ref, gate_ref, h_ref,
                   xbuf0, xbuf1, sem):
    s = pl.program_id(0)
    last = pl.num_programs(0) - 1
    sub = D_MODEL // LANES
    tm = xbuf0.shape[0] // sub
    bufs = (xbuf0, xbuf1)

    def row_copy(step, slot, r):
        src = rs_ref[st_ref[step] * tm + r]
        return pltpu.make_async_copy(x_hbm.at[pl.ds(src * sub, sub), :],
                                     bufs[slot].at[pl.ds(r * sub, sub), :], sem.at[slot])

    def tile(slot):
        return jnp.concatenate([bufs[slot][pl.ds(c, tm, stride=sub), :] for c in range(sub)], axis=-1)

    def wait_rows(step, slot):
        def body(r, carry):
            row_copy(step, slot, r).wait()
            return carry
        lax.fori_loop(0, tm, body, 0, unroll=8)

    @pl.when(s == 0)
    def _():
        def body(r, carry):
            row_copy(0, 0, r).start()
            return carry
        lax.fori_loop(0, tm, body, 0, unroll=8)

    def step(slot):
        wait_rows(s, slot)
        nxt = jnp.minimum(s + 1, last)

        @pl.when(s < nu_ref[0])
        def _():
            for r in range(tm):
                row_copy(nxt, 1 - slot, r).start(priority=r % 2)
            x = tile(slot).astype(BF16)
            a = jnp.dot(x, w1_ref[0, 0].astype(BF16), preferred_element_type=F32)
            b = jnp.dot(x, w2_ref[0, 0].astype(BF16), preferred_element_type=F32)
            h_ref[...] = (a * _sigmoid(a) * b * gate_ref[...]).astype(BF16)

        @pl.when(s >= nu_ref[0])
        def _():
            def body(r, carry):
                row_copy(nxt, 1 - slot, r).start()
                return carry
            lax.fori_loop(0, tm, body, 0, unroll=8)
            h_ref[...] = jnp.zeros(h_ref.shape, h_ref.dtype)

        @pl.when(s == last)
        def _():
            wait_rows(last, 1 - slot)

    for slot in range(2):
        pl.when(s % 2 == slot)(functools.partial(step, slot))


def _moe_down_kernel(st_ref, sf_ref, se_ref, nu_ref, h_ref, w_ref, y_ref):
    @pl.when(pl.program_id(0) < nu_ref[0])
    def _():
        y_ref[...] = jnp.dot(h_ref[...], w_ref[0, 0].astype(BF16), preferred_element_type=F32)

    @pl.when(pl.program_id(0) >= nu_ref[0])
    def _():
        y_ref[...] = jnp.zeros(y_ref.shape, y_ref.dtype)


def _moe_schedule(tiles_e, tile_start, nblocks, n_tiles):
    steps_e = tiles_e * nblocks
    cum = jnp.cumsum(steps_e)
    used = cum[-1]
    s = jnp.arange(n_tiles * nblocks, dtype=jnp.int32)
    sc = jnp.minimum(s, used - 1)
    e = jnp.minimum(jnp.sum((cum[None, :] <= sc[:, None]).astype(jnp.int32), axis=1), N_EXPERTS - 1)
    local = sc - (cum[e] - steps_e[e])
    te = jnp.maximum(tiles_e[e], 1)
    rest = s - used
    tile = jnp.where(s < used, tile_start[e] + local % te, jnp.sum(tiles_e) + rest // nblocks)
    blk = jnp.where(s < used, local // te, rest % nblocks)
    return tile.astype(jnp.int32), blk.astype(jnp.int32), e, used.reshape(1).astype(jnp.int32)


def moe_layer(x, x_lin, w_router, w_up, w_down, layer):
    t, d = x.shape
    tm = MOE_TM
    eid, gates = router(x, w_router, 1024)
    eid = eid.reshape(-1)
    na = TOP_K * t
    n_tiles = _cdiv(na + N_EXPERTS * (tm - 1), tm)
    npad = n_tiles * tm
    onehot = (eid[None, :] == jnp.arange(N_EXPERTS, dtype=jnp.int32)[:, None]).astype(jnp.int32)
    csum = jnp.cumsum(onehot, axis=1)
    counts = csum[:, -1]
    rank = jnp.sum(csum * onehot, axis=0) - 1
    tiles_e = (counts + tm - 1) // tm
    tile_start = jnp.cumsum(tiles_e) - tiles_e
    pos = tile_start[eid] * tm + rank
    row_a = jnp.full((npad,), -1, jnp.int32).at[pos].set(
        jnp.arange(na, dtype=jnp.int32), mode="promise_in_bounds", unique_indices=True)
    a_c = jnp.maximum(row_a, 0)
    row_src = jnp.where(a_c >= t, a_c - t, a_c)
    row_gate = jnp.where(row_a >= 0, gates.reshape(-1).at[a_c].get(mode="promise_in_bounds"), 0.0)

    nf = D_EXPERT // MOE_TF
    st, sf, se, nu = _moe_schedule(tiles_e, tile_start, nf, n_tiles)
    h = pl.pallas_call(
        _moe_up_kernel,
        out_shape=jax.ShapeDtypeStruct((npad, D_EXPERT), BF16),
        grid_spec=pltpu.PrefetchScalarGridSpec(
            num_scalar_prefetch=5,
            grid=(n_tiles * nf,),
            in_specs=[pl.BlockSpec(memory_space=pl.ANY),
                      pl.BlockSpec((1, 1, d, MOE_TF), lambda s, rs, st, sf, se, nu: (layer, se[s], 0, sf[s])),
                      pl.BlockSpec((1, 1, d, MOE_TF), lambda s, rs, st, sf, se, nu: (layer, se[s], 0, nf + sf[s])),
                      pl.BlockSpec((tm, 1), lambda s, rs, st, sf, se, nu: (st[s], 0))],
            out_specs=pl.BlockSpec((tm, MOE_TF), lambda s, rs, st, sf, se, nu: (st[s], sf[s])),
            scratch_shapes=[pltpu.VMEM((tm * (d // LANES), LANES), F32)] * 2
            + [pltpu.SemaphoreType.DMA((2,))]),
        compiler_params=_cparams(("arbitrary",)),
        name="moe_up",
    )(row_src, st, sf, se, nu, x_lin, w_up, w_up, row_gate.reshape(npad, 1))

    nn = d // MOE_TN
    st, sf, se, nu = _moe_schedule(tiles_e, tile_start, nn, n_tiles)
    ys = pl.pallas_call(
        _moe_down_kernel,
        out_shape=jax.ShapeDtypeStruct((npad, d), F32),
        grid_spec=pltpu.PrefetchScalarGridSpec(
            num_scalar_prefetch=4,
            grid=(n_tiles * nn,),
            in_specs=[pl.BlockSpec((tm, D_EXPERT), lambda s, st, sf, se, nu: (st[s], 0)),
                      pl.BlockSpec((1, 1, D_EXPERT, MOE_TN),
                                   lambda s, st, sf, se, nu: (layer, se[s], 0, sf[s]))],
            out_specs=pl.BlockSpec((tm, MOE_TN), lambda s, st, sf, se, nu: (st[s], sf[s]))),
        compiler_params=_cparams(("arbitrary",)),
        name="moe_down",
    )(st, sf, se, nu, h, w_down)
    return (ys.at[pos[:t]].get(mode="promise_in_bounds", unique_indices=True),
            ys.at[pos[t:]].get(mode="promise_in_bounds", unique_indices=True))


TOK_TM = 512
MM_TM = 1024


def kernel(x_prompt, x_sample, state_gla, state_rwkv, state_shift, state_s5_re, state_s5_im, meta, ev_w_in, ev_w_out, a_gate_up, a_gate_b, a_norm_g, b_mu, b_w0, b_w_up, b_a0, b_a_up, b_g_up, b_k_k, b_k_a, b_r_k, b_ln_g, b_ln_b, od_w_in, c_a_re, c_a_im, c_log_dt, c_b_re, c_b_im, c_c_re, c_c_im, c_d, c_w_glu, c_b_glu, od_w_out, w_router, moe_w_up, moe_w_down, ln_mix_g, ln_mix_b, ln_ffn_g, ln_ffn_b):
    bp, sp, d = x_prompt.shape
    bs, ss, _ = x_sample.shape
    lp = sp + N_META
    tp = bp * lp
    t = tp + bs * ss
    pieces = [piece for b in range(bp) for piece in (meta, x_prompt[b])]
    x = jnp.concatenate(pieces + [x_sample.reshape(bs * ss, d)], axis=0)
    xb = x.astype(BF16)

    def ffn(x, x_lin, layer):
        f0, f1 = moe_layer(x, x_lin, w_router, moe_w_up, moe_w_down, layer)
        return add_ln(x, f0, f1, ln_ffn_g[layer], ln_ffn_b[layer], TOK_TM)

    w_in = ev_w_in[0]
    zpad = jnp.zeros((d, P_B_OFF - A_COLS), w_in.dtype)
    w_in = jnp.concatenate([w_in[:, :A_COLS], zpad, w_in[:, A_COLS:]], axis=1).astype(BF16)
    p = matmul(xb, w_in, MM_TM, P_COLS // 4)

    gla_w = (a_gate_up[0], a_gate_b[0], a_norm_g[0])
    oa_p, gla_p = gla_group(p, jnp.zeros((bp, A_HEADS, A_DK, A_DV), F32), *gla_w,
                            row_off=0, nbatch=bp, seq=lp, nb=4, lb=48, chunk=16)
    oa_s, gla_s = gla_group(p, state_gla[0], *gla_w,
                            row_off=tp, nbatch=bs, seq=ss, nb=4, lb=ss, chunk=ss)
    rw_w = (b_mu[0], b_w0[0], b_w_up[0], b_a0[0], b_a_up[0], b_g_up[0], b_k_k[0], b_k_a[0],
            b_r_k[0], b_ln_g[0], b_ln_b[0])
    ob_p, rwkv_p = rwkv_group(p, jnp.zeros((bp, B_HEADS, B_HEAD, B_HEAD), F32),
                              jnp.zeros((bp, B_COLS), F32), *rw_w,
                              row_off=0, nbatch=bp, seq=lp, nb=4, lb=48)
    ob_s, rwkv_s = rwkv_group(p, state_rwkv[0], state_shift[0], *rw_w,
                              row_off=tp, nbatch=bs, seq=ss, nb=4, lb=ss)
    last_p = jnp.arange(bp, dtype=jnp.int32) * lp + (lp - 1)
    last_s = jnp.arange(bs, dtype=jnp.int32) * ss + (tp + ss - 1)
    shift_p = p.at[last_p].get(mode="promise_in_bounds")[:, P_B_OFF:]
    shift_s = p.at[last_s].get(mode="promise_in_bounds")[:, P_B_OFF:]
    o_a = jnp.concatenate([oa_p, oa_s], axis=0)
    o_b = jnp.concatenate([ob_p, ob_s], axis=0)
    x, x_lin = proj_ln([o_a, o_b], ev_w_out[0].astype(BF16), x, ln_mix_g[0], ln_mix_b[0], TOK_TM)
    x, xb = ffn(x, x_lin, 0)

    u = matmul(xb, od_w_in[0].astype(BF16), MM_TM, d // 2)
    s5_w = (c_a_re[0], c_a_im[0], c_log_dt[0], c_b_re[0], c_b_im[0], c_c_re[0], c_c_im[0], c_d[0])
    zero_state = jnp.zeros((bp, C_GROUPS, C_STATE), F32)
    y_p, re_p, im_p = s5_group(u, zero_state, zero_state, s5_tables(*s5_w, 16),
                               nbatch=bp, seq=lp, chunk=16)
    y_s, re_s, im_s = s5_group(u[tp:], state_s5_re[0], state_s5_im[0], s5_tables(*s5_w, ss),
                               nbatch=bs, seq=ss, chunk=ss)
    y = jnp.concatenate([y_p, y_s], axis=0)
    z = gelu_glu(y, c_w_glu[0].astype(BF16), c_b_glu[0], TOK_TM)
    x, x_lin = proj_ln([z], od_w_out[0].astype(BF16), x, ln_mix_g[1], ln_mix_b[1], TOK_TM)
    x, xb = ffn(x, x_lin, 1)

    y_prompt = jnp.stack([x[b * lp + N_META:(b + 1) * lp] for b in range(bp)])
    y_sample = x[tp:].reshape(bs, ss, d)
    return (y_prompt, y_sample, gla_p[None], gla_s[None], rwkv_p[None], rwkv_s[None],
            shift_p[None], shift_s[None], re_p[None], re_s[None], im_p[None], im_s[None])
```

```python
import functools
import math

import jax
import jax.numpy as jnp
from jax import lax
from jax.experimental import pallas as pl
from jax.experimental.pallas import tpu as pltpu

F32 = jnp.float32
BF16 = jnp.bfloat16

D_MODEL = 2048
DEPTH = 2
N_META = 16

A_WIDTH = 1024
A_HEADS = 4
A_DV = 256
A_DK = 128
A_GATE_RANK = 16
A_GATE_TAU = 16.0
A_QK = A_HEADS * A_DK
A_COLS = 2 * A_QK + 2 * A_WIDTH + A_GATE_RANK

B_WIDTH = 1024
B_HEAD = 64
B_HEADS = 16
B_DECAY_RANK = 64
B_AAA_RANK = 64
B_GATE_RANK = 128
B_COLS = 3 * B_WIDTH + B_DECAY_RANK + B_AAA_RANK + B_GATE_RANK

C_GROUP = 16
C_GROUPS = 128
C_STATE = 64

N_EXPERTS = 16
N_EXPERT_GROUPS = 4
EXPERTS_PER_GROUP = 4
TOP_K = 2
D_EXPERT = 1024

ALPHA = (2.0 * DEPTH) ** 0.25
LN_EPS = 1e-5
HEAD_NORM_EPS = 1e-5
RWKV_GN_EPS = 64e-5

P_B_OFF = B_COLS
P_COLS = 2 * B_COLS

VMEM_LIMIT = 56 * 1024 * 1024
MXU_TILE = 256
LANES = 128


def _cparams(sem):
    return pltpu.CompilerParams(dimension_semantics=sem, vmem_limit_bytes=VMEM_LIMIT)


def _cdiv(a, b):
    return (a + b - 1) // b


def _softplus(x):
    return jnp.maximum(x, 0.0) + jnp.log(1.0 + jnp.exp(-jnp.abs(x)))


def _sigmoid(x):
    return 1.0 / (1.0 + jnp.exp(-x))


def _mm_kernel(x_ref, w_ref, o_ref):
    o_ref[...] = jnp.dot(x_ref[...].astype(BF16), w_ref[...].astype(BF16),
                         preferred_element_type=F32)


def matmul(x, w, tm, tn):
    m, k = x.shape
    n = w.shape[1]
    assert n % tn == 0
    return pl.pallas_call(
        _mm_kernel,
        out_shape=jax.ShapeDtypeStruct((m, n), F32),
        grid=(n // tn, _cdiv(m, tm)),
        in_specs=[pl.BlockSpec((tm, k), lambda j, i: (i, 0)),
                  pl.BlockSpec((k, tn), lambda j, i: (0, j))],
        out_specs=pl.BlockSpec((tm, tn), lambda j, i: (i, j)),
        compiler_params=_cparams(("arbitrary", "arbitrary")),
        name="matmul",
    )(x, w)


def _layer_norm_rows(y, g, b):
    mu = jnp.mean(y, axis=-1, keepdims=True)
    yc = y - mu
    var = jnp.mean(yc * yc, axis=-1, keepdims=True)
    return yc * lax.rsqrt(var + LN_EPS) * g + b


def _proj_ln_kernel(n_lhs, *refs):
    lhs = refs[:n_lhs]
    w_ref, x_ref, g_ref, b_ref, o_ref, ol_ref = refs[n_lhs:]
    acc = None
    off = 0
    for r in lhs:
        kk = r.shape[1]
        part = jnp.dot(r[...], w_ref[off:off + kk, :], preferred_element_type=F32)
        acc = part if acc is None else acc + part
        off += kk
    y = _layer_norm_rows(ALPHA * x_ref[...] + acc, g_ref[...], b_ref[...])
    o_ref[...] = y
    tm, d = y.shape
    for s in range(d // LANES):
        ol_ref[pl.ds(s, tm, stride=d // LANES), :] = y[:, s * LANES:(s + 1) * LANES]


def proj_ln(lhs_list, w_bf16, x, g, b, tm):
    m, d = x.shape
    kin = w_bf16.shape[0]
    sub = d // LANES
    in_specs = [pl.BlockSpec((tm, l.shape[1]), lambda i: (i, 0)) for l in lhs_list]
    in_specs += [pl.BlockSpec((kin, d), lambda i: (0, 0)),
                 pl.BlockSpec((tm, d), lambda i: (i, 0)),
                 pl.BlockSpec((1, d), lambda i: (0, 0)),
                 pl.BlockSpec((1, d), lambda i: (0, 0))]
    return pl.pallas_call(
        functools.partial(_proj_ln_kernel, len(lhs_list)),
        out_shape=(jax.ShapeDtypeStruct((m, d), F32), jax.ShapeDtypeStruct((m * sub, LANES), F32)),
        grid=(_cdiv(m, tm),),
        in_specs=in_specs,
        out_specs=(pl.BlockSpec((tm, d), lambda i: (i, 0)),
                   pl.BlockSpec((tm * sub, LANES), lambda i: (i, 0))),
        compiler_params=_cparams(("arbitrary",)),
        name="proj_ln",
    )(*lhs_list, w_bf16, x, g.reshape(1, d), b.reshape(1, d))


def _add_ln_kernel(x_ref, f0_ref, f1_ref, g_ref, b_ref, o_ref, ob_ref):
    y = _layer_norm_rows(ALPHA * x_ref[...] + (f0_ref[...] + f1_ref[...]), g_ref[...], b_ref[...])
    o_ref[...] = y
    ob_ref[...] = y.astype(BF16)


def add_ln(x, f0, f1, g, b, tm):
    m, d = x.shape
    row = pl.BlockSpec((tm, d), lambda i: (i, 0))
    vec = pl.BlockSpec((1, d), lambda i: (0, 0))
    return pl.pallas_call(
        _add_ln_kernel,
        out_shape=(jax.ShapeDtypeStruct((m, d), F32), jax.ShapeDtypeStruct((m, d), BF16)),
        grid=(_cdiv(m, tm),),
        in_specs=[row, row, row, vec, vec],
        out_specs=(row, row),
        compiler_params=_cparams(("arbitrary",)),
        name="add_ln",
    )(x, f0, f1, g.reshape(1, d), b.reshape(1, d))


def _gla_kernel(nb, lb, chunk, *refs):
    p_refs = refs[:nb]
    (s0_ref, gup_ref, gb_ref, ng_ref) = refs[nb:nb + 4]
    o_ref = refs[nb + 4]
    sout_ref = refs[nb + 5]
    s_ref = refs[nb + 6]
    blk = pl.program_id(1)

    @pl.when(blk == 0)
    def _():
        s_ref[...] = s0_ref[...]

    rows = lax.broadcasted_iota(jnp.int32, (chunk, chunk), 0)
    cols = lax.broadcasted_iota(jnp.int32, (chunk, chunk), 1)
    tril = rows >= cols
    tril_f = tril.astype(F32)
    lgs = []
    for j in range(nb):
        gd = p_refs[j][:, 2 * A_QK + 2 * A_WIDTH:2 * A_QK + 2 * A_WIDTH + A_GATE_RANK]
        z = jnp.dot(gd, gup_ref[...], preferred_element_type=F32) + gb_ref[...]
        lgs.append(-_softplus(-z) * (1.0 / A_GATE_TAU))
    for c in range(lb // chunk):
        r0 = c * chunk
        for j in range(nb):
            p_ref = p_refs[j]
            bc = jnp.dot(tril_f, lgs[j][r0:r0 + chunk, :], preferred_element_type=F32,
                         precision=lax.Precision.HIGHEST)
            for h in range(A_HEADS):
                q = p_ref[r0:r0 + chunk, h * A_DK:(h + 1) * A_DK] * (A_DK ** -0.5)
                k = p_ref[r0:r0 + chunk, A_QK + h * A_DK:A_QK + (h + 1) * A_DK]
                v = p_ref[r0:r0 + chunk, 2 * A_QK + h * A_DV:2 * A_QK + (h + 1) * A_DV]
                rg = p_ref[r0:r0 + chunk,
                           2 * A_QK + A_WIDTH + h * A_DV:2 * A_QK + A_WIDTH + (h + 1) * A_DV]
                b = bc[:, h * A_DK:(h + 1) * A_DK]
                bl = b[chunk - 1:chunk, :]
                qd = (q * jnp.exp(b)).astype(BF16)
                kd = (k * jnp.exp(-b)).astype(BF16)
                vb = v.astype(BF16)
                att = lax.dot_general(qd, kd, (((1,), (1,)), ((), ())), preferred_element_type=F32)
                att = jnp.where(tril, att, 0.0).astype(BF16)
                s = s_ref[j, h]
                o = (jnp.dot(att, vb, preferred_element_type=F32)
                     + jnp.dot(qd, s.astype(BF16), preferred_element_type=F32))
                kl = (k * jnp.exp(bl - b)).astype(BF16)
                kv = lax.dot_general(kl, vb, (((0,), (0,)), ((), ())), preferred_element_type=F32)
                dec = jnp.broadcast_to(jnp.exp(bl), (A_DK, A_DK)).T
                s_ref[j, h] = s * jnp.concatenate([dec] * (A_DV // A_DK), axis=1) + kv
                o = o * lax.rsqrt(jnp.mean(o * o, axis=-1, keepdims=True) + HEAD_NORM_EPS)
                o = o * ng_ref[:, h * A_DV:(h + 1) * A_DV] * (rg * _sigmoid(rg))
                o_ref[j, r0:r0 + chunk, h * A_DV:(h + 1) * A_DV] = o.astype(BF16)

    @pl.when(blk == pl.num_programs(1) - 1)
    def _():
        sout_ref[...] = s_ref[...]


def gla_group(p, s0, gate_up, gate_b, norm_g, *, row_off, nbatch, seq, nb, lb, chunk):
    t = p.shape[0]
    nblk = seq // lb
    assert seq % lb == 0 and lb % chunk == 0 and nbatch % nb == 0 and row_off % lb == 0
    base = row_off // lb

    def p_map(j):
        return lambda bi, blk: (base + (bi * nb + j) * nblk + blk, 0)

    in_specs = [pl.BlockSpec((lb, P_B_OFF), p_map(j)) for j in range(nb)]
    in_specs += [pl.BlockSpec((nb, A_HEADS, A_DK, A_DV), lambda bi, blk: (bi, 0, 0, 0)),
                 pl.BlockSpec((A_GATE_RANK, A_QK), lambda bi, blk: (0, 0)),
                 pl.BlockSpec((1, A_QK), lambda bi, blk: (0, 0)),
                 pl.BlockSpec((1, A_WIDTH), lambda bi, blk: (0, 0))]

    out_specs = [pl.BlockSpec((nb, lb, A_WIDTH), lambda bi, blk: (bi, blk, 0)),
                 pl.BlockSpec((nb, A_HEADS, A_DK, A_DV), lambda bi, blk: (bi, 0, 0, 0))]
    out_shape = [jax.ShapeDtypeStruct((nbatch, seq, A_WIDTH), BF16),
                 jax.ShapeDtypeStruct((nbatch, A_HEADS, A_DK, A_DV), F32)]
    o, s_new = pl.pallas_call(
        functools.partial(_gla_kernel, nb, lb, chunk),
        out_shape=out_shape,
        grid=(nbatch // nb, nblk),
        in_specs=in_specs,
        out_specs=out_specs,
        scratch_shapes=[pltpu.VMEM((nb, A_HEADS, A_DK, A_DV), F32)],
        compiler_params=_cparams(("arbitrary", "arbitrary")),
        name="gla",
    )(*([p] * nb), s0, gate_up, gate_b.reshape(1, A_QK), norm_g.reshape(1, A_WIDTH))
    return o.reshape(nbatch * seq, A_WIDTH), s_new


VT_SLOT = 128
RWKV_UNROLL = 8


def _head_ones(dtype):
    r = lax.broadcasted_iota(jnp.int32, (MXU_TILE, MXU_TILE), 0) // B_HEAD
    c = lax.broadcasted_iota(jnp.int32, (MXU_TILE, MXU_TILE), 1) // B_HEAD
    return (r == c).astype(dtype)


def _head_sum(x, ones):
    xh = x.astype(BF16)
    xl = (x - xh.astype(F32)).astype(BF16)
    parts = [jnp.dot(xh[:, c * MXU_TILE:(c + 1) * MXU_TILE], ones, preferred_element_type=F32)
             + jnp.dot(xl[:, c * MXU_TILE:(c + 1) * MXU_TILE], ones, preferred_element_type=F32)
             for c in range(B_WIDTH // MXU_TILE)]
    return jnp.concatenate(parts, axis=-1)


def _rwkv_kernel(nb, lb, *refs):
    p_refs = refs[:nb]
    (shift0_ref, s0_ref, mu_ref, w0_ref, wup_ref, a0_ref, aup_ref, gup_ref,
     kk_ref, ka_ref, rk_ref, lng_ref, lnb_ref) = refs[nb:nb + 13]
    o_ref, sout_ref = refs[nb + 13:nb + 15]
    (s_ref, prev_ref, r_s, w_s, kh_s, kl_s, kk_s, kka_s, on_s, g_s, bon_s, vmh_s, vml_s, vt_s
     ) = refs[nb + 15:]
    blk = pl.program_id(1)
    W = B_WIDTH
    H = B_HEADS
    NT = (((1,), (1,)), ((), ()))

    @pl.when(blk == 0)
    def _():
        s_ref[...] = s0_ref[...]
        prev_ref[...] = shift0_ref[...]

    ones = _head_ones(BF16)
    hmask = (lax.broadcasted_iota(jnp.int32, (H, W), 1) // B_HEAD
             == lax.broadcasted_iota(jnp.int32, (H, W), 0))

    def to_head_rows(ref, j, x):
        ref[j] = jnp.zeros(ref.shape[1:], F32)
        for h in range(H):
            ref[j, pl.ds(h, lb, stride=H), 0:B_HEAD] = x[:, h * B_HEAD:(h + 1) * B_HEAD]

    def split(x):
        xh = x.astype(BF16).astype(F32)
        return xh, x - xh

    xms = []
    for j in range(nb):
        pb = p_refs[j][...]
        first = lax.broadcasted_iota(jnp.int32, (lb, 1), 0) == 0
        prev = jnp.where(first, prev_ref[j], pltpu.roll(pb, 1, axis=0))
        prev_ref[j] = pb[lb - 1:lb, :]
        xms.append(pb + (prev - pb) * mu_ref[...])
    xm = jnp.concatenate(xms, axis=0)
    r = xm[:, :W]
    k = xm[:, W:2 * W]
    v = xm[:, 2 * W:3 * W]
    wd = xm[:, 3 * W:3 * W + B_DECAY_RANK]
    ad = xm[:, 3 * W + B_DECAY_RANK:3 * W + B_DECAY_RANK + B_AAA_RANK]
    gd = xm[:, 3 * W + B_DECAY_RANK + B_AAA_RANK:]
    w = -_softplus(-(w0_ref[...] + jnp.dot(jnp.tanh(wd).astype(BF16), wup_ref[...].astype(BF16),
                                           preferred_element_type=F32))) - 0.5
    a = _sigmoid(a0_ref[...] + jnp.dot(ad.astype(BF16), aup_ref[...].astype(BF16),
                                       preferred_element_type=F32))
    g = jnp.dot(_sigmoid(gd).astype(BF16), gup_ref[...].astype(BF16), preferred_element_type=F32)
    kk = k * kk_ref[...]
    nrm = jnp.sqrt(_head_sum(kk * kk, ones))
    kk = kk / jnp.maximum(nrm, 1e-12)
    k2 = k * (1.0 + (a - 1.0) * ka_ref[...])
    per_seq = lambda x: x.reshape(nb, lb, x.shape[-1])
    r_s[...] = per_seq(r)
    w_s[...] = per_seq(jnp.exp(-jnp.exp(w)))
    kh, kl = split(k2)
    kh_s[...], kl_s[...] = per_seq(kh), per_seq(kl)
    kk_s[...] = per_seq(kk)
    kka_s[...] = per_seq(-(kk * a))
    vh, vl = split(v)
    for j in range(nb):
        to_head_rows(vmh_s, j, vh[j * lb:(j + 1) * lb])
        to_head_rows(vml_s, j, vl[j * lb:(j + 1) * lb])
    g_s[...] = per_seq(g)
    bon_s[...] = per_seq(_head_sum(r * k2 * rk_ref[...], ones) * v)

    def head_rows(ref, j, t):
        return jnp.where(hmask, jnp.broadcast_to(ref[j, pl.ds(t, 1), :], (H, W)), 0.0).astype(BF16)

    def readout(j, t, sb):
        rows = pl.ds(t * H if isinstance(t, int) else pl.multiple_of(t * H, H), H)
        on_s[j, rows, :] = lax.dot_general(head_rows(r_s, j, t), sb, NT, preferred_element_type=F32)

    zrows = lambda n: jnp.zeros((n * H, W), BF16)
    lane = lax.broadcasted_iota(jnp.int32, (B_HEAD, VT_SLOT), 1)

    def vt_step(t, carry):
        rows = pl.ds(pl.multiple_of(t * H, H), H)
        for j in range(nb):
            vmh, vml = vmh_s[j, rows, :], vml_s[j, rows, :]
            tile = jnp.concatenate([vmh, vml, vmh, jnp.zeros((VT_SLOT - 3 * H, VT_SLOT), F32)], axis=0)
            vt_s[j, t] = tile.T[:B_HEAD, :].astype(BF16)
        return carry

    lax.fori_loop(0, lb, vt_step, 0, unroll=RWKV_UNROLL)

    def step(t, carry):
        sbs, sas = [], []
        for j in range(nb):
            sb = s_ref[j].astype(BF16)
            kkw = head_rows(kk_s, j, t)
            sas.append(lax.dot_general(
                sb, jnp.concatenate([zrows(3), kkw, kkw, zrows(VT_SLOT // H - 5)], axis=0),
                NT, preferred_element_type=F32))
            sbs.append(sb)
        for j in range(nb):
            readout(j, jnp.maximum(t - 1, 0), sbs[j])
        for j in range(nb):
            sa = sas[j]
            sa_hi = sa.astype(BF16)
            sa_lo = (sa - sa_hi.astype(F32)).astype(BF16)
            x = jnp.where(lane < 3 * H, vt_s[j, t], jnp.where(lane < 4 * H, sa_hi, sa_lo))
            khw, klw, kaw = head_rows(kh_s, j, t), head_rows(kl_s, j, t), head_rows(kka_s, j, t)
            upd = jnp.dot(x[:, :5 * H], jnp.concatenate([khw, khw, klw, kaw, kaw], axis=0),
                          preferred_element_type=F32)
            s_ref[j] = s_ref[j] * w_s[j, pl.ds(t, 1), :] + upd
        return carry

    lax.fori_loop(0, lb, step, 0, unroll=RWKV_UNROLL)

    os = []
    for j in range(nb):
        readout(j, lb - 1, s_ref[j].astype(BF16))
        os.append(jnp.concatenate([on_s[j, pl.ds(h, lb, stride=H), :] for h in range(H)], axis=-1))
    o = jnp.concatenate(os, axis=0)
    oc = o - _head_sum(o, ones) * (1.0 / B_HEAD)
    var = _head_sum(oc * oc, ones) * (1.0 / B_HEAD)
    o = per_seq(oc * lax.rsqrt(var + RWKV_GN_EPS) * lng_ref[...] + lnb_ref[...])
    o_ref[...] = ((o + bon_s[...]) * g_s[...]).astype(BF16)

    @pl.when(blk == pl.num_programs(1) - 1)
    def _():
        sout_ref[...] = s_ref[...]


def rwkv_group(p, s0, shift0, mu, w0, w_up, a0, a_up, g_up, k_k, k_a, r_k, ln_g, ln_b,
               *, row_off, nbatch, seq, nb, lb):
    nblk = seq // lb
    assert seq % lb == 0 and nbatch % nb == 0 and row_off % lb == 0
    base = row_off // lb
    W = B_WIDTH
    s0 = s0.transpose(0, 2, 1, 3).reshape(nbatch, B_HEAD, W)

    def p_map(j):
        return lambda bi, blk: (base + (bi * nb + j) * nblk + blk, 1)

    const = lambda shape: pl.BlockSpec(shape, lambda bi, blk: (0,) * len(shape))
    in_specs = [pl.BlockSpec((lb, B_COLS), p_map(j)) for j in range(nb)]
    in_specs += [pl.BlockSpec((nb, 1, B_COLS), lambda bi, blk: (bi, 0, 0)),
                 pl.BlockSpec((nb, B_HEAD, W), lambda bi, blk: (bi, 0, 0)),
                 const((1, B_COLS)), const((1, W)), const((B_DECAY_RANK, W)), const((1, W)),
                 const((B_AAA_RANK, W)), const((B_GATE_RANK, W)),
                 const((1, W)), const((1, W)), const((1, W)), const((1, W)), const((1, W))]
    out_specs = [pl.BlockSpec((nb, lb, W), lambda bi, blk: (bi, blk, 0)),
                 pl.BlockSpec((nb, B_HEAD, W), lambda bi, blk: (bi, 0, 0))]
    out_shape = [jax.ShapeDtypeStruct((nbatch, seq, W), BF16),
                 jax.ShapeDtypeStruct((nbatch, B_HEAD, W), F32)]
    tok = pltpu.VMEM((nb, lb, W), F32)
    o, s_new = pl.pallas_call(
        functools.partial(_rwkv_kernel, nb, lb),
        out_shape=out_shape,
        grid=(nbatch // nb, nblk),
        in_specs=in_specs,
        out_specs=out_specs,
        scratch_shapes=[pltpu.VMEM((nb, B_HEAD, W), F32), pltpu.VMEM((nb, 1, B_COLS), F32)]
        + [tok] * 6 + [pltpu.VMEM((nb, lb * B_HEADS, B_HEAD), F32), tok, tok,
                       pltpu.VMEM((nb, lb * B_HEADS, VT_SLOT), F32),
                       pltpu.VMEM((nb, lb * B_HEADS, VT_SLOT), F32),
                       pltpu.VMEM((nb, lb, B_HEAD, VT_SLOT), BF16)],
        compiler_params=_cparams(("arbitrary", "arbitrary")),
        name="rwkv",
    )(*([p] * nb), shift0.reshape(nbatch, 1, B_COLS), s0, mu.reshape(1, B_COLS),
      w0.reshape(1, W), w_up, a0.reshape(1, W), a_up, g_up, k_k.reshape(1, W),
      k_a.reshape(1, W), r_k.reshape(1, W), ln_g.reshape(1, W), ln_b.reshape(1, W))
    s_new = s_new.reshape(nbatch, B_HEAD, B_HEADS, B_HEAD).transpose(0, 2, 1, 3)
    return o.reshape(nbatch * seq, W), s_new


S5_GT = LANES // C_GROUP
SP = C_GROUPS * C_STATE


def s5_tables(a_re, a_im, log_dt, b_re, b_im, c_re, c_im, d, chunk):
    G, P, c = C_GROUPS, C_STATE, C_GROUP
    dt = jnp.exp(log_dt)[:, None]
    m = jnp.arange(chunk + 1, dtype=F32)[:, None, None]
    mag = jnp.exp(m * (dt * a_re))
    ang = m * (dt * a_im)
    pw_re, pw_im = mag * jnp.cos(ang), mag * jnp.sin(ang)
    num_re, num_im = pw_re[1] - 1.0, pw_im[1]
    den = a_re * a_re + a_im * a_im
    q_re = (num_re * a_re + num_im * a_im) / den
    q_im = (num_im * a_re - num_re * a_im) / den
    bb_re = q_re[..., None] * b_re - q_im[..., None] * b_im
    bb_im = q_re[..., None] * b_im + q_im[..., None] * b_re
    ca_re = c_re[None] * pw_re[:, :, None, :] - c_im[None] * pw_im[:, :, None, :]
    ca_im = c_re[None] * pw_im[:, :, None, :] + c_im[None] * pw_re[:, :, None, :]
    cm = lambda z: z[:chunk].transpose(1, 3, 0, 2).reshape(G, P, chunk * c)
    toep = s5_toeplitz(bb_re.transpose(0, 2, 1), bb_im.transpose(0, 2, 1), cm(ca_re), cm(ca_im), chunk)
    dvec = jnp.tile(d.reshape(G, 1, c), (1, chunk, 1)).reshape(G, 1, chunk * c)
    rev_re, rev_im = pw_re[:chunk][::-1], pw_im[:chunk][::-1]
    bs_re = rev_re[..., None] * bb_re[None] - rev_im[..., None] * bb_im[None]
    bs_im = rev_re[..., None] * bb_im[None] + rev_im[..., None] * bb_re[None]
    to_in = lambda z: z.transpose(1, 0, 3, 2).reshape(G, chunk * c, P)
    to_out = lambda z: z.transpose(1, 3, 0, 2).reshape(G, P, chunk * c)
    cs_re, cs_im = to_out(ca_re[1:]), to_out(-ca_im[1:])
    return dict(toep=toep, dvec=dvec,
                bs_re=to_in(bs_re).astype(BF16), bs_im=to_in(bs_im).astype(BF16),
                cs_re=cs_re.astype(BF16), cs_im=cs_im.astype(BF16),
                ac_re=pw_re[chunk].reshape(1, SP), ac_im=pw_im[chunk].reshape(1, SP))


def _s5_toeplitz_kernel(chunk, bre_ref, bim_ref, cre_ref, cim_ref, o_ref):
    c = C_GROUP
    cr = chunk * c
    hi = lax.Precision.HIGHEST
    lane = lax.broadcasted_iota(jnp.int32, (c, cr), 1)
    for q in range(S5_GT):
        k0 = (jnp.dot(bre_ref[q], cre_ref[q], preferred_element_type=F32, precision=hi)
              - jnp.dot(bim_ref[q], cim_ref[q], preferred_element_type=F32, precision=hi))
        for j in range(chunk):
            blk = k0 if j == 0 else jnp.where(lane >= j * c, pltpu.roll(k0, j * c, axis=1), 0.0)
            o_ref[q, j * c:(j + 1) * c, :] = blk.astype(BF16)


def s5_toeplitz(bbt_re, bbt_im, cm_re, cm_im, chunk):
    G, P, c = C_GROUPS, C_STATE, C_GROUP
    cr = chunk * c
    grp = lambda shape: pl.BlockSpec((S5_GT,) + shape, lambda g: (g, 0, 0))
    return pl.pallas_call(
        functools.partial(_s5_toeplitz_kernel, chunk),
        out_shape=jax.ShapeDtypeStruct((G, cr, cr), BF16),
        grid=(G // S5_GT,),
        in_specs=[grp((c, P)), grp((c, P)), grp((P, cr)), grp((P, cr))],
        out_specs=grp((cr, cr)),
        compiler_params=_cparams(("arbitrary",)),
        name="s5_toeplitz",
    )(bbt_re, bbt_im, cm_re, cm_im)


def _swap_blocks(xs, blk):
    n = len(xs)
    w = xs[0].shape[-1] // n
    s = n // 2
    while s >= 1:
        nxt = list(xs)
        for i in range(n):
            if i & s == 0:
                j = i + s
                low = (blk & s) == 0
                nxt[i] = jnp.where(low, xs[i], pltpu.roll(xs[j], s * w, axis=1))
                nxt[j] = jnp.where(low, pltpu.roll(xs[i], (n - s) * w, axis=1), xs[j])
        xs = nxt
        s //= 2
    return xs


def _s5_kernel(nbatch, nchunks, chunk, u_ref, toep_ref, bre_ref, bim_ref, cre_ref, cim_ref, dvec_ref,
               acre_ref, acim_ref, s0re_ref, s0im_ref, y_ref, fre_ref, fim_ref,
               uf_s, vre_s, vim_s, sre_s, sim_s):
    c, P, GT = C_GROUP, C_STATE, S5_GT
    R = nbatch * nchunks
    cr = chunk * c
    halves = cr // LANES
    per_half = LANES // c
    blk = lax.broadcasted_iota(jnp.int32, (R, LANES), 1) // c

    for h in range(halves):
        xs = _swap_blocks([u_ref[pl.ds(h * per_half + q, R, stride=chunk), :] for q in range(per_half)], blk)
        for g in range(GT):
            uf_s[g, :, h * LANES:(h + 1) * LANES] = xs[g]

    for g in range(GT):
        ub = uf_s[g].astype(BF16)
        vre_s[:, g * P:(g + 1) * P] = jnp.dot(ub, bre_ref[g], preferred_element_type=F32)
        vim_s[:, g * P:(g + 1) * P] = jnp.dot(ub, bim_ref[g], preferred_element_type=F32)

    ar, ai = acre_ref[...], acim_ref[...]
    if nchunks == 1:
        sre_s[...] = s0re_ref[...]
        sim_s[...] = s0im_ref[...]
        sr, si = s0re_ref[...], s0im_ref[...]
        fre_ref[...] = ar * sr - ai * si + vre_s[...]
        fim_ref[...] = ar * si + ai * sr + vim_s[...]
    else:
        def step(ci, carry):
            sr, si = carry
            vr = jnp.concatenate([vre_s[pl.ds(b * nchunks + ci, 1), :] for b in range(nbatch)], axis=0)
            vi = jnp.concatenate([vim_s[pl.ds(b * nchunks + ci, 1), :] for b in range(nbatch)], axis=0)
            for b in range(nbatch):
                sre_s[pl.ds(b * nchunks + ci, 1), :] = sr[b:b + 1]
                sim_s[pl.ds(b * nchunks + ci, 1), :] = si[b:b + 1]
            return ar * sr - ai * si + vr, ar * si + ai * sr + vi

        sr, si = lax.fori_loop(0, nchunks, step, (s0re_ref[...], s0im_ref[...]))
        fre_ref[...] = sr
        fim_ref[...] = si

    for g in range(GT):
        u = uf_s[g]
        y = (jnp.dot(u.astype(BF16), toep_ref[g], preferred_element_type=F32)
             + jnp.dot(sre_s[:, g * P:(g + 1) * P].astype(BF16), cre_ref[g], preferred_element_type=F32)
             + jnp.dot(sim_s[:, g * P:(g + 1) * P].astype(BF16), cim_ref[g], preferred_element_type=F32))
        uf_s[g] = y + dvec_ref[g] * u
    for h in range(halves):
        ys = _swap_blocks([uf_s[g, :, h * LANES:(h + 1) * LANES] for g in range(GT)], blk)
        for q in range(per_half):
            y_ref[pl.ds(h * per_half + q, R, stride=chunk), :] = ys[q]


def s5_group(u, s0_re, s0_im, tab, *, nbatch, seq, chunk):
    G, P, c, GT = C_GROUPS, C_STATE, C_GROUP, S5_GT
    nchunks = seq // chunk
    assert seq % chunk == 0 and u.shape[0] >= nbatch * seq and u.shape[1] == G * c
    R = nbatch * nchunks
    cr = chunk * c
    rows = nbatch * seq
    grp = lambda shape: pl.BlockSpec((GT,) + shape, lambda i: (i, 0, 0))
    lanes = lambda r, w: pl.BlockSpec((r, w), lambda i: (0, i))
    y, f_re, f_im = pl.pallas_call(
        functools.partial(_s5_kernel, nbatch, nchunks, chunk),
        out_shape=[jax.ShapeDtypeStruct((rows, G * c), F32)] + [jax.ShapeDtypeStruct((nbatch, SP), F32)] * 2,
        grid=(G // GT,),
        in_specs=[lanes(rows, GT * c), grp((cr, cr)), grp((cr, P)), grp((cr, P)), grp((P, cr)), grp((P, cr)),
                  grp((1, cr)), lanes(1, GT * P), lanes(1, GT * P), lanes(nbatch, GT * P), lanes(nbatch, GT * P)],
        out_specs=[lanes(rows, GT * c), lanes(nbatch, GT * P), lanes(nbatch, GT * P)],
        scratch_shapes=[pltpu.VMEM((GT, R, cr), F32)] + [pltpu.VMEM((R, GT * P), F32)] * 4,
        compiler_params=_cparams(("arbitrary",)),
        name="s5",
    )(u, tab['toep'], tab['bs_re'], tab['bs_im'], tab['cs_re'], tab['cs_im'], tab['dvec'],
      tab['ac_re'], tab['ac_im'], s0_re.reshape(nbatch, SP), s0_im.reshape(nbatch, SP))
    return y, f_re.reshape(nbatch, G, P), f_im.reshape(nbatch, G, P)


def _glu_kernel(y_ref, w_ref, b_ref, o_ref):
    z = jax.nn.gelu(y_ref[...])
    acc = jnp.dot(z.astype(BF16), w_ref[...], preferred_element_type=F32) + b_ref[...]
    o_ref[...] = (z * _sigmoid(acc)).astype(BF16)


def gelu_glu(y, w_bf16, b, tm):
    m, k = y.shape
    n = w_bf16.shape[1]
    assert n == k
    return pl.pallas_call(
        _glu_kernel,
        out_shape=jax.ShapeDtypeStruct((m, n), BF16),
        grid=(_cdiv(m, tm),),
        in_specs=[pl.BlockSpec((tm, k), lambda i: (i, 0)),
                  pl.BlockSpec((k, n), lambda i: (0, 0)),
                  pl.BlockSpec((1, n), lambda i: (0, 0))],
        out_specs=pl.BlockSpec((tm, n), lambda i: (i, 0)),
        compiler_params=_cparams(("arbitrary",)),
        name="gelu_glu",
    )(y, w_bf16, b.reshape(1, n))


MOE_TM = 512
MOE_TF = 1024
MOE_TN = 1024
ROW_GROUP = 8


def _router_kernel(x_ref, wr_ref, e_ref, g_ref):
    logits = lax.dot_general(wr_ref[...], x_ref[...], (((1,), (1,)), ((), ())),
                             preferred_element_type=F32, precision=lax.Precision.HIGHEST)
    mx = jnp.max(logits, axis=0, keepdims=True)
    ex = jnp.exp(logits - mx)
    probs = ex / jnp.sum(ex, axis=0, keepdims=True)
    neg = jnp.float32(-jnp.inf)
    best = None
    for gi in range(N_EXPERT_GROUPS):
        v = [probs[gi * EXPERTS_PER_GROUP + r:gi * EXPERTS_PER_GROUP + r + 1, :]
             for r in range(EXPERTS_PER_GROUP)]
        m1 = jnp.maximum(jnp.maximum(v[0], v[1]), jnp.maximum(v[2], v[3]))
        i1 = jnp.where(v[0] == m1, 0, jnp.where(v[1] == m1, 1, jnp.where(v[2] == m1, 2, 3)))
        w = [jnp.where(i1 == r, neg, v[r]) for r in range(EXPERTS_PER_GROUP)]
        m2 = jnp.maximum(jnp.maximum(w[0], w[1]), jnp.maximum(w[2], w[3]))
        i2 = jnp.where(w[0] == m2, 0, jnp.where(w[1] == m2, 1, jnp.where(w[2] == m2, 2, 3)))
        score = m1 + m2
        cand = (score, m1, m2, i1 + gi * EXPERTS_PER_GROUP, i2 + gi * EXPERTS_PER_GROUP)
        if best is None:
            best = cand
        else:
            take = cand[0] > best[0]
            best = tuple(jnp.where(take, cn, bs) for cn, bs in zip(cand, best))
    _, m1, m2, e1, e2 = best
    tot = m1 + m2
    e_ref[...] = jnp.concatenate([e1, e2], axis=0)
    g_ref[...] = jnp.concatenate([m1 / tot, m2 / tot], axis=0)


def router(x, w_router, tm):
    t, d = x.shape
    return pl.pallas_call(
        _router_kernel,
        out_shape=[jax.ShapeDtypeStruct((TOP_K, t), jnp.int32), jax.ShapeDtypeStruct((TOP_K, t), F32)],
        grid=(_cdiv(t, tm),),
        in_specs=[pl.BlockSpec((tm, d), lambda i: (i, 0)),
                  pl.BlockSpec((N_EXPERTS, d), lambda i: (0, 0))],
        out_specs=[pl.BlockSpec((TOP_K, tm), lambda i: (0, i)),
                   pl.BlockSpec((TOP_K, tm), lambda i: (0, i))],
        compiler_params=_cparams(("arbitrary",)),
        name="router",
    )(x, w_router.T)


def _moe_up_kernel(rs_ref, tr_ref, st_ref, sf_ref, se_ref, nu_ref, x_hbm, w1_ref, w2_ref, gate_ref, h_ref,
                   xbuf0, xbuf1, sem):
    s = pl.program_id(0)
    n = pl.num_programs(0)
    sub = D_MODEL // LANES
    tm = xbuf0.shape[0] // sub
    bufs = (xbuf0, xbuf1)

    def fetched(step):
        rows = tr_ref[st_ref[jnp.minimum(step, n - 1)]]
        return jnp.where(step < n, (rows + ROW_GROUP - 1) // ROW_GROUP * ROW_GROUP, 0)

    def row_copy(step, slot, r):
        src = rs_ref[st_ref[step] * tm + r]
        return pltpu.make_async_copy(x_hbm.at[pl.ds(src * sub, sub), :],
                                     bufs[slot].at[pl.ds(r * sub, sub), :], sem.at[slot])

    def for_rows(step, slot, count, fn):
        def body(i, carry):
            for q in range(ROW_GROUP):
                fn(row_copy(step, slot, i * ROW_GROUP + q))
            return carry
        lax.fori_loop(0, count // ROW_GROUP, body, 0)

    def tile(slot):
        return jnp.concatenate([bufs[slot][pl.ds(c, tm, stride=sub), :] for c in range(sub)], axis=-1)

    def up(slot):
        x = tile(slot).astype(BF16)
        a = jnp.dot(x, w1_ref[0, 0].astype(BF16), preferred_element_type=F32)
        b = jnp.dot(x, w2_ref[0, 0].astype(BF16), preferred_element_type=F32)
        own = lax.broadcasted_iota(jnp.int32, (tm, 1), 0) < tr_ref[st_ref[s]]
        h_ref[...] = jnp.where(own, a * _sigmoid(a) * b * gate_ref[...], 0.0).astype(BF16)

    @pl.when(s == 0)
    def _():
        xbuf0[...] = jnp.zeros(xbuf0.shape, F32)
        xbuf1[...] = jnp.zeros(xbuf1.shape, F32)
        for_rows(0, 0, fetched(0), lambda cp: cp.start())

    def step(slot):
        for_rows(s, slot, fetched(s), lambda cp: cp.wait())
        nxt_rows = fetched(s + 1)

        @pl.when(nxt_rows == tm)
        def _():
            for r in range(tm):
                row_copy(s + 1, 1 - slot, r).start(priority=r % 2)
            up(slot)

        @pl.when(nxt_rows < tm)
        def _():
            for_rows(jnp.minimum(s + 1, n - 1), 1 - slot, nxt_rows, lambda cp: cp.start())
            pl.when(s < nu_ref[0])(functools.partial(up, slot))

            @pl.when(s >= nu_ref[0])
            def _():
                h_ref[...] = jnp.zeros(h_ref.shape, h_ref.dtype)

    for slot in range(2):
        pl.when(s % 2 == slot)(functools.partial(step, slot))


def _moe_down_kernel(st_ref, sf_ref, se_ref, nu_ref, h_ref, w_ref, y_ref):
    @pl.when(pl.program_id(0) < nu_ref[0])
    def _():
        y_ref[...] = jnp.dot(h_ref[...], w_ref[0, 0].astype(BF16), preferred_element_type=F32)

    @pl.when(pl.program_id(0) >= nu_ref[0])
    def _():
        y_ref[...] = jnp.zeros(y_ref.shape, y_ref.dtype)


def _moe_schedule(tiles_e, tile_start, nblocks, n_tiles):
    steps_e = tiles_e * nblocks
    cum = jnp.cumsum(steps_e)
    used = cum[-1]
    s = jnp.arange(n_tiles * nblocks, dtype=jnp.int32)
    sc = jnp.minimum(s, used - 1)
    e = jnp.minimum(jnp.sum((cum[None, :] <= sc[:, None]).astype(jnp.int32), axis=1), N_EXPERTS - 1)
    local = sc - (cum[e] - steps_e[e])
    te = jnp.maximum(tiles_e[e], 1)
    rest = s - used
    tile = jnp.where(s < used, tile_start[e] + local % te, jnp.sum(tiles_e) + rest // nblocks)
    blk = jnp.where(s < used, local // te, rest % nblocks)
    return tile.astype(jnp.int32), blk.astype(jnp.int32), e, used.reshape(1).astype(jnp.int32)


def moe_layer(x, x_lin, w_router, w_up, w_down, layer):
    t, d = x.shape
    tm = MOE_TM
    eid, gates = router(x, w_router, 1024)
    eid = eid.reshape(-1)
    na = TOP_K * t
    n_tiles = _cdiv(na + N_EXPERTS * (tm - 1), tm)
    npad = n_tiles * tm
    onehot = (eid[None, :] == jnp.arange(N_EXPERTS, dtype=jnp.int32)[:, None]).astype(jnp.int32)
    csum = jnp.cumsum(onehot, axis=1)
    counts = csum[:, -1]
    rank = jnp.sum(csum * onehot, axis=0) - 1
    tiles_e = (counts + tm - 1) // tm
    tile_start = jnp.cumsum(tiles_e) - tiles_e
    pos = tile_start[eid] * tm + rank
    row_a = jnp.full((npad,), -1, jnp.int32).at[pos].set(
        jnp.arange(na, dtype=jnp.int32), mode="promise_in_bounds", unique_indices=True)
    a_c = jnp.maximum(row_a, 0)
    row_src = jnp.where(a_c >= t, a_c - t, a_c)
    row_gate = jnp.where(row_a >= 0, gates.reshape(-1).at[a_c].get(mode="promise_in_bounds"), 0.0)

    ti = jnp.arange(n_tiles, dtype=jnp.int32)
    te = jnp.minimum(jnp.sum((jnp.cumsum(tiles_e)[None, :] <= ti[:, None]).astype(jnp.int32), axis=1),
                     N_EXPERTS - 1)
    tile_rows = jnp.where(ti < jnp.sum(tiles_e),
                          jnp.clip(counts[te] - (ti - tile_start[te]) * tm, 0, tm), 0).astype(jnp.int32)

    nf = D_EXPERT // MOE_TF
    st, sf, se, nu = _moe_schedule(tiles_e, tile_start, nf, n_tiles)
    h = pl.pallas_call(
        _moe_up_kernel,
        out_shape=jax.ShapeDtypeStruct((npad, D_EXPERT), BF16),
        grid_spec=pltpu.PrefetchScalarGridSpec(
            num_scalar_prefetch=6,
            grid=(n_tiles * nf,),
            in_specs=[pl.BlockSpec(memory_space=pl.ANY),
                      pl.BlockSpec((1, 1, d, MOE_TF),
                                   lambda s, rs, tr, st, sf, se, nu: (layer, se[s], 0, sf[s])),
                      pl.BlockSpec((1, 1, d, MOE_TF),
                                   lambda s, rs, tr, st, sf, se, nu: (layer, se[s], 0, nf + sf[s])),
                      pl.BlockSpec((tm, 1), lambda s, rs, tr, st, sf, se, nu: (st[s], 0))],
            out_specs=pl.BlockSpec((tm, MOE_TF), lambda s, rs, tr, st, sf, se, nu: (st[s], sf[s])),
            scratch_shapes=[pltpu.VMEM((tm * (d // LANES), LANES), F32)] * 2
            + [pltpu.SemaphoreType.DMA((2,))]),
        compiler_params=_cparams(("arbitrary",)),
        name="moe_up",
    )(row_src, tile_rows, st, sf, se, nu, x_lin, w_up, w_up, row_gate.reshape(npad, 1))

    nn = d // MOE_TN
    st, sf, se, nu = _moe_schedule(tiles_e, tile_start, nn, n_tiles)
    ys = pl.pallas_call(
        _moe_down_kernel,
        out_shape=jax.ShapeDtypeStruct((npad, d), F32),
        grid_spec=pltpu.PrefetchScalarGridSpec(
            num_scalar_prefetch=4,
            grid=(n_tiles * nn,),
            in_specs=[pl.BlockSpec((tm, D_EXPERT), lambda s, st, sf, se, nu: (st[s], 0)),
                      pl.BlockSpec((1, 1, D_EXPERT, MOE_TN),
                                   lambda s, st, sf, se, nu: (layer, se[s], 0, sf[s]))],
            out_specs=pl.BlockSpec((tm, MOE_TN), lambda s, st, sf, se, nu: (st[s], sf[s]))),
        compiler_params=_cparams(("arbitrary",)),
        name="moe_down",
    )(st, sf, se, nu, h, w_down)
    return (ys.at[pos[:t]].get(mode="promise_in_bounds", unique_indices=True),
            ys.at[pos[t:]].get(mode="promise_in_bounds", unique_indices=True))


TOK_TM = 512
MM_TM = 1024


def kernel(x_prompt, x_sample, state_gla, state_rwkv, state_shift, state_s5_re, state_s5_im, meta, ev_w_in, ev_w_out, a_gate_up, a_gate_b, a_norm_g, b_mu, b_w0, b_w_up, b_a0, b_a_up, b_g_up, b_k_k, b_k_a, b_r_k, b_ln_g, b_ln_b, od_w_in, c_a_re, c_a_im, c_log_dt, c_b_re, c_b_im, c_c_re, c_c_im, c_d, c_w_glu, c_b_glu, od_w_out, w_router, moe_w_up, moe_w_down, ln_mix_g, ln_mix_b, ln_ffn_g, ln_ffn_b):
    bp, sp, d = x_prompt.shape
    bs, ss, _ = x_sample.shape
    lp = sp + N_META
    tp = bp * lp
    t = tp + bs * ss
    pieces = [piece for b in range(bp) for piece in (meta, x_prompt[b])]
    x = jnp.concatenate(pieces + [x_sample.reshape(bs * ss, d)], axis=0)
    xb = x.astype(BF16)

    def ffn(x, x_lin, layer):
        f0, f1 = moe_layer(x, x_lin, w_router, moe_w_up, moe_w_down, layer)
        return add_ln(x, f0, f1, ln_ffn_g[layer], ln_ffn_b[layer], TOK_TM)

    w_in = ev_w_in[0]
    zpad = jnp.zeros((d, P_B_OFF - A_COLS), w_in.dtype)
    w_in = jnp.concatenate([w_in[:, :A_COLS], zpad, w_in[:, A_COLS:]], axis=1).astype(BF16)
    p = matmul(xb, w_in, MM_TM, P_COLS // 4)

    gla_w = (a_gate_up[0], a_gate_b[0], a_norm_g[0])
    oa_p, gla_p = gla_group(p, jnp.zeros((bp, A_HEADS, A_DK, A_DV), F32), *gla_w,
                            row_off=0, nbatch=bp, seq=lp, nb=4, lb=48, chunk=16)
    oa_s, gla_s = gla_group(p, state_gla[0], *gla_w,
                            row_off=tp, nbatch=bs, seq=ss, nb=4, lb=ss, chunk=ss)
    rw_w = (b_mu[0], b_w0[0], b_w_up[0], b_a0[0], b_a_up[0], b_g_up[0], b_k_k[0], b_k_a[0],
            b_r_k[0], b_ln_g[0], b_ln_b[0])
    ob_p, rwkv_p = rwkv_group(p, jnp.zeros((bp, B_HEADS, B_HEAD, B_HEAD), F32),
                              jnp.zeros((bp, B_COLS), F32), *rw_w,
                              row_off=0, nbatch=bp, seq=lp, nb=4, lb=48)
    ob_s, rwkv_s = rwkv_group(p, state_rwkv[0], state_shift[0], *rw_w,
                              row_off=tp, nbatch=bs, seq=ss, nb=4, lb=ss)
    last_p = jnp.arange(bp, dtype=jnp.int32) * lp + (lp - 1)
    last_s = jnp.arange(bs, dtype=jnp.int32) * ss + (tp + ss - 1)
    shift_p = p.at[last_p].get(mode="promise_in_bounds")[:, P_B_OFF:]
    shift_s = p.at[last_s].get(mode="promise_in_bounds")[:, P_B_OFF:]
    o_a = jnp.concatenate([oa_p, oa_s], axis=0)
    o_b = jnp.concatenate([ob_p, ob_s], axis=0)
    x, x_lin = proj_ln([o_a, o_b], ev_w_out[0].astype(BF16), x, ln_mix_g[0], ln_mix_b[0], TOK_TM)
    x, xb = ffn(x, x_lin, 0)

    u = matmul(xb, od_w_in[0].astype(BF16), MM_TM, d // 2)
    s5_w = (c_a_re[0], c_a_im[0], c_log_dt[0], c_b_re[0], c_b_im[0], c_c_re[0], c_c_im[0], c_d[0])
    zero_state = jnp.zeros((bp, C_GROUPS, C_STATE), F32)
    y_p, re_p, im_p = s5_group(u, zero_state, zero_state, s5_tables(*s5_w, 16),
                               nbatch=bp, seq=lp, chunk=16)
    y_s, re_s, im_s = s5_group(u[tp:], state_s5_re[0], state_s5_im[0], s5_tables(*s5_w, ss),
                               nbatch=bs, seq=ss, chunk=ss)
    y = jnp.concatenate([y_p, y_s], axis=0)
    z = gelu_glu(y, c_w_glu[0].astype(BF16), c_b_glu[0], TOK_TM)
    x, x_lin = proj_ln([z], od_w_out[0].astype(BF16), x, ln_mix_g[1], ln_mix_b[1], TOK_TM)
    x, xb = ffn(x, x_lin, 1)

    y_prompt = jnp.stack([x[b * lp + N_META:(b + 1) * lp] for b in range(bp)])
    y_sample = x[tp:].reshape(bs, ss, d)
    return (y_prompt, y_sample, gla_p[None], gla_s[None], rwkv_p[None], rwkv_s[None],
            shift_p[None], shift_s[None], re_p[None], re_s[None], im_p[None], im_s[None])
```

```python
import functools
import math

import jax
import jax.numpy as jnp
from jax import lax
from jax.experimental import pallas as pl
from jax.experimental.pallas import tpu as pltpu

F32 = jnp.float32
BF16 = jnp.bfloat16

D_MODEL = 2048
DEPTH = 2
N_META = 16

A_WIDTH = 1024
A_HEADS = 4
A_DV = 256
A_DK = 128
A_GATE_RANK = 16
A_GATE_TAU = 16.0
A_QK = A_HEADS * A_DK
A_COLS = 2 * A_QK + 2 * A_WIDTH + A_GATE_RANK

B_WIDTH = 1024
B_HEAD = 64
B_HEADS = 16
B_DECAY_RANK = 64
B_AAA_RANK = 64
B_GATE_RANK = 128
B_COLS = 3 * B_WIDTH + B_DECAY_RANK + B_AAA_RANK + B_GATE_RANK

C_GROUP = 16
C_GROUPS = 128
C_STATE = 64

N_EXPERTS = 16
N_EXPERT_GROUPS = 4
EXPERTS_PER_GROUP = 4
TOP_K = 2
D_EXPERT = 1024

ALPHA = (2.0 * DEPTH) ** 0.25
LN_EPS = 1e-5
HEAD_NORM_EPS = 1e-5
RWKV_GN_EPS = 64e-5

P_B_OFF = B_COLS
P_COLS = 2 * B_COLS

VMEM_LIMIT = 56 * 1024 * 1024
MXU_TILE = 256
LANES = 128


def _cparams(sem):
    return pltpu.CompilerParams(dimension_semantics=sem, vmem_limit_bytes=VMEM_LIMIT)


def _cdiv(a, b):
    return (a + b - 1) // b


def _softplus(x):
    return jnp.maximum(x, 0.0) + jnp.log(1.0 + jnp.exp(-jnp.abs(x)))


def _sigmoid(x):
    return 1.0 / (1.0 + jnp.exp(-x))


def _mm_kernel(x_ref, w_ref, o_ref):
    o_ref[...] = jnp.dot(x_ref[...].astype(BF16), w_ref[...].astype(BF16),
                         preferred_element_type=F32)


def matmul(x, w, tm, tn):
    m, k = x.shape
    n = w.shape[1]
    assert n % tn == 0
    return pl.pallas_call(
        _mm_kernel,
        out_shape=jax.ShapeDtypeStruct((m, n), F32),
        grid=(n // tn, _cdiv(m, tm)),
        in_specs=[pl.BlockSpec((tm, k), lambda j, i: (i, 0)),
                  pl.BlockSpec((k, tn), lambda j, i: (0, j))],
        out_specs=pl.BlockSpec((tm, tn), lambda j, i: (i, j)),
        compiler_params=_cparams(("arbitrary", "arbitrary")),
        name="matmul",
    )(x, w)


def _layer_norm_rows(y, g, b):
    mu = jnp.mean(y, axis=-1, keepdims=True)
    yc = y - mu
    var = jnp.mean(yc * yc, axis=-1, keepdims=True)
    return yc * lax.rsqrt(var + LN_EPS) * g + b


def _proj_ln_kernel(n_lhs, *refs):
    lhs = refs[:n_lhs]
    w_ref, x_ref, g_ref, b_ref, wr_ref, o_ref, ol_ref, e_ref, gt_ref = refs[n_lhs:]
    acc = None
    off = 0
    for r in lhs:
        kk = r.shape[1]
        part = jnp.dot(r[...], w_ref[off:off + kk, :], preferred_element_type=F32)
        acc = part if acc is None else acc + part
        off += kk
    y = _layer_norm_rows(ALPHA * x_ref[...] + acc, g_ref[...], b_ref[...])
    o_ref[...] = y
    tm, d = y.shape
    for s in range(d // LANES):
        ol_ref[pl.ds(s, tm, stride=d // LANES), :] = y[:, s * LANES:(s + 1) * LANES]
    e_ref[...], gt_ref[...] = _route(y, wr_ref[...])


def proj_ln(lhs_list, w_bf16, x, g, b, w_router, tm):
    m, d = x.shape
    kin = w_bf16.shape[0]
    sub = d // LANES
    in_specs = [pl.BlockSpec((tm, l.shape[1]), lambda i: (i, 0)) for l in lhs_list]
    in_specs += [pl.BlockSpec((kin, d), lambda i: (0, 0)),
                 pl.BlockSpec((tm, d), lambda i: (i, 0)),
                 pl.BlockSpec((1, d), lambda i: (0, 0)),
                 pl.BlockSpec((1, d), lambda i: (0, 0)),
                 pl.BlockSpec((N_EXPERTS, d), lambda i: (0, 0))]
    return pl.pallas_call(
        functools.partial(_proj_ln_kernel, len(lhs_list)),
        out_shape=(jax.ShapeDtypeStruct((m, d), F32), jax.ShapeDtypeStruct((m * sub, LANES), F32),
                   jax.ShapeDtypeStruct((TOP_K, m), jnp.int32), jax.ShapeDtypeStruct((TOP_K, m), F32)),
        grid=(_cdiv(m, tm),),
        in_specs=in_specs,
        out_specs=(pl.BlockSpec((tm, d), lambda i: (i, 0)),
                   pl.BlockSpec((tm * sub, LANES), lambda i: (i, 0)),
                   pl.BlockSpec((TOP_K, tm), lambda i: (0, i)),
                   pl.BlockSpec((TOP_K, tm), lambda i: (0, i))),
        compiler_params=_cparams(("arbitrary",)),
        name="proj_ln",
    )(*lhs_list, w_bf16, x, g.reshape(1, d), b.reshape(1, d), w_router.T)


def _add_ln_kernel(x_ref, f0_ref, f1_ref, g_ref, b_ref, o_ref, ob_ref):
    y = _layer_norm_rows(ALPHA * x_ref[...] + (f0_ref[...] + f1_ref[...]), g_ref[...], b_ref[...])
    o_ref[...] = y
    ob_ref[...] = y.astype(BF16)


def add_ln(x, f0, f1, g, b, tm):
    m, d = x.shape
    row = pl.BlockSpec((tm, d), lambda i: (i, 0))
    vec = pl.BlockSpec((1, d), lambda i: (0, 0))
    return pl.pallas_call(
        _add_ln_kernel,
        out_shape=(jax.ShapeDtypeStruct((m, d), F32), jax.ShapeDtypeStruct((m, d), BF16)),
        grid=(_cdiv(m, tm),),
        in_specs=[row, row, row, vec, vec],
        out_specs=(row, row),
        compiler_params=_cparams(("arbitrary",)),
        name="add_ln",
    )(x, f0, f1, g.reshape(1, d), b.reshape(1, d))


def _gla_kernel(nb, lb, chunk, *refs):
    p_refs = refs[:nb]
    (s0_ref, gup_ref, gb_ref, ng_ref) = refs[nb:nb + 4]
    o_ref = refs[nb + 4]
    sout_ref = refs[nb + 5]
    s_ref = refs[nb + 6]
    blk = pl.program_id(1)

    @pl.when(blk == 0)
    def _():
        s_ref[...] = s0_ref[...]

    rows = lax.broadcasted_iota(jnp.int32, (chunk, chunk), 0)
    cols = lax.broadcasted_iota(jnp.int32, (chunk, chunk), 1)
    tril = rows >= cols
    tril_f = tril.astype(F32)
    lgs = []
    for j in range(nb):
        gd = p_refs[j][:, 2 * A_QK + 2 * A_WIDTH:2 * A_QK + 2 * A_WIDTH + A_GATE_RANK]
        z = jnp.dot(gd, gup_ref[...], preferred_element_type=F32) + gb_ref[...]
        lgs.append(-_softplus(-z) * (1.0 / A_GATE_TAU))
    for c in range(lb // chunk):
        r0 = c * chunk
        for j in range(nb):
            p_ref = p_refs[j]
            bc = jnp.dot(tril_f, lgs[j][r0:r0 + chunk, :], preferred_element_type=F32,
                         precision=lax.Precision.HIGHEST)
            for h in range(A_HEADS):
                q = p_ref[r0:r0 + chunk, h * A_DK:(h + 1) * A_DK] * (A_DK ** -0.5)
                k = p_ref[r0:r0 + chunk, A_QK + h * A_DK:A_QK + (h + 1) * A_DK]
                v = p_ref[r0:r0 + chunk, 2 * A_QK + h * A_DV:2 * A_QK + (h + 1) * A_DV]
                rg = p_ref[r0:r0 + chunk,
                           2 * A_QK + A_WIDTH + h * A_DV:2 * A_QK + A_WIDTH + (h + 1) * A_DV]
                b = bc[:, h * A_DK:(h + 1) * A_DK]
                bl = b[chunk - 1:chunk, :]
                qd = (q * jnp.exp(b)).astype(BF16)
                kd = (k * jnp.exp(-b)).astype(BF16)
                vb = v.astype(BF16)
                att = lax.dot_general(qd, kd, (((1,), (1,)), ((), ())), preferred_element_type=F32)
                att = jnp.where(tril, att, 0.0).astype(BF16)
                s = s_ref[j, h]
                o = (jnp.dot(att, vb, preferred_element_type=F32)
                     + jnp.dot(qd, s.astype(BF16), preferred_element_type=F32))
                kl = (k * jnp.exp(bl - b)).astype(BF16)
                kv = lax.dot_general(kl, vb, (((0,), (0,)), ((), ())), preferred_element_type=F32)
                dec = jnp.broadcast_to(jnp.exp(bl), (A_DK, A_DK)).T
                s_ref[j, h] = s * jnp.concatenate([dec] * (A_DV // A_DK), axis=1) + kv
                o = o * lax.rsqrt(jnp.mean(o * o, axis=-1, keepdims=True) + HEAD_NORM_EPS)
                o = o * ng_ref[:, h * A_DV:(h + 1) * A_DV] * (rg * _sigmoid(rg))
                o_ref[j, r0:r0 + chunk, h * A_DV:(h + 1) * A_DV] = o.astype(BF16)

    @pl.when(blk == pl.num_programs(1) - 1)
    def _():
        sout_ref[...] = s_ref[...]


def gla_group(p, s0, gate_up, gate_b, norm_g, *, row_off, nbatch, seq, nb, lb, chunk):
    t = p.shape[0]
    nblk = seq // lb
    assert seq % lb == 0 and lb % chunk == 0 and nbatch % nb == 0 and row_off % lb == 0
    base = row_off // lb

    def p_map(j):
        return lambda bi, blk: (base + (bi * nb + j) * nblk + blk, 0)

    in_specs = [pl.BlockSpec((lb, P_B_OFF), p_map(j)) for j in range(nb)]
    in_specs += [pl.BlockSpec((nb, A_HEADS, A_DK, A_DV), lambda bi, blk: (bi, 0, 0, 0)),
                 pl.BlockSpec((A_GATE_RANK, A_QK), lambda bi, blk: (0, 0)),
                 pl.BlockSpec((1, A_QK), lambda bi, blk: (0, 0)),
                 pl.BlockSpec((1, A_WIDTH), lambda bi, blk: (0, 0))]

    out_specs = [pl.BlockSpec((nb, lb, A_WIDTH), lambda bi, blk: (bi, blk, 0)),
                 pl.BlockSpec((nb, A_HEADS, A_DK, A_DV), lambda bi, blk: (bi, 0, 0, 0))]
    out_shape = [jax.ShapeDtypeStruct((nbatch, seq, A_WIDTH), BF16),
                 jax.ShapeDtypeStruct((nbatch, A_HEADS, A_DK, A_DV), F32)]
    o, s_new = pl.pallas_call(
        functools.partial(_gla_kernel, nb, lb, chunk),
        out_shape=out_shape,
        grid=(nbatch // nb, nblk),
        in_specs=in_specs,
        out_specs=out_specs,
        scratch_shapes=[pltpu.VMEM((nb, A_HEADS, A_DK, A_DV), F32)],
        compiler_params=_cparams(("arbitrary", "arbitrary")),
        name="gla",
    )(*([p] * nb), s0, gate_up, gate_b.reshape(1, A_QK), norm_g.reshape(1, A_WIDTH))
    return o.reshape(nbatch * seq, A_WIDTH), s_new


VT_SLOT = 128
RWKV_UNROLL = 8


def _head_ones(dtype):
    r = lax.broadcasted_iota(jnp.int32, (MXU_TILE, MXU_TILE), 0) // B_HEAD
    c = lax.broadcasted_iota(jnp.int32, (MXU_TILE, MXU_TILE), 1) // B_HEAD
    return (r == c).astype(dtype)


def _head_sum(x, ones):
    xh = x.astype(BF16)
    xl = (x - xh.astype(F32)).astype(BF16)
    parts = [jnp.dot(xh[:, c * MXU_TILE:(c + 1) * MXU_TILE], ones, preferred_element_type=F32)
             + jnp.dot(xl[:, c * MXU_TILE:(c + 1) * MXU_TILE], ones, preferred_element_type=F32)
             for c in range(B_WIDTH // MXU_TILE)]
    return jnp.concatenate(parts, axis=-1)


def _rwkv_kernel(nb, lb, *refs):
    p_refs = refs[:nb]
    (shift0_ref, s0_ref, mu_ref, w0_ref, wup_ref, a0_ref, aup_ref, gup_ref,
     kk_ref, ka_ref, rk_ref, lng_ref, lnb_ref) = refs[nb:nb + 13]
    o_ref, sout_ref = refs[nb + 13:nb + 15]
    (s_ref, prev_ref, r_s, w_s, kh_s, kl_s, kk_s, kka_s, on_s, g_s, bon_s, vmh_s, vml_s, vt_s
     ) = refs[nb + 15:]
    blk = pl.program_id(1)
    W = B_WIDTH
    H = B_HEADS
    NT = (((1,), (1,)), ((), ()))

    @pl.when(blk == 0)
    def _():
        s_ref[...] = s0_ref[...]
        prev_ref[...] = shift0_ref[...]

    ones = _head_ones(BF16)
    hmask = (lax.broadcasted_iota(jnp.int32, (H, W), 1) // B_HEAD
             == lax.broadcasted_iota(jnp.int32, (H, W), 0))

    def to_head_rows(ref, j, x):
        ref[j] = jnp.zeros(ref.shape[1:], F32)
        for h in range(H):
            ref[j, pl.ds(h, lb, stride=H), 0:B_HEAD] = x[:, h * B_HEAD:(h + 1) * B_HEAD]

    def split(x):
        xh = x.astype(BF16).astype(F32)
        return xh, x - xh

    xms = []
    for j in range(nb):
        pb = p_refs[j][...]
        first = lax.broadcasted_iota(jnp.int32, (lb, 1), 0) == 0
        prev = jnp.where(first, prev_ref[j], pltpu.roll(pb, 1, axis=0))
        prev_ref[j] = pb[lb - 1:lb, :]
        xms.append(pb + (prev - pb) * mu_ref[...])
    xm = jnp.concatenate(xms, axis=0)
    r = xm[:, :W]
    k = xm[:, W:2 * W]
    v = xm[:, 2 * W:3 * W]
    wd = xm[:, 3 * W:3 * W + B_DECAY_RANK]
    ad = xm[:, 3 * W + B_DECAY_RANK:3 * W + B_DECAY_RANK + B_AAA_RANK]
    gd = xm[:, 3 * W + B_DECAY_RANK + B_AAA_RANK:]
    w = -_softplus(-(w0_ref[...] + jnp.dot(jnp.tanh(wd).astype(BF16), wup_ref[...].astype(BF16),
                                           preferred_element_type=F32))) - 0.5
    a = _sigmoid(a0_ref[...] + jnp.dot(ad.astype(BF16), aup_ref[...].astype(BF16),
                                       preferred_element_type=F32))
    g = jnp.dot(_sigmoid(gd).astype(BF16), gup_ref[...].astype(BF16), preferred_element_type=F32)
    kk = k * kk_ref[...]
    nrm = jnp.sqrt(_head_sum(kk * kk, ones))
    kk = kk / jnp.maximum(nrm, 1e-12)
    k2 = k * (1.0 + (a - 1.0) * ka_ref[...])
    per_seq = lambda x: x.reshape(nb, lb, x.shape[-1])
    r_s[...] = per_seq(r)
    w_s[...] = per_seq(jnp.exp(-jnp.exp(w)))
    kh, kl = split(k2)
    kh_s[...], kl_s[...] = per_seq(kh), per_seq(kl)
    kk_s[...] = per_seq(kk)
    kka_s[...] = per_seq(-(kk * a))
    vh, vl = split(v)
    for j in range(nb):
        to_head_rows(vmh_s, j, vh[j * lb:(j + 1) * lb])
        to_head_rows(vml_s, j, vl[j * lb:(j + 1) * lb])
    g_s[...] = per_seq(g)
    bon_s[...] = per_seq(_head_sum(r * k2 * rk_ref[...], ones) * v)

    def head_rows(ref, j, t):
        return jnp.where(hmask, jnp.broadcast_to(ref[j, pl.ds(t, 1), :], (H, W)), 0.0).astype(BF16)

    def readout(j, t, sb):
        rows = pl.ds(t * H if isinstance(t, int) else pl.multiple_of(t * H, H), H)
        on_s[j, rows, :] = lax.dot_general(head_rows(r_s, j, t), sb, NT, preferred_element_type=F32)

    zrows = lambda n: jnp.zeros((n * H, W), BF16)
    lane = lax.broadcasted_iota(jnp.int32, (B_HEAD, VT_SLOT), 1)

    def vt_step(t, carry):
        rows = pl.ds(pl.multiple_of(t * H, H), H)
        for j in range(nb):
            vmh, vml = vmh_s[j, rows, :], vml_s[j, rows, :]
            tile = jnp.concatenate([vmh, vml, vmh, jnp.zeros((VT_SLOT - 3 * H, VT_SLOT), F32)], axis=0)
            vt_s[j, t] = tile.T[:B_HEAD, :].astype(BF16)
        return carry

    lax.fori_loop(0, lb, vt_step, 0, unroll=RWKV_UNROLL)

    def step(t, carry):
        tp = jnp.maximum(t - 1, 0)
        prods = []
        for j in range(nb):
            sb = s_ref[j].astype(BF16)
            prods.append(lax.dot_general(
                jnp.concatenate([head_rows(kk_s, j, t), head_rows(r_s, j, tp)], axis=0), sb, NT,
                preferred_element_type=F32))
        pad_v = jnp.zeros((H, VT_SLOT - B_HEAD), F32)
        zblk = jnp.zeros((3 * H, VT_SLOT), F32)
        for j in range(nb):
            on_s[j, pl.ds(pl.multiple_of(tp * H, H), H), :] = prods[j][H:]
            sat = jnp.concatenate([prods[j][:H], pad_v], axis=1)
            sa = jnp.concatenate([zblk, sat, sat, zblk], axis=0).T[:B_HEAD, :]
            sa_hi = sa.astype(BF16)
            sa_lo = (sa - sa_hi.astype(F32)).astype(BF16)
            x = jnp.where(lane < 3 * H, vt_s[j, t], jnp.where(lane < 4 * H, sa_hi, sa_lo))
            khw, klw, kaw = head_rows(kh_s, j, t), head_rows(kl_s, j, t), head_rows(kka_s, j, t)
            upd = jnp.dot(x[:, :5 * H], jnp.concatenate([khw, khw, klw, kaw, kaw], axis=0),
                          preferred_element_type=F32)
            s_ref[j] = s_ref[j] * w_s[j, pl.ds(t, 1), :] + upd
        return carry

    lax.fori_loop(0, lb, step, 0, unroll=RWKV_UNROLL)

    os = []
    for j in range(nb):
        readout(j, lb - 1, s_ref[j].astype(BF16))
        os.append(jnp.concatenate([on_s[j, pl.ds(h, lb, stride=H), :] for h in range(H)], axis=-1))
    o = jnp.concatenate(os, axis=0)
    oc = o - _head_sum(o, ones) * (1.0 / B_HEAD)
    var = _head_sum(oc * oc, ones) * (1.0 / B_HEAD)
    o = per_seq(oc * lax.rsqrt(var + RWKV_GN_EPS) * lng_ref[...] + lnb_ref[...])
    o_ref[...] = ((o + bon_s[...]) * g_s[...]).astype(BF16)

    @pl.when(blk == pl.num_programs(1) - 1)
    def _():
        sout_ref[...] = s_ref[...]


def rwkv_group(p, s0, shift0, mu, w0, w_up, a0, a_up, g_up, k_k, k_a, r_k, ln_g, ln_b,
               *, row_off, nbatch, seq, nb, lb):
    nblk = seq // lb
    assert seq % lb == 0 and nbatch % nb == 0 and row_off % lb == 0
    base = row_off // lb
    W = B_WIDTH
    s0 = s0.transpose(0, 2, 1, 3).reshape(nbatch, B_HEAD, W)

    def p_map(j):
        return lambda bi, blk: (base + (bi * nb + j) * nblk + blk, 1)

    const = lambda shape: pl.BlockSpec(shape, lambda bi, blk: (0,) * len(shape))
    in_specs = [pl.BlockSpec((lb, B_COLS), p_map(j)) for j in range(nb)]
    in_specs += [pl.BlockSpec((nb, 1, B_COLS), lambda bi, blk: (bi, 0, 0)),
                 pl.BlockSpec((nb, B_HEAD, W), lambda bi, blk: (bi, 0, 0)),
                 const((1, B_COLS)), const((1, W)), const((B_DECAY_RANK, W)), const((1, W)),
                 const((B_AAA_RANK, W)), const((B_GATE_RANK, W)),
                 const((1, W)), const((1, W)), const((1, W)), const((1, W)), const((1, W))]
    out_specs = [pl.BlockSpec((nb, lb, W), lambda bi, blk: (bi, blk, 0)),
                 pl.BlockSpec((nb, B_HEAD, W), lambda bi, blk: (bi, 0, 0))]
    out_shape = [jax.ShapeDtypeStruct((nbatch, seq, W), BF16),
                 jax.ShapeDtypeStruct((nbatch, B_HEAD, W), F32)]
    tok = pltpu.VMEM((nb, lb, W), F32)
    o, s_new = pl.pallas_call(
        functools.partial(_rwkv_kernel, nb, lb),
        out_shape=out_shape,
        grid=(nbatch // nb, nblk),
        in_specs=in_specs,
        out_specs=out_specs,
        scratch_shapes=[pltpu.VMEM((nb, B_HEAD, W), F32), pltpu.VMEM((nb, 1, B_COLS), F32)]
        + [tok] * 6 + [pltpu.VMEM((nb, lb * B_HEADS, B_HEAD), F32), tok, tok,
                       pltpu.VMEM((nb, lb * B_HEADS, VT_SLOT), F32),
                       pltpu.VMEM((nb, lb * B_HEADS, VT_SLOT), F32),
                       pltpu.VMEM((nb, lb, B_HEAD, VT_SLOT), BF16)],
        compiler_params=_cparams(("arbitrary", "arbitrary")),
        name="rwkv",
    )(*([p] * nb), shift0.reshape(nbatch, 1, B_COLS), s0, mu.reshape(1, B_COLS),
      w0.reshape(1, W), w_up, a0.reshape(1, W), a_up, g_up, k_k.reshape(1, W),
      k_a.reshape(1, W), r_k.reshape(1, W), ln_g.reshape(1, W), ln_b.reshape(1, W))
    s_new = s_new.reshape(nbatch, B_HEAD, B_HEADS, B_HEAD).transpose(0, 2, 1, 3)
    return o.reshape(nbatch * seq, W), s_new


S5_GT = LANES // C_GROUP
SP = C_GROUPS * C_STATE


def s5_tables(a_re, a_im, log_dt, b_re, b_im, c_re, c_im, d, chunk):
    G, P, c = C_GROUPS, C_STATE, C_GROUP
    dt = jnp.exp(log_dt)[:, None]
    m = jnp.arange(chunk + 1, dtype=F32)[:, None, None]
    mag = jnp.exp(m * (dt * a_re))
    ang = m * (dt * a_im)
    pw_re, pw_im = mag * jnp.cos(ang), mag * jnp.sin(ang)
    num_re, num_im = pw_re[1] - 1.0, pw_im[1]
    den = a_re * a_re + a_im * a_im
    q_re = (num_re * a_re + num_im * a_im) / den
    q_im = (num_im * a_re - num_re * a_im) / den
    bb_re = q_re[..., None] * b_re - q_im[..., None] * b_im
    bb_im = q_re[..., None] * b_im + q_im[..., None] * b_re
    ca_re = c_re[None] * pw_re[:, :, None, :] - c_im[None] * pw_im[:, :, None, :]
    ca_im = c_re[None] * pw_im[:, :, None, :] + c_im[None] * pw_re[:, :, None, :]
    cm = lambda z: z[:chunk].transpose(1, 3, 0, 2).reshape(G, P, chunk * c)
    toep = s5_toeplitz(bb_re.transpose(0, 2, 1), bb_im.transpose(0, 2, 1), cm(ca_re), cm(ca_im), chunk)
    dvec = jnp.tile(d.reshape(G, 1, c), (1, chunk, 1)).reshape(G, 1, chunk * c)
    rev_re, rev_im = pw_re[:chunk][::-1], pw_im[:chunk][::-1]
    bs_re = rev_re[..., None] * bb_re[None] - rev_im[..., None] * bb_im[None]
    bs_im = rev_re[..., None] * bb_im[None] + rev_im[..., None] * bb_re[None]
    to_in = lambda z: z.transpose(1, 0, 3, 2).reshape(G, chunk * c, P)
    to_out = lambda z: z.transpose(1, 3, 0, 2).reshape(G, P, chunk * c)
    cs_re, cs_im = to_out(ca_re[1:]), to_out(-ca_im[1:])
    return dict(toep=toep, dvec=dvec,
                bs_re=to_in(bs_re).astype(BF16), bs_im=to_in(bs_im).astype(BF16),
                cs_re=cs_re.astype(BF16), cs_im=cs_im.astype(BF16),
                ac_re=pw_re[chunk].reshape(1, SP), ac_im=pw_im[chunk].reshape(1, SP))


def _s5_toeplitz_kernel(chunk, bre_ref, bim_ref, cre_ref, cim_ref, o_ref):
    c = C_GROUP
    cr = chunk * c
    hi = lax.Precision.HIGHEST
    lane = lax.broadcasted_iota(jnp.int32, (c, cr), 1)
    for q in range(S5_GT):
        k0 = (jnp.dot(bre_ref[q], cre_ref[q], preferred_element_type=F32, precision=hi)
              - jnp.dot(bim_ref[q], cim_ref[q], preferred_element_type=F32, precision=hi))
        for j in range(chunk):
            blk = k0 if j == 0 else jnp.where(lane >= j * c, pltpu.roll(k0, j * c, axis=1), 0.0)
            o_ref[q, j * c:(j + 1) * c, :] = blk.astype(BF16)


def s5_toeplitz(bbt_re, bbt_im, cm_re, cm_im, chunk):
    G, P, c = C_GROUPS, C_STATE, C_GROUP
    cr = chunk * c
    grp = lambda shape: pl.BlockSpec((S5_GT,) + shape, lambda g: (g, 0, 0))
    return pl.pallas_call(
        functools.partial(_s5_toeplitz_kernel, chunk),
        out_shape=jax.ShapeDtypeStruct((G, cr, cr), BF16),
        grid=(G // S5_GT,),
        in_specs=[grp((c, P)), grp((c, P)), grp((P, cr)), grp((P, cr))],
        out_specs=grp((cr, cr)),
        compiler_params=_cparams(("arbitrary",)),
        name="s5_toeplitz",
    )(bbt_re, bbt_im, cm_re, cm_im)


def _swap_blocks(xs, blk):
    n = len(xs)
    w = xs[0].shape[-1] // n
    s = n // 2
    while s >= 1:
        nxt = list(xs)
        for i in range(n):
            if i & s == 0:
                j = i + s
                low = (blk & s) == 0
                nxt[i] = jnp.where(low, xs[i], pltpu.roll(xs[j], s * w, axis=1))
                nxt[j] = jnp.where(low, pltpu.roll(xs[i], (n - s) * w, axis=1), xs[j])
        xs = nxt
        s //= 2
    return xs


def _s5_kernel(nbatch, nchunks, chunk, u_ref, toep_ref, bre_ref, bim_ref, cre_ref, cim_ref, dvec_ref,
               acre_ref, acim_ref, s0re_ref, s0im_ref, y_ref, fre_ref, fim_ref,
               uf_s, vre_s, vim_s, sre_s, sim_s):
    c, P, GT = C_GROUP, C_STATE, S5_GT
    R = nbatch * nchunks
    cr = chunk * c
    halves = cr // LANES
    per_half = LANES // c
    blk = lax.broadcasted_iota(jnp.int32, (R, LANES), 1) // c

    for h in range(halves):
        xs = _swap_blocks([u_ref[pl.ds(h * per_half + q, R, stride=chunk), :] for q in range(per_half)], blk)
        for g in range(GT):
            uf_s[g, :, h * LANES:(h + 1) * LANES] = xs[g]

    for g in range(GT):
        ub = uf_s[g].astype(BF16)
        vre_s[:, g * P:(g + 1) * P] = jnp.dot(ub, bre_ref[g], preferred_element_type=F32)
        vim_s[:, g * P:(g + 1) * P] = jnp.dot(ub, bim_ref[g], preferred_element_type=F32)

    ar, ai = acre_ref[...], acim_ref[...]
    if nchunks == 1:
        sre_s[...] = s0re_ref[...]
        sim_s[...] = s0im_ref[...]
        sr, si = s0re_ref[...], s0im_ref[...]
        fre_ref[...] = ar * sr - ai * si + vre_s[...]
        fim_ref[...] = ar * si + ai * sr + vim_s[...]
    else:
        def step(ci, carry):
            sr, si = carry
            vr = jnp.concatenate([vre_s[pl.ds(b * nchunks + ci, 1), :] for b in range(nbatch)], axis=0)
            vi = jnp.concatenate([vim_s[pl.ds(b * nchunks + ci, 1), :] for b in range(nbatch)], axis=0)
            for b in range(nbatch):
                sre_s[pl.ds(b * nchunks + ci, 1), :] = sr[b:b + 1]
                sim_s[pl.ds(b * nchunks + ci, 1), :] = si[b:b + 1]
            return ar * sr - ai * si + vr, ar * si + ai * sr + vi

        sr, si = lax.fori_loop(0, nchunks, step, (s0re_ref[...], s0im_ref[...]))
        fre_ref[...] = sr
        fim_ref[...] = si

    for g in range(GT):
        u = uf_s[g]
        y = (jnp.dot(u.astype(BF16), toep_ref[g], preferred_element_type=F32)
             + jnp.dot(sre_s[:, g * P:(g + 1) * P].astype(BF16), cre_ref[g], preferred_element_type=F32)
             + jnp.dot(sim_s[:, g * P:(g + 1) * P].astype(BF16), cim_ref[g], preferred_element_type=F32))
        uf_s[g] = y + dvec_ref[g] * u
    for h in range(halves):
        ys = _swap_blocks([uf_s[g, :, h * LANES:(h + 1) * LANES] for g in range(GT)], blk)
        for q in range(per_half):
            y_ref[pl.ds(h * per_half + q, R, stride=chunk), :] = ys[q]


def s5_group(u, s0_re, s0_im, tab, *, nbatch, seq, chunk):
    G, P, c, GT = C_GROUPS, C_STATE, C_GROUP, S5_GT
    nchunks = seq // chunk
    assert seq % chunk == 0 and u.shape[0] >= nbatch * seq and u.shape[1] == G * c
    R = nbatch * nchunks
    cr = chunk * c
    rows = nbatch * seq
    grp = lambda shape: pl.BlockSpec((GT,) + shape, lambda i: (i, 0, 0))
    lanes = lambda r, w: pl.BlockSpec((r, w), lambda i: (0, i))
    y, f_re, f_im = pl.pallas_call(
        functools.partial(_s5_kernel, nbatch, nchunks, chunk),
        out_shape=[jax.ShapeDtypeStruct((rows, G * c), F32)] + [jax.ShapeDtypeStruct((nbatch, SP), F32)] * 2,
        grid=(G // GT,),
        in_specs=[lanes(rows, GT * c), grp((cr, cr)), grp((cr, P)), grp((cr, P)), grp((P, cr)), grp((P, cr)),
                  grp((1, cr)), lanes(1, GT * P), lanes(1, GT * P), lanes(nbatch, GT * P), lanes(nbatch, GT * P)],
        out_specs=[lanes(rows, GT * c), lanes(nbatch, GT * P), lanes(nbatch, GT * P)],
        scratch_shapes=[pltpu.VMEM((GT, R, cr), F32)] + [pltpu.VMEM((R, GT * P), F32)] * 4,
        compiler_params=_cparams(("arbitrary",)),
        name="s5",
    )(u, tab['toep'], tab['bs_re'], tab['bs_im'], tab['cs_re'], tab['cs_im'], tab['dvec'],
      tab['ac_re'], tab['ac_im'], s0_re.reshape(nbatch, SP), s0_im.reshape(nbatch, SP))
    return y, f_re.reshape(nbatch, G, P), f_im.reshape(nbatch, G, P)


def _glu_kernel(y_ref, w_ref, b_ref, o_ref):
    z = jax.nn.gelu(y_ref[...])
    acc = jnp.dot(z.astype(BF16), w_ref[...], preferred_element_type=F32) + b_ref[...]
    o_ref[...] = (z * _sigmoid(acc)).astype(BF16)


def gelu_glu(y, w_bf16, b, tm):
    m, k = y.shape
    n = w_bf16.shape[1]
    assert n == k
    return pl.pallas_call(
        _glu_kernel,
        out_shape=jax.ShapeDtypeStruct((m, n), BF16),
        grid=(_cdiv(m, tm),),
        in_specs=[pl.BlockSpec((tm, k), lambda i: (i, 0)),
                  pl.BlockSpec((k, n), lambda i: (0, 0)),
                  pl.BlockSpec((1, n), lambda i: (0, 0))],
        out_specs=pl.BlockSpec((tm, n), lambda i: (i, 0)),
        compiler_params=_cparams(("arbitrary",)),
        name="gelu_glu",
    )(y, w_bf16, b.reshape(1, n))


MOE_TM = 512
MOE_TF = 1024
MOE_TN = 2048
ROW_GROUP = 8


def _route(x, wr):
    logits = lax.dot_general(wr, x, (((1,), (1,)), ((), ())),
                             preferred_element_type=F32, precision=lax.Precision.HIGHEST)
    mx = jnp.max(logits, axis=0, keepdims=True)
    ex = jnp.exp(logits - mx)
    probs = ex / jnp.sum(ex, axis=0, keepdims=True)
    neg = jnp.float32(-jnp.inf)
    best = None
    for gi in range(N_EXPERT_GROUPS):
        v = [probs[gi * EXPERTS_PER_GROUP + r:gi * EXPERTS_PER_GROUP + r + 1, :]
             for r in range(EXPERTS_PER_GROUP)]
        m1 = jnp.maximum(jnp.maximum(v[0], v[1]), jnp.maximum(v[2], v[3]))
        i1 = jnp.where(v[0] == m1, 0, jnp.where(v[1] == m1, 1, jnp.where(v[2] == m1, 2, 3)))
        w = [jnp.where(i1 == r, neg, v[r]) for r in range(EXPERTS_PER_GROUP)]
        m2 = jnp.maximum(jnp.maximum(w[0], w[1]), jnp.maximum(w[2], w[3]))
        i2 = jnp.where(w[0] == m2, 0, jnp.where(w[1] == m2, 1, jnp.where(w[2] == m2, 2, 3)))
        score = m1 + m2
        cand = (score, m1, m2, i1 + gi * EXPERTS_PER_GROUP, i2 + gi * EXPERTS_PER_GROUP)
        if best is None:
            best = cand
        else:
            take = cand[0] > best[0]
            best = tuple(jnp.where(take, cn, bs) for cn, bs in zip(cand, best))
    _, m1, m2, e1, e2 = best
    tot = m1 + m2
    return jnp.concatenate([e1, e2], axis=0), jnp.concatenate([m1 / tot, m2 / tot], axis=0)


def _moe_up_kernel(rs_ref, tr_ref, st_ref, sf_ref, se_ref, nu_ref, x_hbm, w1_ref, w2_ref, gate_ref, h_ref,
                   xbuf0, xbuf1, sem):
    s = pl.program_id(0)
    n = pl.num_programs(0)
    sub = D_MODEL // LANES
    tm = xbuf0.shape[0] // sub
    bufs = (xbuf0, xbuf1)

    def fetched(step):
        rows = tr_ref[st_ref[jnp.minimum(step, n - 1)]]
        return jnp.where(step < n, (rows + ROW_GROUP - 1) // ROW_GROUP * ROW_GROUP, 0)

    def row_copy(step, slot, r):
        src = rs_ref[st_ref[step] * tm + r]
        return pltpu.make_async_copy(x_hbm.at[pl.ds(src * sub, sub), :],
                                     bufs[slot].at[pl.ds(r * sub, sub), :], sem.at[slot])

    def for_rows(step, slot, count, fn):
        def body(i, carry):
            for q in range(ROW_GROUP):
                fn(row_copy(step, slot, i * ROW_GROUP + q))
            return carry
        lax.fori_loop(0, count // ROW_GROUP, body, 0)

    def tile(slot):
        return jnp.concatenate([bufs[slot][pl.ds(c, tm, stride=sub), :] for c in range(sub)], axis=-1)

    def up(slot):
        x = tile(slot).astype(BF16)
        a = jnp.dot(x, w1_ref[0, 0].astype(BF16), preferred_element_type=F32)
        b = jnp.dot(x, w2_ref[0, 0].astype(BF16), preferred_element_type=F32)
        own = lax.broadcasted_iota(jnp.int32, (tm, 1), 0) < tr_ref[st_ref[s]]
        h_ref[...] = jnp.where(own, a * _sigmoid(a) * b * gate_ref[...], 0.0).astype(BF16)

    @pl.when(s == 0)
    def _():
        xbuf0[...] = jnp.zeros(xbuf0.shape, F32)
        xbuf1[...] = jnp.zeros(xbuf1.shape, F32)
        for_rows(0, 0, fetched(0), lambda cp: cp.start())

    def step(slot):
        for_rows(s, slot, fetched(s), lambda cp: cp.wait())
        nxt_rows = fetched(s + 1)

        @pl.when(nxt_rows == tm)
        def _():
            for r in range(tm):
                row_copy(s + 1, 1 - slot, r).start(priority=r % 2)
            up(slot)

        @pl.when(nxt_rows < tm)
        def _():
            for_rows(jnp.minimum(s + 1, n - 1), 1 - slot, nxt_rows, lambda cp: cp.start())
            pl.when(s < nu_ref[0])(functools.partial(up, slot))

            @pl.when(s >= nu_ref[0])
            def _():
                h_ref[...] = jnp.zeros(h_ref.shape, h_ref.dtype)

    for slot in range(2):
        pl.when(s % 2 == slot)(functools.partial(step, slot))


def _moe_down_kernel(st_ref, sf_ref, se_ref, nu_ref, h_ref, w_ref, y_ref):
    @pl.when(pl.program_id(0) < nu_ref[0])
    def _():
        y_ref[...] = jnp.dot(h_ref[...], w_ref[0, 0].astype(BF16), preferred_element_type=F32)

    @pl.when(pl.program_id(0) >= nu_ref[0])
    def _():
        y_ref[...] = jnp.zeros(y_ref.shape, y_ref.dtype)


def _moe_schedule(tiles_e, tile_start, nblocks, n_tiles):
    steps_e = tiles_e * nblocks
    cum = jnp.cumsum(steps_e)
    used = cum[-1]
    s = jnp.arange(n_tiles * nblocks, dtype=jnp.int32)
    sc = jnp.minimum(s, used - 1)
    e = jnp.minimum(jnp.sum((cum[None, :] <= sc[:, None]).astype(jnp.int32), axis=1), N_EXPERTS - 1)
    local = sc - (cum[e] - steps_e[e])
    te = jnp.maximum(tiles_e[e], 1)
    rest = s - used
    tile = jnp.where(s < used, tile_start[e] + local % te, jnp.sum(tiles_e) + rest // nblocks)
    blk = jnp.where(s < used, local // te, rest % nblocks)
    return tile.astype(jnp.int32), blk.astype(jnp.int32), e, used.reshape(1).astype(jnp.int32)


def moe_layer(x_lin, eid, gates, w_up, w_down, layer):
    d = D_MODEL
    t = eid.shape[1]
    tm = MOE_TM
    eid = eid.reshape(-1)
    na = TOP_K * t
    n_tiles = _cdiv(na + N_EXPERTS * (tm - 1), tm)
    npad = n_tiles * tm
    onehot = (eid[None, :] == jnp.arange(N_EXPERTS, dtype=jnp.int32)[:, None]).astype(jnp.int32)
    csum = jnp.cumsum(onehot, axis=1)
    counts = csum[:, -1]
    rank = jnp.sum(csum * onehot, axis=0) - 1
    tiles_e = (counts + tm - 1) // tm
    tile_start = jnp.cumsum(tiles_e) - tiles_e
    pos = tile_start[eid] * tm + rank
    row_a = jnp.full((npad,), -1, jnp.int32).at[pos].set(
        jnp.arange(na, dtype=jnp.int32), mode="promise_in_bounds", unique_indices=True)
    a_c = jnp.maximum(row_a, 0)
    row_src = jnp.where(a_c >= t, a_c - t, a_c)
    row_gate = jnp.where(row_a >= 0, gates.reshape(-1).at[a_c].get(mode="promise_in_bounds"), 0.0)

    ti = jnp.arange(n_tiles, dtype=jnp.int32)
    te = jnp.minimum(jnp.sum((jnp.cumsum(tiles_e)[None, :] <= ti[:, None]).astype(jnp.int32), axis=1),
                     N_EXPERTS - 1)
    tile_rows = jnp.where(ti < jnp.sum(tiles_e),
                          jnp.clip(counts[te] - (ti - tile_start[te]) * tm, 0, tm), 0).astype(jnp.int32)

    nf = D_EXPERT // MOE_TF
    st, sf, se, nu = _moe_schedule(tiles_e, tile_start, nf, n_tiles)
    h = pl.pallas_call(
        _moe_up_kernel,
        out_shape=jax.ShapeDtypeStruct((npad, D_EXPERT), BF16),
        grid_spec=pltpu.PrefetchScalarGridSpec(
            num_scalar_prefetch=6,
            grid=(n_tiles * nf,),
            in_specs=[pl.BlockSpec(memory_space=pl.ANY),
                      pl.BlockSpec((1, 1, d, MOE_TF),
                                   lambda s, rs, tr, st, sf, se, nu: (layer, se[s], 0, sf[s])),
                      pl.BlockSpec((1, 1, d, MOE_TF),
                                   lambda s, rs, tr, st, sf, se, nu: (layer, se[s], 0, nf + sf[s])),
                      pl.BlockSpec((tm, 1), lambda s, rs, tr, st, sf, se, nu: (st[s], 0))],
            out_specs=pl.BlockSpec((tm, MOE_TF), lambda s, rs, tr, st, sf, se, nu: (st[s], sf[s])),
            scratch_shapes=[pltpu.VMEM((tm * (d // LANES), LANES), F32)] * 2
            + [pltpu.SemaphoreType.DMA((2,))]),
        compiler_params=_cparams(("arbitrary",)),
        name="moe_up",
    )(row_src, tile_rows, st, sf, se, nu, x_lin, w_up, w_up, row_gate.reshape(npad, 1))

    nn = d // MOE_TN
    st, sf, se, nu = _moe_schedule(tiles_e, tile_start, nn, n_tiles)
    ys = pl.pallas_call(
        _moe_down_kernel,
        out_shape=jax.ShapeDtypeStruct((npad, d), F32),
        grid_spec=pltpu.PrefetchScalarGridSpec(
            num_scalar_prefetch=4,
            grid=(n_tiles * nn,),
            in_specs=[pl.BlockSpec((tm, D_EXPERT), lambda s, st, sf, se, nu: (st[s], 0)),
                      pl.BlockSpec((1, 1, D_EXPERT, MOE_TN),
                                   lambda s, st, sf, se, nu: (layer, se[s], 0, sf[s]))],
            out_specs=pl.BlockSpec((tm, MOE_TN), lambda s, st, sf, se, nu: (st[s], sf[s]))),
        compiler_params=_cparams(("arbitrary",)),
        name="moe_down",
    )(st, sf, se, nu, h, w_down)
    return (ys.at[pos[:t]].get(mode="promise_in_bounds", unique_indices=True),
            ys.at[pos[t:]].get(mode="promise_in_bounds", unique_indices=True))


TOK_TM = 512
MM_TM = 1024


def kernel(x_prompt, x_sample, state_gla, state_rwkv, state_shift, state_s5_re, state_s5_im, meta, ev_w_in, ev_w_out, a_gate_up, a_gate_b, a_norm_g, b_mu, b_w0, b_w_up, b_a0, b_a_up, b_g_up, b_k_k, b_k_a, b_r_k, b_ln_g, b_ln_b, od_w_in, c_a_re, c_a_im, c_log_dt, c_b_re, c_b_im, c_c_re, c_c_im, c_d, c_w_glu, c_b_glu, od_w_out, w_router, moe_w_up, moe_w_down, ln_mix_g, ln_mix_b, ln_ffn_g, ln_ffn_b):
    bp, sp, d = x_prompt.shape
    bs, ss, _ = x_sample.shape
    lp = sp + N_META
    tp = bp * lp
    t = tp + bs * ss
    pieces = [piece for b in range(bp) for piece in (meta, x_prompt[b])]
    x = jnp.concatenate(pieces + [x_sample.reshape(bs * ss, d)], axis=0)
    xb = x.astype(BF16)

    def ffn(x, x_lin, eid, gates, layer):
        f0, f1 = moe_layer(x_lin, eid, gates, moe_w_up, moe_w_down, layer)
        return add_ln(x, f0, f1, ln_ffn_g[layer], ln_ffn_b[layer], TOK_TM)

    w_in = ev_w_in[0]
    zpad = jnp.zeros((d, P_B_OFF - A_COLS), w_in.dtype)
    w_in = jnp.concatenate([w_in[:, :A_COLS], zpad, w_in[:, A_COLS:]], axis=1).astype(BF16)
    p = matmul(xb, w_in, MM_TM, P_COLS // 4)

    gla_w = (a_gate_up[0], a_gate_b[0], a_norm_g[0])
    oa_p, gla_p = gla_group(p, jnp.zeros((bp, A_HEADS, A_DK, A_DV), F32), *gla_w,
                            row_off=0, nbatch=bp, seq=lp, nb=4, lb=48, chunk=16)
    oa_s, gla_s = gla_group(p, state_gla[0], *gla_w,
                            row_off=tp, nbatch=bs, seq=ss, nb=4, lb=ss, chunk=ss)
    rw_w = (b_mu[0], b_w0[0], b_w_up[0], b_a0[0], b_a_up[0], b_g_up[0], b_k_k[0], b_k_a[0],
            b_r_k[0], b_ln_g[0], b_ln_b[0])
    ob_p, rwkv_p = rwkv_group(p, jnp.zeros((bp, B_HEADS, B_HEAD, B_HEAD), F32),
                              jnp.zeros((bp, B_COLS), F32), *rw_w,
                              row_off=0, nbatch=bp, seq=lp, nb=4, lb=48)
    ob_s, rwkv_s = rwkv_group(p, state_rwkv[0], state_shift[0], *rw_w,
                              row_off=tp, nbatch=bs, seq=ss, nb=8, lb=ss)
    last_p = jnp.arange(bp, dtype=jnp.int32) * lp + (lp - 1)
    last_s = jnp.arange(bs, dtype=jnp.int32) * ss + (tp + ss - 1)
    shift_p = p.at[last_p].get(mode="promise_in_bounds")[:, P_B_OFF:]
    shift_s = p.at[last_s].get(mode="promise_in_bounds")[:, P_B_OFF:]
    o_a = jnp.concatenate([oa_p, oa_s], axis=0)
    o_b = jnp.concatenate([ob_p, ob_s], axis=0)
    x, *routed = proj_ln([o_a, o_b], ev_w_out[0].astype(BF16), x, ln_mix_g[0], ln_mix_b[0], w_router, TOK_TM)
    x, xb = ffn(x, *routed, 0)

    u = matmul(xb, od_w_in[0].astype(BF16), MM_TM, d // 2)
    s5_w = (c_a_re[0], c_a_im[0], c_log_dt[0], c_b_re[0], c_b_im[0], c_c_re[0], c_c_im[0], c_d[0])
    zero_state = jnp.zeros((bp, C_GROUPS, C_STATE), F32)
    y_p, re_p, im_p = s5_group(u, zero_state, zero_state, s5_tables(*s5_w, 16),
                               nbatch=bp, seq=lp, chunk=16)
    y_s, re_s, im_s = s5_group(u[tp:], state_s5_re[0], state_s5_im[0], s5_tables(*s5_w, ss),
                               nbatch=bs, seq=ss, chunk=ss)
    y = jnp.concatenate([y_p, y_s], axis=0)
    z = gelu_glu(y, c_w_glu[0].astype(BF16), c_b_glu[0], TOK_TM)
    x, *routed = proj_ln([z], od_w_out[0].astype(BF16), x, ln_mix_g[1], ln_mix_b[1], w_router, TOK_TM)
    x, xb = ffn(x, *routed, 1)

    y_prompt = jnp.stack([x[b * lp + N_META:(b + 1) * lp] for b in range(bp)])
    y_sample = x[tp:].reshape(bs, ss, d)
    return (y_prompt, y_sample, gla_p[None], gla_s[None], rwkv_p[None], rwkv_s[None],
            shift_p[None], shift_s[None], re_p[None], re_s[None], im_p[None], im_s[None])
```

```python
import functools
import math

import jax
import jax.numpy as jnp
from jax import lax
from jax.experimental import pallas as pl
from jax.experimental.pallas import tpu as pltpu

F32 = jnp.float32
BF16 = jnp.bfloat16

D_MODEL = 2048
DEPTH = 2
N_META = 16

A_WIDTH = 1024
A_HEADS = 4
A_DV = 256
A_DK = 128
A_GATE_RANK = 16
A_GATE_TAU = 16.0
A_QK = A_HEADS * A_DK
A_COLS = 2 * A_QK + 2 * A_WIDTH + A_GATE_RANK

B_WIDTH = 1024
B_HEAD = 64
B_HEADS = 16
B_DECAY_RANK = 64
B_AAA_RANK = 64
B_GATE_RANK = 128
B_COLS = 3 * B_WIDTH + B_DECAY_RANK + B_AAA_RANK + B_GATE_RANK

C_GROUP = 16
C_GROUPS = 128
C_STATE = 64

N_EXPERTS = 16
N_EXPERT_GROUPS = 4
EXPERTS_PER_GROUP = 4
TOP_K = 2
D_EXPERT = 1024

ALPHA = (2.0 * DEPTH) ** 0.25
LN_EPS = 1e-5
HEAD_NORM_EPS = 1e-5
RWKV_GN_EPS = 64e-5

P_B_OFF = B_COLS
P_COLS = 2 * B_COLS

VMEM_LIMIT = 56 * 1024 * 1024
MXU_TILE = 256
LANES = 128


def _cparams(sem):
    return pltpu.CompilerParams(dimension_semantics=sem, vmem_limit_bytes=VMEM_LIMIT)


def _cdiv(a, b):
    return (a + b - 1) // b


def _softplus(x):
    return jnp.maximum(x, 0.0) + jnp.log(1.0 + jnp.exp(-jnp.abs(x)))


def _sigmoid(x):
    return 1.0 / (1.0 + jnp.exp(-x))


def _mm_kernel(x_ref, w_ref, o_ref):
    o_ref[...] = jnp.dot(x_ref[...].astype(BF16), w_ref[...].astype(BF16),
                         preferred_element_type=F32)


def matmul(x, w, tm, tn):
    m, k = x.shape
    n = w.shape[1]
    assert n % tn == 0
    return pl.pallas_call(
        _mm_kernel,
        out_shape=jax.ShapeDtypeStruct((m, n), F32),
        grid=(n // tn, _cdiv(m, tm)),
        in_specs=[pl.BlockSpec((tm, k), lambda j, i: (i, 0)),
                  pl.BlockSpec((k, tn), lambda j, i: (0, j))],
        out_specs=pl.BlockSpec((tm, tn), lambda j, i: (i, j)),
        compiler_params=_cparams(("arbitrary", "arbitrary")),
        name="matmul",
    )(x, w)


def _layer_norm_rows(y, g, b):
    mu = jnp.mean(y, axis=-1, keepdims=True)
    yc = y - mu
    var = jnp.mean(yc * yc, axis=-1, keepdims=True)
    return yc * lax.rsqrt(var + LN_EPS) * g + b


def _proj_ln_kernel(n_lhs, *refs):
    lhs = refs[:n_lhs]
    w_ref, x_ref, g_ref, b_ref, wr_ref, o_ref, ol_ref, e_ref, gt_ref = refs[n_lhs:]
    acc = None
    off = 0
    for r in lhs:
        kk = r.shape[1]
        part = jnp.dot(r[...], w_ref[off:off + kk, :], preferred_element_type=F32)
        acc = part if acc is None else acc + part
        off += kk
    y = _layer_norm_rows(ALPHA * x_ref[...] + acc, g_ref[...], b_ref[...])
    o_ref[...] = y
    tm, d = y.shape
    for s in range(d // LANES):
        ol_ref[pl.ds(s, tm, stride=d // LANES), :] = y[:, s * LANES:(s + 1) * LANES]
    e_ref[...], gt_ref[...] = _route(y, wr_ref[...])


def proj_ln(lhs_list, w_bf16, x, g, b, w_router, tm):
    m, d = x.shape
    kin = w_bf16.shape[0]
    sub = d // LANES
    in_specs = [pl.BlockSpec((tm, l.shape[1]), lambda i: (i, 0)) for l in lhs_list]
    in_specs += [pl.BlockSpec((kin, d), lambda i: (0, 0)),
                 pl.BlockSpec((tm, d), lambda i: (i, 0)),
                 pl.BlockSpec((1, d), lambda i: (0, 0)),
                 pl.BlockSpec((1, d), lambda i: (0, 0)),
                 pl.BlockSpec((N_EXPERTS, d), lambda i: (0, 0))]
    return pl.pallas_call(
        functools.partial(_proj_ln_kernel, len(lhs_list)),
        out_shape=(jax.ShapeDtypeStruct((m, d), F32), jax.ShapeDtypeStruct((m * sub, LANES), F32),
                   jax.ShapeDtypeStruct((TOP_K, m), jnp.int32), jax.ShapeDtypeStruct((TOP_K, m), F32)),
        grid=(_cdiv(m, tm),),
        in_specs=in_specs,
        out_specs=(pl.BlockSpec((tm, d), lambda i: (i, 0)),
                   pl.BlockSpec((tm * sub, LANES), lambda i: (i, 0)),
                   pl.BlockSpec((TOP_K, tm), lambda i: (0, i)),
                   pl.BlockSpec((TOP_K, tm), lambda i: (0, i))),
        compiler_params=_cparams(("arbitrary",)),
        name="proj_ln",
    )(*lhs_list, w_bf16, x, g.reshape(1, d), b.reshape(1, d), w_router.T)


def _add_ln_kernel(x_ref, f0_ref, f1_ref, g_ref, b_ref, o_ref, ob_ref):
    y = _layer_norm_rows(ALPHA * x_ref[...] + (f0_ref[...] + f1_ref[...]), g_ref[...], b_ref[...])
    o_ref[...] = y
    ob_ref[...] = y.astype(BF16)


def add_ln(x, f0, f1, g, b, tm):
    m, d = x.shape
    row = pl.BlockSpec((tm, d), lambda i: (i, 0))
    vec = pl.BlockSpec((1, d), lambda i: (0, 0))
    return pl.pallas_call(
        _add_ln_kernel,
        out_shape=(jax.ShapeDtypeStruct((m, d), F32), jax.ShapeDtypeStruct((m, d), BF16)),
        grid=(_cdiv(m, tm),),
        in_specs=[row, row, row, vec, vec],
        out_specs=(row, row),
        compiler_params=_cparams(("arbitrary",)),
        name="add_ln",
    )(x, f0, f1, g.reshape(1, d), b.reshape(1, d))


def _gla_kernel(nb, lb, chunk, *refs):
    p_refs = refs[:nb]
    (s0_ref, gup_ref, gb_ref, ng_ref) = refs[nb:nb + 4]
    o_ref = refs[nb + 4]
    sout_ref = refs[nb + 5]
    s_ref = refs[nb + 6]
    blk = pl.program_id(1)

    @pl.when(blk == 0)
    def _():
        s_ref[...] = s0_ref[...]

    rows = lax.broadcasted_iota(jnp.int32, (chunk, chunk), 0)
    cols = lax.broadcasted_iota(jnp.int32, (chunk, chunk), 1)
    tril = rows >= cols
    tril_f = tril.astype(F32)
    lgs = []
    for j in range(nb):
        gd = p_refs[j][:, 2 * A_QK + 2 * A_WIDTH:2 * A_QK + 2 * A_WIDTH + A_GATE_RANK]
        z = jnp.dot(gd, gup_ref[...], preferred_element_type=F32) + gb_ref[...]
        lgs.append(-_softplus(-z) * (1.0 / A_GATE_TAU))
    for c in range(lb // chunk):
        r0 = c * chunk
        for j in range(nb):
            p_ref = p_refs[j]
            bc = jnp.dot(tril_f, lgs[j][r0:r0 + chunk, :], preferred_element_type=F32,
                         precision=lax.Precision.HIGHEST)
            for h in range(A_HEADS):
                q = p_ref[r0:r0 + chunk, h * A_DK:(h + 1) * A_DK] * (A_DK ** -0.5)
                k = p_ref[r0:r0 + chunk, A_QK + h * A_DK:A_QK + (h + 1) * A_DK]
                v = p_ref[r0:r0 + chunk, 2 * A_QK + h * A_DV:2 * A_QK + (h + 1) * A_DV]
                rg = p_ref[r0:r0 + chunk,
                           2 * A_QK + A_WIDTH + h * A_DV:2 * A_QK + A_WIDTH + (h + 1) * A_DV]
                b = bc[:, h * A_DK:(h + 1) * A_DK]
                bl = b[chunk - 1:chunk, :]
                qd = (q * jnp.exp(b)).astype(BF16)
                kd = (k * jnp.exp(-b)).astype(BF16)
                vb = v.astype(BF16)
                att = lax.dot_general(qd, kd, (((1,), (1,)), ((), ())), preferred_element_type=F32)
                att = jnp.where(tril, att, 0.0).astype(BF16)
                s = s_ref[j, h]
                o = (jnp.dot(att, vb, preferred_element_type=F32)
                     + jnp.dot(qd, s.astype(BF16), preferred_element_type=F32))
                kl = (k * jnp.exp(bl - b)).astype(BF16)
                kv = lax.dot_general(kl, vb, (((0,), (0,)), ((), ())), preferred_element_type=F32)
                dec = jnp.broadcast_to(jnp.exp(bl), (A_DK, A_DK)).T
                s_ref[j, h] = s * jnp.concatenate([dec] * (A_DV // A_DK), axis=1) + kv
                o = o * lax.rsqrt(jnp.mean(o * o, axis=-1, keepdims=True) + HEAD_NORM_EPS)
                o = o * ng_ref[:, h * A_DV:(h + 1) * A_DV] * (rg * _sigmoid(rg))
                o_ref[j, r0:r0 + chunk, h * A_DV:(h + 1) * A_DV] = o.astype(BF16)

    @pl.when(blk == pl.num_programs(1) - 1)
    def _():
        sout_ref[...] = s_ref[...]


def gla_group(p, s0, gate_up, gate_b, norm_g, *, row_off, nbatch, seq, nb, lb, chunk):
    t = p.shape[0]
    nblk = seq // lb
    assert seq % lb == 0 and lb % chunk == 0 and nbatch % nb == 0 and row_off % lb == 0
    base = row_off // lb

    def p_map(j):
        return lambda bi, blk: (base + (bi * nb + j) * nblk + blk, 0)

    in_specs = [pl.BlockSpec((lb, P_B_OFF), p_map(j)) for j in range(nb)]
    in_specs += [pl.BlockSpec((nb, A_HEADS, A_DK, A_DV), lambda bi, blk: (bi, 0, 0, 0)),
                 pl.BlockSpec((A_GATE_RANK, A_QK), lambda bi, blk: (0, 0)),
                 pl.BlockSpec((1, A_QK), lambda bi, blk: (0, 0)),
                 pl.BlockSpec((1, A_WIDTH), lambda bi, blk: (0, 0))]

    out_specs = [pl.BlockSpec((nb, lb, A_WIDTH), lambda bi, blk: (bi, blk, 0)),
                 pl.BlockSpec((nb, A_HEADS, A_DK, A_DV), lambda bi, blk: (bi, 0, 0, 0))]
    out_shape = [jax.ShapeDtypeStruct((nbatch, seq, A_WIDTH), BF16),
                 jax.ShapeDtypeStruct((nbatch, A_HEADS, A_DK, A_DV), F32)]
    o, s_new = pl.pallas_call(
        functools.partial(_gla_kernel, nb, lb, chunk),
        out_shape=out_shape,
        grid=(nbatch // nb, nblk),
        in_specs=in_specs,
        out_specs=out_specs,
        scratch_shapes=[pltpu.VMEM((nb, A_HEADS, A_DK, A_DV), F32)],
        compiler_params=_cparams(("arbitrary", "arbitrary")),
        name="gla",
    )(*([p] * nb), s0, gate_up, gate_b.reshape(1, A_QK), norm_g.reshape(1, A_WIDTH))
    return o.reshape(nbatch * seq, A_WIDTH), s_new


VT_SLOT = 128
RWKV_UNROLL = 8


def _head_ones(dtype):
    r = lax.broadcasted_iota(jnp.int32, (MXU_TILE, MXU_TILE), 0) // B_HEAD
    c = lax.broadcasted_iota(jnp.int32, (MXU_TILE, MXU_TILE), 1) // B_HEAD
    return (r == c).astype(dtype)


def _head_sum(x, ones):
    xh = x.astype(BF16)
    xl = (x - xh.astype(F32)).astype(BF16)
    parts = [jnp.dot(xh[:, c * MXU_TILE:(c + 1) * MXU_TILE], ones, preferred_element_type=F32)
             + jnp.dot(xl[:, c * MXU_TILE:(c + 1) * MXU_TILE], ones, preferred_element_type=F32)
             for c in range(B_WIDTH // MXU_TILE)]
    return jnp.concatenate(parts, axis=-1)


def _rwkv_kernel(nb, lb, *refs):
    p_refs = refs[:nb]
    (shift0_ref, s0_ref, mu_ref, w0_ref, wup_ref, a0_ref, aup_ref, gup_ref,
     kk_ref, ka_ref, rk_ref, lng_ref, lnb_ref) = refs[nb:nb + 13]
    o_ref, sout_ref = refs[nb + 13:nb + 15]
    (s_ref, prev_ref, r_s, w_s, kh_s, kl_s, kk_s, kka_s, on_s, g_s, bon_s, vmh_s, vml_s, vt_s
     ) = refs[nb + 15:]
    blk = pl.program_id(1)
    W = B_WIDTH
    H = B_HEADS
    NT = (((1,), (1,)), ((), ()))

    @pl.when(blk == 0)
    def _():
        s_ref[...] = s0_ref[...]
        prev_ref[...] = shift0_ref[...]

    ones = _head_ones(BF16)
    hmask = (lax.broadcasted_iota(jnp.int32, (H, W), 1) // B_HEAD
             == lax.broadcasted_iota(jnp.int32, (H, W), 0))

    def to_head_rows(ref, j, x):
        ref[j] = jnp.zeros(ref.shape[1:], F32)
        for h in range(H):
            ref[j, pl.ds(h, lb, stride=H), 0:B_HEAD] = x[:, h * B_HEAD:(h + 1) * B_HEAD]

    def split(x):
        xh = x.astype(BF16).astype(F32)
        return xh, x - xh

    xms = []
    for j in range(nb):
        pb = p_refs[j][...]
        first = lax.broadcasted_iota(jnp.int32, (lb, 1), 0) == 0
        prev = jnp.where(first, prev_ref[j], pltpu.roll(pb, 1, axis=0))
        prev_ref[j] = pb[lb - 1:lb, :]
        xms.append(pb + (prev - pb) * mu_ref[...])
    xm = jnp.concatenate(xms, axis=0)
    r = xm[:, :W]
    k = xm[:, W:2 * W]
    v = xm[:, 2 * W:3 * W]
    wd = xm[:, 3 * W:3 * W + B_DECAY_RANK]
    ad = xm[:, 3 * W + B_DECAY_RANK:3 * W + B_DECAY_RANK + B_AAA_RANK]
    gd = xm[:, 3 * W + B_DECAY_RANK + B_AAA_RANK:]
    w = -_softplus(-(w0_ref[...] + jnp.dot(jnp.tanh(wd).astype(BF16), wup_ref[...].astype(BF16),
                                           preferred_element_type=F32))) - 0.5
    a = _sigmoid(a0_ref[...] + jnp.dot(ad.astype(BF16), aup_ref[...].astype(BF16),
                                       preferred_element_type=F32))
    g = jnp.dot(_sigmoid(gd).astype(BF16), gup_ref[...].astype(BF16), preferred_element_type=F32)
    kk = k * kk_ref[...]
    nrm = jnp.sqrt(_head_sum(kk * kk, ones))
    kk = kk / jnp.maximum(nrm, 1e-12)
    k2 = k * (1.0 + (a - 1.0) * ka_ref[...])
    per_seq = lambda x: x.reshape(nb, lb, x.shape[-1])
    r_s[...] = per_seq(r)
    w_s[...] = per_seq(jnp.exp(-jnp.exp(w)))
    kh, kl = split(k2)
    kh_s[...], kl_s[...] = per_seq(kh), per_seq(kl)
    kk_s[...] = per_seq(kk)
    kka_s[...] = per_seq(-(kk * a))
    vh, vl = split(v)
    for j in range(nb):
        to_head_rows(vmh_s, j, vh[j * lb:(j + 1) * lb])
        to_head_rows(vml_s, j, vl[j * lb:(j + 1) * lb])
    g_s[...] = per_seq(g)
    bon_s[...] = per_seq(_head_sum(r * k2 * rk_ref[...], ones) * v)

    def head_rows(ref, j, t):
        return jnp.where(hmask, jnp.broadcast_to(ref[j, pl.ds(t, 1), :], (H, W)), 0.0).astype(BF16)

    def readout(j, t, sb):
        rows = pl.ds(t * H if isinstance(t, int) else pl.multiple_of(t * H, H), H)
        on_s[j, rows, :] = lax.dot_general(head_rows(r_s, j, t), sb, NT, preferred_element_type=F32)

    zrows = lambda n: jnp.zeros((n * H, W), BF16)
    lane = lax.broadcasted_iota(jnp.int32, (B_HEAD, VT_SLOT), 1)

    def vt_step(t, carry):
        rows = pl.ds(pl.multiple_of(t * H, H), H)
        for j in range(nb):
            vmh, vml = vmh_s[j, rows, :], vml_s[j, rows, :]
            tile = jnp.concatenate([vmh, vml, vmh, jnp.zeros((VT_SLOT - 3 * H, VT_SLOT), F32)], axis=0)
            vt_s[j, t] = tile.T[:B_HEAD, :].astype(BF16)
        return carry

    lax.fori_loop(0, lb, vt_step, 0, unroll=RWKV_UNROLL)

    def step(t, carry):
        tp = jnp.maximum(t - 1, 0)
        prods = []
        for j in range(nb):
            sb = s_ref[j].astype(BF16)
            prods.append(lax.dot_general(
                jnp.concatenate([head_rows(kk_s, j, t), head_rows(r_s, j, tp)], axis=0), sb, NT,
                preferred_element_type=F32))
        pad_v = jnp.zeros((H, VT_SLOT - B_HEAD), F32)
        zblk = jnp.zeros((3 * H, VT_SLOT), F32)
        for j in range(nb):
            on_s[j, pl.ds(pl.multiple_of(tp * H, H), H), :] = prods[j][H:]
            sat = jnp.concatenate([prods[j][:H], pad_v], axis=1)
            sa = jnp.concatenate([zblk, sat, sat, zblk], axis=0).T[:B_HEAD, :]
            sa_hi = sa.astype(BF16)
            sa_lo = (sa - sa_hi.astype(F32)).astype(BF16)
            x = jnp.where(lane < 3 * H, vt_s[j, t], jnp.where(lane < 4 * H, sa_hi, sa_lo))
            khw, klw, kaw = head_rows(kh_s, j, t), head_rows(kl_s, j, t), head_rows(kka_s, j, t)
            upd = jnp.dot(x[:, :5 * H], jnp.concatenate([khw, khw, klw, kaw, kaw], axis=0),
                          preferred_element_type=F32)
            s_ref[j] = s_ref[j] * w_s[j, pl.ds(t, 1), :] + upd
        return carry

    lax.fori_loop(0, lb, step, 0, unroll=RWKV_UNROLL)

    os = []
    for j in range(nb):
        readout(j, lb - 1, s_ref[j].astype(BF16))
        os.append(jnp.concatenate([on_s[j, pl.ds(h, lb, stride=H), :] for h in range(H)], axis=-1))
    o = jnp.concatenate(os, axis=0)
    oc = o - _head_sum(o, ones) * (1.0 / B_HEAD)
    var = _head_sum(oc * oc, ones) * (1.0 / B_HEAD)
    o = per_seq(oc * lax.rsqrt(var + RWKV_GN_EPS) * lng_ref[...] + lnb_ref[...])
    o_ref[...] = ((o + bon_s[...]) * g_s[...]).astype(BF16)

    @pl.when(blk == pl.num_programs(1) - 1)
    def _():
        sout_ref[...] = s_ref[...]


def rwkv_group(p, s0, shift0, mu, w0, w_up, a0, a_up, g_up, k_k, k_a, r_k, ln_g, ln_b,
               *, row_off, nbatch, seq, nb, lb):
    nblk = seq // lb
    assert seq % lb == 0 and nbatch % nb == 0 and row_off % lb == 0
    base = row_off // lb
    W = B_WIDTH
    s0 = s0.transpose(0, 2, 1, 3).reshape(nbatch, B_HEAD, W)

    def p_map(j):
        return lambda bi, blk: (base + (bi * nb + j) * nblk + blk, 1)

    const = lambda shape: pl.BlockSpec(shape, lambda bi, blk: (0,) * len(shape))
    in_specs = [pl.BlockSpec((lb, B_COLS), p_map(j)) for j in range(nb)]
    in_specs += [pl.BlockSpec((nb, 1, B_COLS), lambda bi, blk: (bi, 0, 0)),
                 pl.BlockSpec((nb, B_HEAD, W), lambda bi, blk: (bi, 0, 0)),
                 const((1, B_COLS)), const((1, W)), const((B_DECAY_RANK, W)), const((1, W)),
                 const((B_AAA_RANK, W)), const((B_GATE_RANK, W)),
                 const((1, W)), const((1, W)), const((1, W)), const((1, W)), const((1, W))]
    out_specs = [pl.BlockSpec((nb, lb, W), lambda bi, blk: (bi, blk, 0)),
                 pl.BlockSpec((nb, B_HEAD, W), lambda bi, blk: (bi, 0, 0))]
    out_shape = [jax.ShapeDtypeStruct((nbatch, seq, W), BF16),
                 jax.ShapeDtypeStruct((nbatch, B_HEAD, W), F32)]
    tok = pltpu.VMEM((nb, lb, W), F32)
    o, s_new = pl.pallas_call(
        functools.partial(_rwkv_kernel, nb, lb),
        out_shape=out_shape,
        grid=(nbatch // nb, nblk),
        in_specs=in_specs,
        out_specs=out_specs,
        scratch_shapes=[pltpu.VMEM((nb, B_HEAD, W), F32), pltpu.VMEM((nb, 1, B_COLS), F32)]
        + [tok] * 6 + [pltpu.VMEM((nb, lb * B_HEADS, B_HEAD), F32), tok, tok,
                       pltpu.VMEM((nb, lb * B_HEADS, VT_SLOT), F32),
                       pltpu.VMEM((nb, lb * B_HEADS, VT_SLOT), F32),
                       pltpu.VMEM((nb, lb, B_HEAD, VT_SLOT), BF16)],
        compiler_params=_cparams(("arbitrary", "arbitrary")),
        name="rwkv",
    )(*([p] * nb), shift0.reshape(nbatch, 1, B_COLS), s0, mu.reshape(1, B_COLS),
      w0.reshape(1, W), w_up, a0.reshape(1, W), a_up, g_up, k_k.reshape(1, W),
      k_a.reshape(1, W), r_k.reshape(1, W), ln_g.reshape(1, W), ln_b.reshape(1, W))
    s_new = s_new.reshape(nbatch, B_HEAD, B_HEADS, B_HEAD).transpose(0, 2, 1, 3)
    return o.reshape(nbatch * seq, W), s_new


S5_GT = LANES // C_GROUP
SP = C_GROUPS * C_STATE


def s5_tables(a_re, a_im, log_dt, b_re, b_im, c_re, c_im, d, chunk):
    G, P, c = C_GROUPS, C_STATE, C_GROUP
    dt = jnp.exp(log_dt)[:, None]
    m = jnp.arange(chunk + 1, dtype=F32)[:, None, None]
    mag = jnp.exp(m * (dt * a_re))
    ang = m * (dt * a_im)
    pw_re, pw_im = mag * jnp.cos(ang), mag * jnp.sin(ang)
    num_re, num_im = pw_re[1] - 1.0, pw_im[1]
    den = a_re * a_re + a_im * a_im
    q_re = (num_re * a_re + num_im * a_im) / den
    q_im = (num_im * a_re - num_re * a_im) / den
    bb_re = q_re[..., None] * b_re - q_im[..., None] * b_im
    bb_im = q_re[..., None] * b_im + q_im[..., None] * b_re
    ca_re = c_re[None] * pw_re[:, :, None, :] - c_im[None] * pw_im[:, :, None, :]
    ca_im = c_re[None] * pw_im[:, :, None, :] + c_im[None] * pw_re[:, :, None, :]
    cm = lambda z: z[:chunk].transpose(1, 3, 0, 2).reshape(G, P, chunk * c)
    toep = s5_toeplitz(bb_re.transpose(0, 2, 1), bb_im.transpose(0, 2, 1), cm(ca_re), cm(ca_im), chunk)
    dvec = jnp.tile(d.reshape(G, 1, c), (1, chunk, 1)).reshape(G, 1, chunk * c)
    rev_re, rev_im = pw_re[:chunk][::-1], pw_im[:chunk][::-1]
    bs_re = rev_re[..., None] * bb_re[None] - rev_im[..., None] * bb_im[None]
    bs_im = rev_re[..., None] * bb_im[None] + rev_im[..., None] * bb_re[None]
    to_in = lambda z: z.transpose(1, 0, 3, 2).reshape(G, chunk * c, P)
    to_out = lambda z: z.transpose(1, 3, 0, 2).reshape(G, P, chunk * c)
    cs_re, cs_im = to_out(ca_re[1:]), to_out(-ca_im[1:])
    return dict(toep=toep, dvec=dvec,
                bs_re=to_in(bs_re).astype(BF16), bs_im=to_in(bs_im).astype(BF16),
                cs_re=cs_re.astype(BF16), cs_im=cs_im.astype(BF16),
                ac_re=pw_re[chunk].reshape(1, SP), ac_im=pw_im[chunk].reshape(1, SP))


def _s5_toeplitz_kernel(chunk, bre_ref, bim_ref, cre_ref, cim_ref, o_ref):
    c = C_GROUP
    cr = chunk * c
    hi = lax.Precision.HIGHEST
    lane = lax.broadcasted_iota(jnp.int32, (c, cr), 1)
    for q in range(S5_GT):
        k0 = (jnp.dot(bre_ref[q], cre_ref[q], preferred_element_type=F32, precision=hi)
              - jnp.dot(bim_ref[q], cim_ref[q], preferred_element_type=F32, precision=hi))
        for j in range(chunk):
            blk = k0 if j == 0 else jnp.where(lane >= j * c, pltpu.roll(k0, j * c, axis=1), 0.0)
            o_ref[q, j * c:(j + 1) * c, :] = blk.astype(BF16)


def s5_toeplitz(bbt_re, bbt_im, cm_re, cm_im, chunk):
    G, P, c = C_GROUPS, C_STATE, C_GROUP
    cr = chunk * c
    grp = lambda shape: pl.BlockSpec((S5_GT,) + shape, lambda g: (g, 0, 0))
    return pl.pallas_call(
        functools.partial(_s5_toeplitz_kernel, chunk),
        out_shape=jax.ShapeDtypeStruct((G, cr, cr), BF16),
        grid=(G // S5_GT,),
        in_specs=[grp((c, P)), grp((c, P)), grp((P, cr)), grp((P, cr))],
        out_specs=grp((cr, cr)),
        compiler_params=_cparams(("arbitrary",)),
        name="s5_toeplitz",
    )(bbt_re, bbt_im, cm_re, cm_im)


def _swap_blocks(xs, blk):
    n = len(xs)
    w = xs[0].shape[-1] // n
    s = n // 2
    while s >= 1:
        nxt = list(xs)
        for i in range(n):
            if i & s == 0:
                j = i + s
                low = (blk & s) == 0
                nxt[i] = jnp.where(low, xs[i], pltpu.roll(xs[j], s * w, axis=1))
                nxt[j] = jnp.where(low, pltpu.roll(xs[i], (n - s) * w, axis=1), xs[j])
        xs = nxt
        s //= 2
    return xs


def _s5_kernel(nbatch, nchunks, chunk, u_ref, toep_ref, bre_ref, bim_ref, cre_ref, cim_ref, dvec_ref,
               acre_ref, acim_ref, s0re_ref, s0im_ref, y_ref, fre_ref, fim_ref,
               uf_s, vre_s, vim_s, sre_s, sim_s):
    c, P, GT = C_GROUP, C_STATE, S5_GT
    R = nbatch * nchunks
    cr = chunk * c
    halves = cr // LANES
    per_half = LANES // c
    blk = lax.broadcasted_iota(jnp.int32, (R, LANES), 1) // c

    for h in range(halves):
        xs = _swap_blocks([u_ref[pl.ds(h * per_half + q, R, stride=chunk), :] for q in range(per_half)], blk)
        for g in range(GT):
            uf_s[g, :, h * LANES:(h + 1) * LANES] = xs[g]

    for g in range(GT):
        ub = uf_s[g].astype(BF16)
        vre_s[:, g * P:(g + 1) * P] = jnp.dot(ub, bre_ref[g], preferred_element_type=F32)
        vim_s[:, g * P:(g + 1) * P] = jnp.dot(ub, bim_ref[g], preferred_element_type=F32)

    ar, ai = acre_ref[...], acim_ref[...]
    if nchunks == 1:
        sre_s[...] = s0re_ref[...]
        sim_s[...] = s0im_ref[...]
        sr, si = s0re_ref[...], s0im_ref[...]
        fre_ref[...] = ar * sr - ai * si + vre_s[...]
        fim_ref[...] = ar * si + ai * sr + vim_s[...]
    else:
        def step(ci, carry):
            sr, si = carry
            vr = jnp.concatenate([vre_s[pl.ds(b * nchunks + ci, 1), :] for b in range(nbatch)], axis=0)
            vi = jnp.concatenate([vim_s[pl.ds(b * nchunks + ci, 1), :] for b in range(nbatch)], axis=0)
            for b in range(nbatch):
                sre_s[pl.ds(b * nchunks + ci, 1), :] = sr[b:b + 1]
                sim_s[pl.ds(b * nchunks + ci, 1), :] = si[b:b + 1]
            return ar * sr - ai * si + vr, ar * si + ai * sr + vi

        sr, si = lax.fori_loop(0, nchunks, step, (s0re_ref[...], s0im_ref[...]))
        fre_ref[...] = sr
        fim_ref[...] = si

    for g in range(GT):
        u = uf_s[g]
        y = (jnp.dot(u.astype(BF16), toep_ref[g], preferred_element_type=F32)
             + jnp.dot(sre_s[:, g * P:(g + 1) * P].astype(BF16), cre_ref[g], preferred_element_type=F32)
             + jnp.dot(sim_s[:, g * P:(g + 1) * P].astype(BF16), cim_ref[g], preferred_element_type=F32))
        uf_s[g] = y + dvec_ref[g] * u
    for h in range(halves):
        ys = _swap_blocks([uf_s[g, :, h * LANES:(h + 1) * LANES] for g in range(GT)], blk)
        for q in range(per_half):
            y_ref[pl.ds(h * per_half + q, R, stride=chunk), :] = ys[q]


def s5_group(u, s0_re, s0_im, tab, *, nbatch, seq, chunk):
    G, P, c, GT = C_GROUPS, C_STATE, C_GROUP, S5_GT
    nchunks = seq // chunk
    assert seq % chunk == 0 and u.shape[0] >= nbatch * seq and u.shape[1] == G * c
    R = nbatch * nchunks
    cr = chunk * c
    rows = nbatch * seq
    grp = lambda shape: pl.BlockSpec((GT,) + shape, lambda i: (i, 0, 0))
    lanes = lambda r, w: pl.BlockSpec((r, w), lambda i: (0, i))
    y, f_re, f_im = pl.pallas_call(
        functools.partial(_s5_kernel, nbatch, nchunks, chunk),
        out_shape=[jax.ShapeDtypeStruct((rows, G * c), F32)] + [jax.ShapeDtypeStruct((nbatch, SP), F32)] * 2,
        grid=(G // GT,),
        in_specs=[lanes(rows, GT * c), grp((cr, cr)), grp((cr, P)), grp((cr, P)), grp((P, cr)), grp((P, cr)),
                  grp((1, cr)), lanes(1, GT * P), lanes(1, GT * P), lanes(nbatch, GT * P), lanes(nbatch, GT * P)],
        out_specs=[lanes(rows, GT * c), lanes(nbatch, GT * P), lanes(nbatch, GT * P)],
        scratch_shapes=[pltpu.VMEM((GT, R, cr), F32)] + [pltpu.VMEM((R, GT * P), F32)] * 4,
        compiler_params=_cparams(("arbitrary",)),
        name="s5",
    )(u, tab['toep'], tab['bs_re'], tab['bs_im'], tab['cs_re'], tab['cs_im'], tab['dvec'],
      tab['ac_re'], tab['ac_im'], s0_re.reshape(nbatch, SP), s0_im.reshape(nbatch, SP))
    return y, f_re.reshape(nbatch, G, P), f_im.reshape(nbatch, G, P)


def _glu_kernel(y_ref, w_ref, b_ref, o_ref):
    z = jax.nn.gelu(y_ref[...])
    acc = jnp.dot(z.astype(BF16), w_ref[...], preferred_element_type=F32) + b_ref[...]
    o_ref[...] = (z * _sigmoid(acc)).astype(BF16)


def gelu_glu(y, w_bf16, b, tm):
    m, k = y.shape
    n = w_bf16.shape[1]
    assert n == k
    return pl.pallas_call(
        _glu_kernel,
        out_shape=jax.ShapeDtypeStruct((m, n), BF16),
        grid=(_cdiv(m, tm),),
        in_specs=[pl.BlockSpec((tm, k), lambda i: (i, 0)),
                  pl.BlockSpec((k, n), lambda i: (0, 0)),
                  pl.BlockSpec((1, n), lambda i: (0, 0))],
        out_specs=pl.BlockSpec((tm, n), lambda i: (i, 0)),
        compiler_params=_cparams(("arbitrary",)),
        name="gelu_glu",
    )(y, w_bf16, b.reshape(1, n))


MOE_TM = 512
MOE_TF = 1024
MOE_TN = 2048
ROW_GROUP = 8


def _route(x, wr):
    nt = (((1,), (1,)), ((), ()))
    xh, wh = x.astype(BF16), wr.astype(BF16)
    xl, wl = (x - xh.astype(F32)).astype(BF16), (wr - wh.astype(F32)).astype(BF16)
    logits = (lax.dot_general(wh, xh, nt, preferred_element_type=F32)
              + lax.dot_general(wh, xl, nt, preferred_element_type=F32)
              + lax.dot_general(wl, xh, nt, preferred_element_type=F32))
    mx = jnp.max(logits, axis=0, keepdims=True)
    ex = jnp.exp(logits - mx)
    probs = ex / jnp.sum(ex, axis=0, keepdims=True)
    neg = jnp.float32(-jnp.inf)
    best = None
    for gi in range(N_EXPERT_GROUPS):
        v = [probs[gi * EXPERTS_PER_GROUP + r:gi * EXPERTS_PER_GROUP + r + 1, :]
             for r in range(EXPERTS_PER_GROUP)]
        m1 = jnp.maximum(jnp.maximum(v[0], v[1]), jnp.maximum(v[2], v[3]))
        i1 = jnp.where(v[0] == m1, 0, jnp.where(v[1] == m1, 1, jnp.where(v[2] == m1, 2, 3)))
        w = [jnp.where(i1 == r, neg, v[r]) for r in range(EXPERTS_PER_GROUP)]
        m2 = jnp.maximum(jnp.maximum(w[0], w[1]), jnp.maximum(w[2], w[3]))
        i2 = jnp.where(w[0] == m2, 0, jnp.where(w[1] == m2, 1, jnp.where(w[2] == m2, 2, 3)))
        score = m1 + m2
        cand = (score, m1, m2, i1 + gi * EXPERTS_PER_GROUP, i2 + gi * EXPERTS_PER_GROUP)
        if best is None:
            best = cand
        else:
            take = cand[0] > best[0]
            best = tuple(jnp.where(take, cn, bs) for cn, bs in zip(cand, best))
    _, m1, m2, e1, e2 = best
    tot = m1 + m2
    return jnp.concatenate([e1, e2], axis=0), jnp.concatenate([m1 / tot, m2 / tot], axis=0)


def _moe_up_kernel(rs_ref, tr_ref, st_ref, sf_ref, se_ref, nu_ref, x_hbm, w1_ref, w2_ref, gate_ref, h_ref,
                   xbuf0, xbuf1, sem):
    s = pl.program_id(0)
    n = pl.num_programs(0)
    sub = D_MODEL // LANES
    tm = xbuf0.shape[0] // sub
    bufs = (xbuf0, xbuf1)

    def fetched(step):
        rows = tr_ref[st_ref[jnp.minimum(step, n - 1)]]
        return jnp.where(step < n, (rows + ROW_GROUP - 1) // ROW_GROUP * ROW_GROUP, 0)

    def row_copy(step, slot, r):
        src = rs_ref[st_ref[step] * tm + r]
        return pltpu.make_async_copy(x_hbm.at[pl.ds(src * sub, sub), :],
                                     bufs[slot].at[pl.ds(r * sub, sub), :], sem.at[slot])

    def for_rows(step, slot, count, fn):
        def body(i, carry):
            for q in range(ROW_GROUP):
                fn(row_copy(step, slot, i * ROW_GROUP + q))
            return carry
        lax.fori_loop(0, count // ROW_GROUP, body, 0)

    def tile(slot):
        return jnp.concatenate([bufs[slot][pl.ds(c, tm, stride=sub), :] for c in range(sub)], axis=-1)

    def up(slot):
        x = tile(slot).astype(BF16)
        a = jnp.dot(x, w1_ref[0, 0].astype(BF16), preferred_element_type=F32)
        b = jnp.dot(x, w2_ref[0, 0].astype(BF16), preferred_element_type=F32)
        own = lax.broadcasted_iota(jnp.int32, (tm, 1), 0) < tr_ref[st_ref[s]]
        h_ref[...] = jnp.where(own, a * _sigmoid(a) * b * gate_ref[...], 0.0).astype(BF16)

    @pl.when(s == 0)
    def _():
        xbuf0[...] = jnp.zeros(xbuf0.shape, F32)
        xbuf1[...] = jnp.zeros(xbuf1.shape, F32)
        for_rows(0, 0, fetched(0), lambda cp: cp.start())

    def step(slot):
        for_rows(s, slot, fetched(s), lambda cp: cp.wait())
        nxt_rows = fetched(s + 1)

        @pl.when(nxt_rows == tm)
        def _():
            for r in range(tm):
                row_copy(s + 1, 1 - slot, r).start(priority=r % 2)
            up(slot)

        @pl.when(nxt_rows < tm)
        def _():
            for_rows(jnp.minimum(s + 1, n - 1), 1 - slot, nxt_rows, lambda cp: cp.start())
            pl.when(s < nu_ref[0])(functools.partial(up, slot))

            @pl.when(s >= nu_ref[0])
            def _():
                h_ref[...] = jnp.zeros(h_ref.shape, h_ref.dtype)

    for slot in range(2):
        pl.when(s % 2 == slot)(functools.partial(step, slot))


def _moe_down_kernel(st_ref, sf_ref, se_ref, nu_ref, h_ref, w_ref, y_ref):
    @pl.when(pl.program_id(0) < nu_ref[0])
    def _():
        y_ref[...] = jnp.dot(h_ref[...], w_ref[0, 0].astype(BF16), preferred_element_type=F32)

    @pl.when(pl.program_id(0) >= nu_ref[0])
    def _():
        y_ref[...] = jnp.zeros(y_ref.shape, y_ref.dtype)


def _moe_schedule(tiles_e, tile_start, nblocks, n_tiles):
    steps_e = tiles_e * nblocks
    cum = jnp.cumsum(steps_e)
    used = cum[-1]
    s = jnp.arange(n_tiles * nblocks, dtype=jnp.int32)
    sc = jnp.minimum(s, used - 1)
    e = jnp.minimum(jnp.sum((cum[None, :] <= sc[:, None]).astype(jnp.int32), axis=1), N_EXPERTS - 1)
    local = sc - (cum[e] - steps_e[e])
    te = jnp.maximum(tiles_e[e], 1)
    rest = s - used
    tile = jnp.where(s < used, tile_start[e] + local % te, jnp.sum(tiles_e) + rest // nblocks)
    blk = jnp.where(s < used, local // te, rest % nblocks)
    return tile.astype(jnp.int32), blk.astype(jnp.int32), e, used.reshape(1).astype(jnp.int32)


def moe_layer(x_lin, eid, gates, w_up, w_down, layer):
    d = D_MODEL
    t = eid.shape[1]
    tm = MOE_TM
    eid = eid.reshape(-1)
    na = TOP_K * t
    n_tiles = _cdiv(na + N_EXPERTS * (tm - 1), tm)
    npad = n_tiles * tm
    onehot = (eid[None, :] == jnp.arange(N_EXPERTS, dtype=jnp.int32)[:, None]).astype(jnp.int32)
    csum = jnp.cumsum(onehot, axis=1)
    counts = csum[:, -1]
    rank = jnp.sum(csum * onehot, axis=0) - 1
    tiles_e = (counts + tm - 1) // tm
    tile_start = jnp.cumsum(tiles_e) - tiles_e
    pos = tile_start[eid] * tm + rank
    row_a = jnp.full((npad,), -1, jnp.int32).at[pos].set(
        jnp.arange(na, dtype=jnp.int32), mode="promise_in_bounds", unique_indices=True)
    a_c = jnp.maximum(row_a, 0)
    row_src = jnp.where(a_c >= t, a_c - t, a_c)
    row_gate = jnp.where(row_a >= 0, gates.reshape(-1).at[a_c].get(mode="promise_in_bounds"), 0.0)

    ti = jnp.arange(n_tiles, dtype=jnp.int32)
    te = jnp.minimum(jnp.sum((jnp.cumsum(tiles_e)[None, :] <= ti[:, None]).astype(jnp.int32), axis=1),
                     N_EXPERTS - 1)
    tile_rows = jnp.where(ti < jnp.sum(tiles_e),
                          jnp.clip(counts[te] - (ti - tile_start[te]) * tm, 0, tm), 0).astype(jnp.int32)

    nf = D_EXPERT // MOE_TF
    st, sf, se, nu = _moe_schedule(tiles_e, tile_start, nf, n_tiles)
    h = pl.pallas_call(
        _moe_up_kernel,
        out_shape=jax.ShapeDtypeStruct((npad, D_EXPERT), BF16),
        grid_spec=pltpu.PrefetchScalarGridSpec(
            num_scalar_prefetch=6,
            grid=(n_tiles * nf,),
            in_specs=[pl.BlockSpec(memory_space=pl.ANY),
                      pl.BlockSpec((1, 1, d, MOE_TF),
                                   lambda s, rs, tr, st, sf, se, nu: (layer, se[s], 0, sf[s])),
                      pl.BlockSpec((1, 1, d, MOE_TF),
                                   lambda s, rs, tr, st, sf, se, nu: (layer, se[s], 0, nf + sf[s])),
                      pl.BlockSpec((tm, 1), lambda s, rs, tr, st, sf, se, nu: (st[s], 0))],
            out_specs=pl.BlockSpec((tm, MOE_TF), lambda s, rs, tr, st, sf, se, nu: (st[s], sf[s])),
            scratch_shapes=[pltpu.VMEM((tm * (d // LANES), LANES), F32)] * 2
            + [pltpu.SemaphoreType.DMA((2,))]),
        compiler_params=_cparams(("arbitrary",)),
        name="moe_up",
    )(row_src, tile_rows, st, sf, se, nu, x_lin, w_up, w_up, row_gate.reshape(npad, 1))

    nn = d // MOE_TN
    st, sf, se, nu = _moe_schedule(tiles_e, tile_start, nn, n_tiles)
    ys = pl.pallas_call(
        _moe_down_kernel,
        out_shape=jax.ShapeDtypeStruct((npad, d), F32),
        grid_spec=pltpu.PrefetchScalarGridSpec(
            num_scalar_prefetch=4,
            grid=(n_tiles * nn,),
            in_specs=[pl.BlockSpec((tm, D_EXPERT), lambda s, st, sf, se, nu: (st[s], 0)),
                      pl.BlockSpec((1, 1, D_EXPERT, MOE_TN),
                                   lambda s, st, sf, se, nu: (layer, se[s], 0, sf[s]))],
            out_specs=pl.BlockSpec((tm, MOE_TN), lambda s, st, sf, se, nu: (st[s], sf[s]))),
        compiler_params=_cparams(("arbitrary",)),
        name="moe_down",
    )(st, sf, se, nu, h, w_down)
    return (ys.at[pos[:t]].get(mode="promise_in_bounds", unique_indices=True),
            ys.at[pos[t:]].get(mode="promise_in_bounds", unique_indices=True))


TOK_TM = 512
MM_TM = 1024


def kernel(x_prompt, x_sample, state_gla, state_rwkv, state_shift, state_s5_re, state_s5_im, meta, ev_w_in, ev_w_out, a_gate_up, a_gate_b, a_norm_g, b_mu, b_w0, b_w_up, b_a0, b_a_up, b_g_up, b_k_k, b_k_a, b_r_k, b_ln_g, b_ln_b, od_w_in, c_a_re, c_a_im, c_log_dt, c_b_re, c_b_im, c_c_re, c_c_im, c_d, c_w_glu, c_b_glu, od_w_out, w_router, moe_w_up, moe_w_down, ln_mix_g, ln_mix_b, ln_ffn_g, ln_ffn_b):
    bp, sp, d = x_prompt.shape
    bs, ss, _ = x_sample.shape
    lp = sp + N_META
    tp = bp * lp
    t = tp + bs * ss
    pieces = [piece for b in range(bp) for piece in (meta, x_prompt[b])]
    x = jnp.concatenate(pieces + [x_sample.reshape(bs * ss, d)], axis=0)
    xb = x.astype(BF16)

    def ffn(x, x_lin, eid, gates, layer):
        f0, f1 = moe_layer(x_lin, eid, gates, moe_w_up, moe_w_down, layer)
        return add_ln(x, f0, f1, ln_ffn_g[layer], ln_ffn_b[layer], TOK_TM)

    w_in = ev_w_in[0]
    zpad = jnp.zeros((d, P_B_OFF - A_COLS), w_in.dtype)
    w_in = jnp.concatenate([w_in[:, :A_COLS], zpad, w_in[:, A_COLS:]], axis=1).astype(BF16)
    p = matmul(xb, w_in, MM_TM, P_COLS // 4)

    gla_w = (a_gate_up[0], a_gate_b[0], a_norm_g[0])
    oa_p, gla_p = gla_group(p, jnp.zeros((bp, A_HEADS, A_DK, A_DV), F32), *gla_w,
                            row_off=0, nbatch=bp, seq=lp, nb=4, lb=48, chunk=16)
    oa_s, gla_s = gla_group(p, state_gla[0], *gla_w,
                            row_off=tp, nbatch=bs, seq=ss, nb=8, lb=ss, chunk=ss)
    rw_w = (b_mu[0], b_w0[0], b_w_up[0], b_a0[0], b_a_up[0], b_g_up[0], b_k_k[0], b_k_a[0],
            b_r_k[0], b_ln_g[0], b_ln_b[0])
    ob_p, rwkv_p = rwkv_group(p, jnp.zeros((bp, B_HEADS, B_HEAD, B_HEAD), F32),
                              jnp.zeros((bp, B_COLS), F32), *rw_w,
                              row_off=0, nbatch=bp, seq=lp, nb=4, lb=48)
    ob_s, rwkv_s = rwkv_group(p, state_rwkv[0], state_shift[0], *rw_w,
                              row_off=tp, nbatch=bs, seq=ss, nb=8, lb=ss)
    last_p = jnp.arange(bp, dtype=jnp.int32) * lp + (lp - 1)
    last_s = jnp.arange(bs, dtype=jnp.int32) * ss + (tp + ss - 1)
    shift_p = p.at[last_p].get(mode="promise_in_bounds")[:, P_B_OFF:]
    shift_s = p.at[last_s].get(mode="promise_in_bounds")[:, P_B_OFF:]
    o_a = jnp.concatenate([oa_p, oa_s], axis=0)
    o_b = jnp.concatenate([ob_p, ob_s], axis=0)
    x, *routed = proj_ln([o_a, o_b], ev_w_out[0].astype(BF16), x, ln_mix_g[0], ln_mix_b[0], w_router, TOK_TM)
    x, xb = ffn(x, *routed, 0)

    u = matmul(xb, od_w_in[0].astype(BF16), MM_TM, d // 2)
    s5_w = (c_a_re[0], c_a_im[0], c_log_dt[0], c_b_re[0], c_b_im[0], c_c_re[0], c_c_im[0], c_d[0])
    zero_state = jnp.zeros((bp, C_GROUPS, C_STATE), F32)
    y_p, re_p, im_p = s5_group(u, zero_state, zero_state, s5_tables(*s5_w, 16),
                               nbatch=bp, seq=lp, chunk=16)
    y_s, re_s, im_s = s5_group(u[tp:], state_s5_re[0], state_s5_im[0], s5_tables(*s5_w, ss),
                               nbatch=bs, seq=ss, chunk=ss)
    y = jnp.concatenate([y_p, y_s], axis=0)
    z = gelu_glu(y, c_w_glu[0].astype(BF16), c_b_glu[0], TOK_TM)
    x, *routed = proj_ln([z], od_w_out[0].astype(BF16), x, ln_mix_g[1], ln_mix_b[1], w_router, TOK_TM)
    x, xb = ffn(x, *routed, 1)

    y_prompt = jnp.stack([x[b * lp + N_META:(b + 1) * lp] for b in range(bp)])
    y_sample = x[tp:].reshape(bs, ss, d)
    return (y_prompt, y_sample, gla_p[None], gla_s[None], rwkv_p[None], rwkv_s[None],
            shift_p[None], shift_s[None], re_p[None], re_s[None], im_p[None], im_s[None])
```
